```python
import jax
import jax.numpy as jnp
from jax import lax
import numpy as np

D_MODEL = 2048
BATCH = 4
SEQ = 4096
DEPTH = 2

GRID_W = 64
CTX_LEN = 256
ROPE_THETA = 10000.0
NORM_EPS = 1e-6
Q_BLOCK = 128

A_HEADS = 8
A_KV_HEADS = 2
A_HEAD_DIM = 128
B_HEADS = 4
B_Q_RANK = 512
B_KV_RANK = 256
B_NOPE = 128
B_ROPE = 64
B_V = 128
C_HEADS = 4
C_DK = 64
C_DV = 128
C_GATE_RANK = 16
C_GATE_NORM = 16.0
C_CHUNK = 64

MIX_WIDTH = A_HEADS * A_HEAD_DIM + B_HEADS * B_V + C_HEADS * C_DV
D_FF = -(-8 * D_MODEL // (3 * 256)) * 256

SPLIT_SIZES = (
    A_HEADS * A_HEAD_DIM,
    A_KV_HEADS * A_HEAD_DIM,
    A_KV_HEADS * A_HEAD_DIM,
    B_Q_RANK,
    B_KV_RANK,
    B_ROPE,
    C_HEADS * C_DK,
    C_HEADS * C_DK,
    C_HEADS * C_DV,
    C_HEADS * C_DV,
    C_GATE_RANK,
    C_GATE_RANK,
)
IN_COLS = sum(SPLIT_SIZES)

kernel_name = "hybrid_dit_gqa_mla_gla_block"


def _rms_norm(x, g):
    xf = x.astype(jnp.float32)
    y = xf * lax.rsqrt(jnp.mean(xf * xf, axis=-1, keepdims=True) + NORM_EPS)
    return (y * g.astype(jnp.float32)).astype(x.dtype)


def _grid_positions(n):
    rows = n // GRID_W
    row = jnp.repeat(jnp.arange(rows, dtype=jnp.int32), GRID_W)
    col = jnp.tile(jnp.arange(GRID_W, dtype=jnp.int32), rows)
    return row, col


def _rope_1d(x, pos):
    half = x.shape[-1] // 2
    inv_freq = ROPE_THETA ** (-jnp.arange(half, dtype=jnp.float32) / half)
    ang = pos.astype(jnp.float32)[:, None] * inv_freq[None, :]
    cos = jnp.cos(ang)[:, None, :]
    sin = jnp.sin(ang)[:, None, :]
    xf = x.astype(jnp.float32)
    x1, x2 = xf[..., :half], xf[..., half:]
    return jnp.concatenate([x1 * cos - x2 * sin, x1 * sin + x2 * cos], axis=-1).astype(x.dtype)


def _rope_2d(x, row, col):
    d = x.shape[-1]
    return jnp.concatenate([_rope_1d(x[..., : d // 2], row), _rope_1d(x[..., d // 2:], col)], axis=-1)


def _block_attention(q, k, v):
    bsz, nq, hk, grp, d = q.shape
    nb = nq // Q_BLOCK
    scale = d ** -0.5
    qb = q.reshape(bsz, nb, Q_BLOCK, hk, grp, d).transpose(1, 0, 2, 3, 4, 5)

    def one_block(qblk):
        s = jnp.einsum("bqkgd,bnkd->bkgqn", qblk, k, preferred_element_type=jnp.float32) * scale
        p = jax.nn.softmax(s, axis=-1)
        return jnp.einsum("bkgqn,bnke->bqkge", p.astype(v.dtype), v)

    o = lax.map(one_block, qb)
    return o.transpose(1, 0, 2, 3, 4, 5).reshape(bsz, nq, hk * grp, v.shape[-1])


def _gla_chunk_scan(q, k, v, log_a, s0):
    bsz, n, h, _ = q.shape
    nc = n // C_CHUNK

    def to_chunks(t):
        return t.reshape(bsz, nc, C_CHUNK, h, t.shape[-1]).transpose(1, 0, 3, 2, 4)

    mask = jnp.tril(jnp.ones((C_CHUNK, C_CHUNK), dtype=bool))[:, :, None]

    def step(state, inp):
        qc, kc, vc, gc = inp
        cb = jnp.cumsum(gc, axis=2)
        inter = jnp.einsum("bhcd,bhde->bhce", qc * jnp.exp(cb), state)
        diff = cb[:, :, :, None, :] - cb[:, :, None, :, :]
        decay = jnp.where(mask, jnp.exp(jnp.where(mask, diff, 0.0)), 0.0)
        att = jnp.einsum("bhid,bhijd,bhjd->bhij", qc, decay, kc)
        intra = jnp.einsum("bhij,bhje->bhie", att, vc)
        last = cb[:, :, -1, :]
        new_state = jnp.exp(last)[..., None] * state + jnp.einsum(
            "bhjd,bhje->bhde", kc * jnp.exp(last[:, :, None, :] - cb), vc)
        return new_state, inter + intra

    s_fin, o = lax.scan(step, s0, (to_chunks(q), to_chunks(k), to_chunks(v), to_chunks(log_a)))
    o = o.transpose(1, 0, 3, 2, 4).reshape(bsz, n, h, v.shape[-1])
    return o, s_fin


def _gla_bidirectional(lat, cx):
    zero = jnp.zeros((cx["c_q"].shape[0], C_HEADS, C_DK, C_DV), jnp.float32)
    flip = lambda t: t[:, ::-1]
    oc_f, s_f = _gla_chunk_scan(cx["c_q"], cx["c_k"], cx["c_v"], cx["c_af"], zero)
    ol_f, _ = _gla_chunk_scan(lat["c_q"], lat["c_k"], lat["c_v"], lat["c_af"], s_f)
    oc_b, s_b = _gla_chunk_scan(flip(cx["c_q"]), flip(cx["c_k"]), flip(cx["c_v"]), flip(cx["c_ab"]), zero)
    ol_b, _ = _gla_chunk_scan(flip(lat["c_q"]), flip(lat["c_k"]), flip(lat["c_v"]), flip(lat["c_ab"]), s_b)
    return ol_f + flip(ol_b), oc_f + flip(oc_b)


def _stream_proj(h, pos, p):
    bsz, n, _ = h.shape
    idx = np.cumsum(SPLIT_SIZES)[:-1].tolist()
    (a_q, a_k, a_v, b_cq, b_ckv, b_kr, c_q, c_k, c_v, c_g, c_gf, c_gb) = jnp.split(h @ p["w_in"], idx, axis=-1)
    a_q = _rms_norm(a_q.reshape(bsz, n, A_HEADS, A_HEAD_DIM), p["a_q_norm"])
    a_k = _rms_norm(a_k.reshape(bsz, n, A_KV_HEADS, A_HEAD_DIM), p["a_k_norm"])
    a_v = a_v.reshape(bsz, n, A_KV_HEADS, A_HEAD_DIM)
    qb = (_rms_norm(b_cq, p["b_q_lora_norm"]) @ p["w_uq"]).reshape(bsz, n, B_HEADS, B_NOPE + B_ROPE)
    kvb = (_rms_norm(b_ckv, p["b_kv_lora_norm"]) @ p["w_ukv"]).reshape(bsz, n, B_HEADS, B_NOPE + B_V)
    q_nope = _rms_norm(qb[..., :B_NOPE], p["b_q_nope_norm"])
    q_rope = _rms_norm(qb[..., B_NOPE:], p["b_q_rope_norm"])
    k_nope = _rms_norm(kvb[..., :B_NOPE], p["b_k_nope_norm"])
    b_v = kvb[..., B_NOPE:]
    k_rope = _rms_norm(b_kr.reshape(bsz, n, 1, B_ROPE), p["b_k_rope_norm"])
    if pos is not None:
        row, col = pos
        a_q = _rope_2d(a_q, row, col)
        a_k = _rope_2d(a_k, row, col)
        q_rope = _rope_2d(q_rope, row, col)
        k_rope = _rope_2d(k_rope, row, col)
    b_q = jnp.concatenate([q_nope, q_rope], axis=-1)
    b_k = jnp.concatenate([k_nope, jnp.broadcast_to(k_rope, (bsz, n, B_HEADS, B_ROPE))], axis=-1)
    f32 = jnp.float32
    c_q = c_q.reshape(bsz, n, C_HEADS, C_DK).astype(f32) * (C_DK ** -0.5)
    c_k = c_k.reshape(bsz, n, C_HEADS, C_DK).astype(f32)
    c_v = c_v.reshape(bsz, n, C_HEADS, C_DV).astype(f32)
    c_af = jax.nn.log_sigmoid((c_gf @ p["w_gk_f"] + p["b_gk_f"]).astype(f32)).reshape(bsz, n, C_HEADS, C_DK) / C_GATE_NORM
    c_ab = jax.nn.log_sigmoid((c_gb @ p["w_gk_b"] + p["b_gk_b"]).astype(f32)).reshape(bsz, n, C_HEADS, C_DK) / C_GATE_NORM
    return {"a_q": a_q, "a_k": a_k, "a_v": a_v, "b_q": b_q, "b_k": b_k, "b_v": b_v,
            "c_q": c_q, "c_k": c_k, "c_v": c_v, "c_g": c_g, "c_af": c_af, "c_ab": c_ab}


def _merge(o_a, o_b, o_c, g, p):
    bsz, n = o_a.shape[:2]
    o_c = _rms_norm(o_c, p["c_out_norm"]) * jax.nn.silu(g.reshape(bsz, n, C_HEADS, C_DV).astype(jnp.float32))
    z = jnp.concatenate([o_a.reshape(bsz, n, -1), o_b.reshape(bsz, n, -1),
                         o_c.reshape(bsz, n, -1).astype(o_a.dtype)], axis=-1)
    return z @ p["w_out"]


def _mixers(h, hc, row, col, p, ctx_out):
    bsz, n, _ = h.shape
    lat = _stream_proj(h, (row, col), p)
    cx = _stream_proj(hc, None, p)
    grp = A_HEADS // A_KV_HEADS
    cat = lambda a, b: jnp.concatenate([a, b], axis=1)
    o_a = _block_attention(lat["a_q"].reshape(bsz, n, A_KV_HEADS, grp, A_HEAD_DIM),
                           cat(cx["a_k"], lat["a_k"]), cat(cx["a_v"], lat["a_v"]))
    o_b = _block_attention(lat["b_q"][:, :, :, None, :], cat(cx["b_k"], lat["b_k"]), cat(cx["b_v"], lat["b_v"]))
    o_c, oc_c = _gla_bidirectional(lat, cx)
    y = _merge(o_a, o_b, o_c, lat["c_g"], p)
    if not ctx_out:
        return y, None
    lc = hc.shape[1]
    oc_a = _block_attention(cx["a_q"].reshape(hc.shape[0], lc, A_KV_HEADS, grp, A_HEAD_DIM), cx["a_k"], cx["a_v"])
    oc_b = _block_attention(cx["b_q"][:, :, :, None, :], cx["b_k"], cx["b_v"])
    yc = _merge(oc_a, oc_b, oc_c, cx["c_g"], p)
    return y, yc


def _modulation(cvec, w_mod, b_mod):
    m = (jax.nn.silu(cvec) @ w_mod + b_mod)[:, None, :]
    return jnp.split(m, 6, axis=-1)


def _swiglu(h, p):
    return (jax.nn.silu(h @ p["w_gate"]) * (h @ p["w_up"])) @ p["w_down"]


def _layer(x, xc, c, c_ctx, row, col, p, ctx_out):
    sh1, sc1, g1, sh2, sc2, g2 = _modulation(c, p["w_mod"], p["b_mod"])
    csh1, csc1, cg1, csh2, csc2, cg2 = _modulation(c_ctx[None, :], p["w_mod"], p["b_mod"])
    h = _rms_norm(x, p["norm1_g"]) * (1 + sc1) + sh1
    hc = _rms_norm(xc, p["norm1_g"]) * (1 + csc1) + csh1
    y, yc = _mixers(h, hc, row, col, p, ctx_out)
    x = x + g1 * y
    x = x + g2 * _swiglu(_rms_norm(x, p["norm2_g"]) * (1 + sc2) + sh2, p)
    if ctx_out:
        xc = xc + cg1 * yc
        xc = xc + cg2 * _swiglu(_rms_norm(xc, p["norm2_g"]) * (1 + csc2) + csh2, p)
    return x, xc


def setup_inputs(seed: int = 0) -> dict:
    key = jax.random.key(seed)
    ks = jax.random.split(key, 32)
    f32 = jnp.float32

    def nrm(k, shape, scale):
        return jax.random.normal(k, shape, f32) * scale

    def gain(k, shape):
        return 1.0 + 0.02 * jax.random.normal(k, shape, f32)

    L = DEPTH
    return {
        "x": nrm(ks[0], (BATCH, SEQ, D_MODEL), 1.0),
        "c": nrm(ks[1], (BATCH, D_MODEL), 1.0),
        "ctx": nrm(ks[2], (BATCH, CTX_LEN, D_MODEL), 1.0),
        "c_ctx": nrm(ks[3], (D_MODEL,), 1.0),
        "w_mod": nrm(ks[4], (L, D_MODEL, 6 * D_MODEL), D_MODEL ** -0.5),
        "b_mod": nrm(ks[5], (L, 6 * D_MODEL), 0.01),
        "norm1_g": gain(ks[6], (L, D_MODEL)),
        "norm2_g": gain(ks[7], (L, D_MODEL)),
        "w_in": nrm(ks[8], (L, D_MODEL, IN_COLS), D_MODEL ** -0.5),
        "a_q_norm": gain(ks[9], (L, A_HEAD_DIM)),
        "a_k_norm": gain(ks[10], (L, A_HEAD_DIM)),
        "b_q_lora_norm": gain(ks[11], (L, B_Q_RANK)),
        "b_kv_lora_norm": gain(ks[12], (L, B_KV_RANK)),
        "w_uq": nrm(ks[13], (L, B_Q_RANK, B_HEADS * (B_NOPE + B_ROPE)), B_Q_RANK ** -0.5),
        "w_ukv": nrm(ks[14], (L, B_KV_RANK, B_HEADS * (B_NOPE + B_V)), B_KV_RANK ** -0.5),
        "b_q_nope_norm": gain(ks[15], (L, B_NOPE)),
        "b_k_nope_norm": gain(ks[16], (L, B_NOPE)),
        "b_q_rope_norm": gain(ks[17], (L, B_ROPE)),
        "b_k_rope_norm": gain(ks[18], (L, B_ROPE)),
        "w_gk_f": nrm(ks[19], (L, C_GATE_RANK, C_HEADS * C_DK), C_GATE_RANK ** -0.5),
        "b_gk_f": nrm(ks[20], (L, C_HEADS * C_DK), 0.1),
        "w_gk_b": nrm(ks[21], (L, C_GATE_RANK, C_HEADS * C_DK), C_GATE_RANK ** -0.5),
        "b_gk_b": nrm(ks[22], (L, C_HEADS * C_DK), 0.1),
        "c_out_norm": gain(ks[23], (L, C_DV)),
        "w_out": nrm(ks[24], (L, MIX_WIDTH, D_MODEL), MIX_WIDTH ** -0.5),
        "w_gate": nrm(ks[25], (L, D_MODEL, D_FF), D_MODEL ** -0.5),
        "w_up": nrm(ks[26], (L, D_MODEL, D_FF), D_MODEL ** -0.5),
        "w_down": nrm(ks[27], (L, D_FF, D_MODEL), D_FF ** -0.5),
    }


def reference(x, c, ctx, c_ctx, w_mod, b_mod, norm1_g, norm2_g, w_in, a_q_norm, a_k_norm,
              b_q_lora_norm, b_kv_lora_norm, w_uq, w_ukv, b_q_nope_norm, b_k_nope_norm,
              b_q_rope_norm, b_k_rope_norm, w_gk_f, b_gk_f, w_gk_b, b_gk_b, c_out_norm,
              w_out, w_gate, w_up, w_down):
    row, col = _grid_positions(x.shape[1])
    xc = ctx
    for l in range(DEPTH):
        p = {
            "w_mod": w_mod[l], "b_mod": b_mod[l], "norm1_g": norm1_g[l], "norm2_g": norm2_g[l],
            "w_in": w_in[l], "a_q_norm": a_q_norm[l], "a_k_norm": a_k_norm[l],
            "b_q_lora_norm": b_q_lora_norm[l], "b_kv_lora_norm": b_kv_lora_norm[l],
            "w_uq": w_uq[l], "w_ukv": w_ukv[l], "b_q_nope_norm": b_q_nope_norm[l],
            "b_k_nope_norm": b_k_nope_norm[l], "b_q_rope_norm": b_q_rope_norm[l],
            "b_k_rope_norm": b_k_rope_norm[l], "w_gk_f": w_gk_f[l], "b_gk_f": b_gk_f[l],
            "w_gk_b": w_gk_b[l], "b_gk_b": b_gk_b[l], "c_out_norm": c_out_norm[l],
            "w_out": w_out[l], "w_gate": w_gate[l], "w_up": w_up[l], "w_down": w_down[l],
        }
        x, xc = _layer(x, xc, c, c_ctx, row, col, p, l < DEPTH - 1)
    return x
```

```python
import functools

import numpy as np
import jax
import jax.numpy as jnp
from jax import lax
from jax.experimental import pallas as pl
from jax.experimental.pallas import tpu as pltpu

F32 = jnp.float32
BF16 = jnp.bfloat16

D_MODEL = 2048
GRID_W = 64
ROPE_THETA = 10000.0
NORM_EPS = 1e-6

A_HEADS = 8
A_KV_HEADS = 2
A_HEAD_DIM = 128
B_HEADS = 4
B_Q_RANK = 512
B_KV_RANK = 256
B_NOPE = 128
B_ROPE = 64
B_V = 128
B_QK_PAD = 256
C_HEADS = 4
C_DK = 64
C_DV = 128
C_GATE_RANK = 16
C_GATE_NORM = 16.0

D_FF = 5632
P_COLS = 4096

COL_AQ = 0
COL_AK = 1024
COL_AV = 1280
COL_BCQ = 1536
COL_BCKV = 2048
COL_BKR = 2304
COL_GATE = 2432
COL_CQ = 2560
COL_CK = 2816
COL_CV = 3072
COL_CG = 3584

GLA_CHUNK = 128

VMEM_LIMIT = 56 * 2**20


def _cparams(sem, vmem=VMEM_LIMIT):
    return pltpu.CompilerParams(dimension_semantics=sem, vmem_limit_bytes=vmem)


def _silu(x):
    return x / (1.0 + jnp.exp(-x))


def _rms(x, g):
    ms = jnp.mean(x * x, axis=-1, keepdims=True)
    return x * lax.rsqrt(ms + NORM_EPS) * g


def _mod_kernel(c_ref, w_ref, b_ref, o_ref):
    s = _silu(c_ref[...]).astype(BF16)
    o_ref[...] = jnp.dot(s, w_ref[...].astype(BF16), preferred_element_type=F32) + b_ref[...]


def _modulation(cpad, w_mod, b_mod):
    d = cpad.shape[1]
    n = w_mod.shape[1]
    tn = 1024
    out = pl.pallas_call(
        _mod_kernel,
        out_shape=jax.ShapeDtypeStruct((8, n), F32),
        grid=(n // tn,),
        in_specs=[pl.BlockSpec((8, d), lambda j: (0, 0)),
                  pl.BlockSpec((d, tn), lambda j: (0, j)),
                  pl.BlockSpec((1, tn), lambda j: (0, j))],
        out_specs=pl.BlockSpec((8, tn), lambda j: (0, j)),
        compiler_params=_cparams(("arbitrary",)),
        name="modulation",
    )(cpad, w_mod, b_mod.reshape(1, n))
    return out.reshape(8, 6, d)


def _mod_index(ctx):
    if ctx:
        return lambda b, i, *_: (4, 0, 0)
    return lambda b, i, *_: (b, 0, 0)


def _inproj_kernel(x_ref, mod_ref, g_ref, w_ref, o_ref, h_ref):
    @pl.when(pl.program_id(2) == 0)
    def _():
        y = _rms(x_ref[0], g_ref[...])
        h_ref[...] = (y * (1.0 + mod_ref[0, 1:2, :]) + mod_ref[0, 0:1, :]).astype(BF16)

    o_ref[0] = jnp.dot(h_ref[...], w_ref[...], preferred_element_type=F32)


def _inproj(x, mod, g, w, ctx):
    bsz, n, d = x.shape
    tm = min(512, n)
    tn = 1024
    return pl.pallas_call(
        _inproj_kernel,
        out_shape=jax.ShapeDtypeStruct((bsz, n, P_COLS), F32),
        grid=(bsz, n // tm, P_COLS // tn),
        in_specs=[pl.BlockSpec((1, tm, d), lambda b, i, j: (b, i, 0)),
                  pl.BlockSpec((1, 6, d), _mod_index(ctx)),
                  pl.BlockSpec((1, d), lambda b, i, j: (0, 0)),
                  pl.BlockSpec((d, tn), lambda b, i, j: (0, j))],
        out_specs=pl.BlockSpec((1, tm, tn), lambda b, i, j: (b, i, j)),
        scratch_shapes=[pltpu.VMEM((tm, d), BF16)],
        compiler_params=_cparams(("parallel", "parallel", "arbitrary")),
        name="inproj",
    )(x, mod, g.reshape(1, d), w)


def _swap_halves(x, lane, width):
    return jnp.where((lane // width) % 2 == 0,
                     pltpu.roll(x, 128 - width, 1), pltpu.roll(x, width, 1))


def _post_kernel(*refs, rope):
    if rope:
        (pq_ref, pkv_ref, pcq_ref, pmix_ref, cosa_ref, sina_ref, cosb_ref, sinb_ref,
         gaq_ref, gak_ref, gql_ref, gkvl_ref, gqn_ref, gkn_ref, gqr_ref, gkr_ref,
         wuq_ref, wukv_ref, wgk_ref, bgk_ref,
         aq_ref, ak_ref, av_ref, bq_ref, bk_ref, bv_ref, gd_ref) = refs
    else:
        (pq_ref, pkv_ref, pcq_ref, pmix_ref,
         gaq_ref, gak_ref, gql_ref, gkvl_ref, gqn_ref, gkn_ref, gqr_ref, gkr_ref,
         wuq_ref, wukv_ref, wgk_ref, bgk_ref,
         aq_ref, ak_ref, av_ref, bq_ref, bk_ref, bv_ref, gd_ref) = refs
    tm = pq_ref.shape[1]
    lane = lax.broadcasted_iota(jnp.int32, (tm, 128), 1)
    lo = lane < 64

    def rope_a(x):
        if not rope:
            return x
        return x * cosa_ref[...] + _swap_halves(x, lane, 32) * sina_ref[...]

    def rope_b(x):
        if not rope:
            return x
        return x * cosb_ref[...] + _swap_halves(x, lane, 16) * sinb_ref[...]

    a_scale = A_HEAD_DIM ** -0.5
    for h in range(A_HEADS):
        x = pq_ref[0, :, h * 128:(h + 1) * 128]
        aq_ref[0, :, h * 128:(h + 1) * 128] = (rope_a(_rms(x, gaq_ref[...])) * a_scale).astype(BF16)
    for h in range(A_KV_HEADS):
        x = pkv_ref[0, :, h * 128:(h + 1) * 128]
        ak_ref[0, :, h * 128:(h + 1) * 128] = rope_a(_rms(x, gak_ref[...])).astype(BF16)
    av_ref[0] = pkv_ref[0, :, 256:512].astype(BF16)

    b_scale = (B_NOPE + B_ROPE) ** -0.5
    cq = _rms(pcq_ref[0], gql_ref[...]).astype(BF16)
    qb = jnp.dot(cq, wuq_ref[...], preferred_element_type=F32)
    for h in range(B_HEADS):
        x = qb[:, h * 128:(h + 1) * 128]
        bq_ref[0, :, h * B_QK_PAD:h * B_QK_PAD + 128] = (_rms(x, gqn_ref[...]) * b_scale).astype(BF16)
    for p in range(B_HEADS // 2):
        r = qb[:, 512 + p * 128:512 + (p + 1) * 128]
        sq = r * r
        ms_lo = jnp.sum(jnp.where(lo, sq, 0.0), axis=-1, keepdims=True)
        ms_hi = jnp.sum(jnp.where(lo, 0.0, sq), axis=-1, keepdims=True)
        ms = jnp.where(lo, ms_lo, ms_hi) * (1.0 / B_ROPE)
        rn = rope_b(r * lax.rsqrt(ms + NORM_EPS) * gqr_ref[...]) * b_scale
        h0 = 2 * p
        bq_ref[0, :, h0 * B_QK_PAD + 128:(h0 + 1) * B_QK_PAD] = jnp.where(lo, rn, 0.0).astype(BF16)
        bq_ref[0, :, (h0 + 1) * B_QK_PAD + 128:(h0 + 2) * B_QK_PAD] = jnp.where(lo, 0.0, rn).astype(BF16)

    ckv = _rms(pmix_ref[0, :, 0:256], gkvl_ref[...]).astype(BF16)
    kv = jnp.dot(ckv, wukv_ref[...], preferred_element_type=F32)
    kr = pmix_ref[0, :, 256:384]
    krn = rope_b(_rms(kr, gkr_ref[...]))
    kr_even = jnp.where(lo, krn, 0.0).astype(BF16)
    kr_odd = jnp.where(lo, 0.0, krn).astype(BF16)
    for h in range(B_HEADS):
        x = kv[:, h * 128:(h + 1) * 128]
        bk_ref[0, :, h * B_QK_PAD:h * B_QK_PAD + 128] = _rms(x, gkn_ref[...]).astype(BF16)
        bk_ref[0, :, h * B_QK_PAD + 128:(h + 1) * B_QK_PAD] = kr_even if h % 2 == 0 else kr_odd
    bv_ref[0] = kv[:, 512:1024].astype(BF16)

    z = jnp.dot(pmix_ref[0, :, 384:512], wgk_ref[...], preferred_element_type=F32,
                precision=lax.Precision.HIGHEST) + bgk_ref[...]
    gd_ref[0] = (jnp.minimum(z, 0.0) - jnp.log(1.0 + jnp.exp(-jnp.abs(z)))) * (1.0 / C_GATE_NORM)


def _post(proj, tabs, lw, rope):
    bsz, n, _ = proj.shape
    tm = min(512, n)
    row = lambda b, i: (b, i, 0)
    const = lambda b, i: (0, 0)
    in_specs = [pl.BlockSpec((1, tm, 1024), lambda b, i: (b, i, COL_AQ // 1024)),
                pl.BlockSpec((1, tm, 512), lambda b, i: (b, i, COL_AK // 512)),
                pl.BlockSpec((1, tm, 512), lambda b, i: (b, i, COL_BCQ // 512)),
                pl.BlockSpec((1, tm, 512), lambda b, i: (b, i, COL_BCKV // 512))]
    args = [proj, proj, proj, proj]
    if rope:
        in_specs += [pl.BlockSpec((tm, 128), lambda b, i: (i, 0))] * 4
        args += list(tabs)
    small = [lw["a_q_norm"], lw["a_k_norm"], lw["b_q_lora_norm"], lw["b_kv_lora_norm"],
             lw["b_q_nope_norm"], lw["b_k_nope_norm"], lw["b_q_rope_norm2"], lw["b_k_rope_norm2"],
             lw["w_uq"], lw["w_ukv"], lw["w_gk"], lw["b_gk"]]
    in_specs += [pl.BlockSpec(a.shape, const) for a in small]
    args += small
    out_shape = [jax.ShapeDtypeStruct((bsz, n, 1024), BF16),
                 jax.ShapeDtypeStruct((bsz, n, 256), BF16),
                 jax.ShapeDtypeStruct((bsz, n, 256), BF16),
                 jax.ShapeDtypeStruct((bsz, n, B_HEADS * B_QK_PAD), BF16),
                 jax.ShapeDtypeStruct((bsz, n, B_HEADS * B_QK_PAD), BF16),
                 jax.ShapeDtypeStruct((bsz, n, B_HEADS * B_V), BF16),
                 jax.ShapeDtypeStruct((bsz, n, 512), F32)]
    out_specs = [pl.BlockSpec((1, tm, s.shape[2]), row) for s in out_shape]
    return pl.pallas_call(
        functools.partial(_post_kernel, rope=rope),
        out_shape=out_shape,
        grid=(bsz, n // tm),
        in_specs=in_specs,
        out_specs=out_specs,
        compiler_params=_cparams(("parallel", "parallel")),
        name="post_lat" if rope else "post_ctx",
    )(*args)


def _attn_kernel(*refs, groups, dq, dv, tk, has_lat):
    if has_lat:
        q_ref, kc_ref, vc_ref, kl_ref, vl_ref, o_ref = refs
    else:
        q_ref, kc_ref, vc_ref, o_ref = refs
    tq = q_ref.shape[1]
    if groups > 1:
        q = jnp.concatenate([q_ref[0, :, g * dq:(g + 1) * dq] for g in range(groups)], axis=0)
    else:
        q = q_ref[0]
    m_rows = groups * tq

    def update(k, v, m, l, acc):
        s = lax.dot_general(q, k, (((1,), (1,)), ((), ())), preferred_element_type=F32)
        m_new = jnp.maximum(m, jnp.max(s, axis=-1, keepdims=True))
        p = jnp.exp(s - m_new)
        a = jnp.exp(m - m_new)
        l = a * l + jnp.sum(p, axis=-1, keepdims=True)
        acc = a * acc + jnp.dot(p.astype(BF16), v, preferred_element_type=F32)
        return m_new, l, acc

    carry = (jnp.full((m_rows, 1), -jnp.inf, F32), jnp.zeros((m_rows, 1), F32), jnp.zeros((m_rows, dv), F32))
    carry = update(kc_ref[0], vc_ref[0], *carry)
    if has_lat:
        def body(c, carry):
            off = pl.multiple_of(c * tk, tk)
            return update(kl_ref[0, pl.ds(off, tk), :], vl_ref[0, pl.ds(off, tk), :], *carry)

        carry = lax.fori_loop(0, kl_ref.shape[1] // tk, body, carry)
    _, l, acc = carry
    out = acc * (1.0 / l)
    for g in range(groups):
        o_ref[0, :, g * dv:(g + 1) * dv] = out[g * tq:(g + 1) * tq].astype(o_ref.dtype)


def _attention(q, kc, vc, kl, vl, *, kv_heads, groups, dq, dv, tq, tk=512):
    bsz, n, _ = q.shape
    nc = kc.shape[1]
    has_lat = kl is not None
    tq = min(tq, n)
    in_specs = [pl.BlockSpec((1, tq, groups * dq), lambda b, h, i: (b, i, h)),
                pl.BlockSpec((1, nc, dq), lambda b, h, i: (b, 0, h)),
                pl.BlockSpec((1, nc, dv), lambda b, h, i: (b, 0, h))]
    args = [q, kc, vc]
    if has_lat:
        nl = kl.shape[1]
        in_specs += [pl.BlockSpec((1, nl, dq), lambda b, h, i: (b, 0, h)),
                     pl.BlockSpec((1, nl, dv), lambda b, h, i: (b, 0, h))]
        args += [kl, vl]
    return pl.pallas_call(
        functools.partial(_attn_kernel, groups=groups, dq=dq, dv=dv, tk=tk, has_lat=has_lat),
        out_shape=jax.ShapeDtypeStruct((bsz, n, kv_heads * groups * dv), BF16),
        grid=(bsz, kv_heads, n // tq),
        in_specs=in_specs,
        out_specs=pl.BlockSpec((1, tq, groups * dv), lambda b, h, i: (b, i, h)),
        compiler_params=_cparams(("parallel", "parallel", "arbitrary")),
        name="attention",
    )(*args)


def _gla_kernel(q_ref, k_ref, v_ref, g_ref, s0_ref, o_ref, sfin_ref, st_ref, *, chunk):
    d = pl.program_id(0)
    i = pl.program_id(2)
    t_rows = q_ref.shape[1]
    nch = t_rows // chunk

    @pl.when(i == 0)
    def _():
        st_ref[...] = s0_ref[0, 0]

    row = lax.broadcasted_iota(jnp.int32, (chunk, chunk), 0)
    col = lax.broadcasted_iota(jnp.int32, (chunk, chunk), 1)
    tri = jnp.where(d == 0, col - row, row - col) <= 0
    tri_bf = jnp.where(tri, 1.0, 0.0).astype(BF16)
    row2 = lax.broadcasted_iota(jnp.int32, (chunk, 2 * chunk), 0)
    col2 = lax.broadcasted_iota(jnp.int32, (chunk, 2 * chunk), 1)
    col2 = jnp.where(col2 >= chunk, col2 - chunk, col2)
    tri2 = jnp.where(d == 0, col2 - row2, row2 - col2) <= 0
    lo =lax.broadcasted_iota(jnp.int32, (chunk, 128), 1) < C_DK
    vlo = lax.broadcasted_iota(jnp.int32, (chunk, 2 * C_DV), 1) < C_DV
    srow = lax.broadcasted_iota(jnp.int32, (2 * C_DV, 2 * C_DK), 0) // C_DV
    scol = lax.broadcasted_iota(jnp.int32, (2 * C_DV, 2 * C_DK), 1) // C_DK
    same_head = srow == scol
    nt = (((1,), (1,)), ((), ()))
    tn = (((0,), (0,)), ((), ()))

    for c in range(nch):
        cc = c + d * (nch - 1 - 2 * c)
        r0 = pl.multiple_of(cc * chunk, chunk)
        g = g_ref[0, pl.ds(r0, chunk), :]
        g_hi = g.astype(BF16)
        g_lo = (g - g_hi.astype(F32)).astype(BF16)
        cb = (jnp.dot(tri_bf, g_hi, preferred_element_type=F32)
              + jnp.dot(tri_bf, g_lo, preferred_element_type=F32))
        tot = jnp.sum(g, axis=0, keepdims=True)
        q = q_ref[0, pl.ds(r0, chunk), :] * (C_DK ** -0.5)
        k = k_ref[0, pl.ds(r0, chunk), :]
        v = v_ref[0, pl.ds(r0, chunk), :].astype(BF16)
        qe = (q * jnp.exp(cb)).astype(BF16)
        kd = (k * jnp.exp(tot - cb)).astype(BF16)
        ke = (k * jnp.exp(-cb)).astype(BF16)
        dec = jnp.exp(tot)
        for p in range(C_HEADS // 2):
            ls = slice(p * 128, (p + 1) * 128)
            vs = slice(p * 2 * C_DV, (p + 1) * 2 * C_DV)
            qe_p, kd_p, ke_p, v_p = qe[:, ls], kd[:, ls], ke[:, ls], v[:, vs]
            zk = jnp.zeros_like(ke_p)
            ke_bd = jnp.concatenate([jnp.where(lo, ke_p, zk), jnp.where(lo, zk, ke_p)], axis=0)
            att = lax.dot_general(qe_p, ke_bd, nt, preferred_element_type=F32)
            att = jnp.where(tri2, att, 0.0).astype(BF16)
            zv = jnp.zeros_like(v_p)
            v_bd = jnp.concatenate([jnp.where(vlo, v_p, zv), jnp.where(vlo, zv, v_p)], axis=0)
            st = st_ref[p]
            o = (jnp.dot(att, v_bd, preferred_element_type=F32)
                 + lax.dot_general(qe_p, st.astype(BF16), nt, preferred_element_type=F32))
            o_ref[0, 0, pl.ds(r0, chunk), vs] = o
            u = lax.dot_general(v_p, kd_p, tn, preferred_element_type=F32)
            st_ref[p] = dec[:, ls] * st + jnp.where(same_head, u, 0.0)

    @pl.when(i == pl.num_programs(2) - 1)
    def _():
        sfin_ref[0, 0] = st_ref[...]


def _gla(proj, gdec, s0):
    bsz, n, _ = proj.shape
    t_rows = min(512, n)
    nb = n // t_rows
    rb = lambda d, i: i + d * (nb - 1 - 2 * i)
    o, sfin = pl.pallas_call(
        functools.partial(_gla_kernel, chunk=GLA_CHUNK),
        out_shape=[jax.ShapeDtypeStruct((2, bsz, n, C_HEADS * C_DV), F32),
                   jax.ShapeDtypeStruct(s0.shape, F32)],
        grid=(2, bsz, nb),
        in_specs=[pl.BlockSpec((1, t_rows, 256), lambda d, b, i: (b, rb(d, i), COL_CQ // 256)),
                  pl.BlockSpec((1, t_rows, 256), lambda d, b, i: (b, rb(d, i), COL_CK // 256)),
                  pl.BlockSpec((1, t_rows, 512), lambda d, b, i: (b, rb(d, i), COL_CV // 512)),
                  pl.BlockSpec((1, t_rows, 256), lambda d, b, i: (b, rb(d, i), d)),
                  pl.BlockSpec((1, 1, 2, 256, 128), lambda d, b, i: (d, b, 0, 0, 0))],
        out_specs=[pl.BlockSpec((1, 1, t_rows, 512), lambda d, b, i: (d, b, rb(d, i), 0)),
                   pl.BlockSpec((1, 1, 2, 256, 128), lambda d, b, i: (d, b, 0, 0, 0))],
        scratch_shapes=[pltpu.VMEM((2, 256, 128), F32)],
        compiler_params=_cparams(("parallel", "parallel", "arbitrary")),
        name="gla",
    )(proj, proj, proj, gdec, s0)
    return o, sfin


def _outproj_kernel(oa_ref, ob_ref, oc_ref, cg_ref, gn_ref, w_ref, x_ref, mod_ref, o_ref, z_ref):
    z_ref[:, 0:1024] = oa_ref[0]
    z_ref[:, 1024:1536] = ob_ref[0]
    for h in range(C_HEADS):
        hs = slice(h * C_DV, (h + 1) * C_DV)
        oc = oc_ref[0, 0, :, hs] + oc_ref[1, 0, :, hs]
        z_ref[:, 1536 + h * C_DV:1536 + (h + 1) * C_DV] = (
            _rms(oc, gn_ref[...]) * _silu(cg_ref[0, :, hs])).astype(BF16)
    y = jnp.dot(z_ref[...], w_ref[...], preferred_element_type=F32)
    o_ref[0] = x_ref[0] + mod_ref[0, 2:3, :] * y


def _outproj(oa, ob, oc, proj, gn, w, x, mod, ctx):
    bsz, n, d = x.shape
    tm = min(512, n)
    return pl.pallas_call(
        _outproj_kernel,
        out_shape=jax.ShapeDtypeStruct((bsz, n, d), F32),
        grid=(bsz, n // tm),
        in_specs=[pl.BlockSpec((1, tm, 1024), lambda b, i: (b, i, 0)),
                  pl.BlockSpec((1, tm, 512), lambda b, i: (b, i, 0)),
                  pl.BlockSpec((2, 1, tm, 512), lambda b, i: (0, b, i, 0)),
                  pl.BlockSpec((1, tm, 512), lambda b, i: (b, i, COL_CG // 512)),
                  pl.BlockSpec((1, C_DV), lambda b, i: (0, 0)),
                  pl.BlockSpec(w.shape, lambda b, i: (0, 0)),
                  pl.BlockSpec((1, tm, d), lambda b, i: (b, i, 0)),
                  pl.BlockSpec((1, 6, d), _mod_index(ctx))],
        out_specs=pl.BlockSpec((1, tm, d), lambda b, i: (b, i, 0)),
        scratch_shapes=[pltpu.VMEM((tm, w.shape[0]), BF16)],
        compiler_params=_cparams(("parallel", "parallel")),
        name="outproj",
    )(oa, ob, oc, proj, gn, w, x, mod)


def _ffn_kernel(x_ref, mod_ref, g_ref, wg_ref, wu_ref, wd_ref, o_ref, h_ref, acc_ref):
    j = pl.program_id(2)

    @pl.when(j == 0)
    def _():
        y = _rms(x_ref[0], g_ref[...])
        h_ref[...] = (y * (1.0 + mod_ref[0, 4:5, :]) + mod_ref[0, 3:4, :]).astype(BF16)
        acc_ref[...] = jnp.zeros_like(acc_ref)

    h = h_ref[...]
    a = jnp.dot(h, wg_ref[...], preferred_element_type=F32)
    u = jnp.dot(h, wu_ref[...], preferred_element_type=F32)
    t = (_silu(a) * u).astype(BF16)
    acc_ref[...] += jnp.dot(t, wd_ref[...], preferred_element_type=F32)

    @pl.when(j == pl.num_programs(2) - 1)
    def _():
        o_ref[0] = x_ref[0] + mod_ref[0, 5:6, :] * acc_ref[...]


def _ffn(x, mod, g, wg, wu, wd, ctx):
    bsz, n, d = x.shape
    f = wg.shape[1]
    tm = min(512, n)
    tf = 512
    return pl.pallas_call(
        _ffn_kernel,
        out_shape=jax.ShapeDtypeStruct((bsz, n, d), F32),
        grid=(bsz, n // tm, f // tf),
        in_specs=[pl.BlockSpec((1, tm, d), lambda b, i, j: (b, i, 0)),
                  pl.BlockSpec((1, 6, d), _mod_index(ctx)),
                  pl.BlockSpec((1, d), lambda b, i, j: (0, 0)),
                  pl.BlockSpec((d, tf), lambda b, i, j: (0, j)),
                  pl.BlockSpec((d, tf), lambda b, i, j: (0, j)),
                  pl.BlockSpec((tf, d), lambda b, i, j: (j, 0))],
        out_specs=pl.BlockSpec((1, tm, d), lambda b, i, j: (b, i, 0)),
        scratch_shapes=[pltpu.VMEM((tm, d), BF16), pltpu.VMEM((tm, d), F32)],
        compiler_params=_cparams(("parallel", "parallel", "arbitrary")),
        name="ffn",
    )(x, mod, g.reshape(1, d), wg, wu, wd)


def _rope_tables(n):
    pos = np.arange(n)
    row = (pos // GRID_W).astype(np.float64)[:, None]
    col = (pos % GRID_W).astype(np.float64)[:, None]

    def table(half):
        inv = ROPE_THETA ** (-np.arange(half, dtype=np.float64) / half)
        ar, ac = row * inv[None, :], col * inv[None, :]
        cos = np.concatenate([np.cos(ar), np.cos(ar), np.cos(ac), np.cos(ac)], axis=1)
        sin = np.concatenate([-np.sin(ar), np.sin(ar), -np.sin(ac), np.sin(ac)], axis=1)
        return cos, sin

    cos_a, sin_a = table(A_HEAD_DIM // 4)
    cos_b, sin_b = table(B_ROPE // 4)
    cos_b, sin_b = np.tile(cos_b, (1, 2)), np.tile(sin_b, (1, 2))
    return tuple(jnp.asarray(t, F32) for t in (cos_a, sin_a, cos_b, sin_b))


def _layer_weights(l, w_in, w_uq, w_ukv, w_gk_f, b_gk_f, w_gk_b, b_gk_b, w_out, w_gate, w_up, w_down, smalls):
    w = w_in[l]
    d = w.shape[0]
    w_r = jnp.concatenate([
        w[:, 0:2304],
        w[:, 2304:2368], w[:, 2304:2368],
        w[:, 3904:3936], jnp.zeros((d, 96), w.dtype),
        w[:, 2368:3904],
    ], axis=1).astype(BF16)
    uq = w_uq[l].reshape(B_Q_RANK, B_HEADS, B_NOPE + B_ROPE)
    uq = jnp.concatenate([uq[:, :, :B_NOPE].reshape(B_Q_RANK, -1), uq[:, :, B_NOPE:].reshape(B_Q_RANK, -1)], axis=1)
    ukv = w_ukv[l].reshape(B_KV_RANK, B_HEADS, B_NOPE + B_V)
    ukv = jnp.concatenate([ukv[:, :, :B_NOPE].reshape(B_KV_RANK, -1), ukv[:, :, B_NOPE:].reshape(B_KV_RANK, -1)], axis=1)
    nk = C_HEADS * C_DK
    w_gk = jnp.zeros((128, 2 * nk), F32)
    w_gk = w_gk.at[0:C_GATE_RANK, 0:nk].set(w_gk_f[l]).at[C_GATE_RANK:2 * C_GATE_RANK, nk:].set(w_gk_b[l])
    lw = {k: v[l].reshape(1, -1) for k, v in smalls.items()}
    lw["b_q_rope_norm2"] = jnp.tile(lw["b_q_rope_norm"], (1, 2))
    lw["b_k_rope_norm2"] = jnp.tile(lw["b_k_rope_norm"], (1, 2))
    lw.update(w_in=w_r, w_uq=uq.astype(BF16), w_ukv=ukv.astype(BF16), w_gk=w_gk,
              b_gk=jnp.concatenate([b_gk_f[l], b_gk_b[l]]).reshape(1, -1),
              w_out=w_out[l].astype(BF16), w_gate=w_gate[l].astype(BF16),
              w_up=w_up[l].astype(BF16), w_down=w_down[l].astype(BF16))
    return lw


def kernel(x, c, ctx, c_ctx, w_mod, b_mod, norm1_g, norm2_g, w_in, a_q_norm, a_k_norm, b_q_lora_norm, b_kv_lora_norm, w_uq, w_ukv, b_q_nope_norm, b_k_nope_norm, b_q_rope_norm, b_k_rope_norm, w_gk_f, b_gk_f, w_gk_b, b_gk_b, c_out_norm, w_out, w_gate, w_up, w_down):
    bsz, n, d = x.shape
    depth = w_in.shape[0]
    tabs = _rope_tables(n)
    cpad = jnp.concatenate([c, c_ctx[None, :], jnp.zeros((8 - bsz - 1, d), F32)], axis=0)
    smalls = dict(a_q_norm=a_q_norm, a_k_norm=a_k_norm, b_q_lora_norm=b_q_lora_norm,
                  b_kv_lora_norm=b_kv_lora_norm, b_q_nope_norm=b_q_nope_norm, b_k_nope_norm=b_k_nope_norm,
                  b_q_rope_norm=b_q_rope_norm, b_k_rope_norm=b_k_rope_norm, c_out_norm=c_out_norm)
    s_zero = jnp.zeros((2, bsz, 2, 2 * C_DV, 2 * C_DK), F32)
    xc = ctx
    for l in range(depth):
        ctx_out = l < depth - 1
        lw = _layer_weights(l, w_in, w_uq, w_ukv, w_gk_f, b_gk_f, w_gk_b, b_gk_b,
                            w_out, w_gate, w_up, w_down, smalls)
        mod = _modulation(cpad, w_mod[l], b_mod[l])

        pc = _inproj(xc, mod, norm1_g[l], lw["w_in"], True)
        pl_ = _inproj(x, mod, norm1_g[l], lw["w_in"], False)
        caq, cak, cav, cbq, cbk, cbv, cgd = _post(pc, None, lw, False)
        laq, lak, lav, lbq, lbk, lbv, lgd = _post(pl_, tabs, lw, True)

        o_a = _attention(laq, cak, cav, lak, lav, kv_heads=A_KV_HEADS, groups=A_HEADS // A_KV_HEADS,
                         dq=A_HEAD_DIM, dv=A_HEAD_DIM, tq=256)
        o_b = _attention(lbq, cbk, cbv, lbk, lbv, kv_heads=B_HEADS, groups=1,
                         dq=B_QK_PAD, dv=B_V, tq=512)
        oc_c, s_ctx = _gla(pc, cgd, s_zero)
        o_c, _ = _gla(pl_, lgd, s_ctx)

        x = _outproj(o_a, o_b, o_c, pl_, lw["c_out_norm"], lw["w_out"], x, mod, False)
        x = _ffn(x, mod, norm2_g[l], lw["w_gate"], lw["w_up"], lw["w_down"], False)
        if ctx_out:
            oc_a = _attention(caq, cak, cav, None, None, kv_heads=A_KV_HEADS, groups=A_HEADS // A_KV_HEADS,
                              dq=A_HEAD_DIM, dv=A_HEAD_DIM, tq=256)
            oc_b = _attention(cbq, cbk, cbv, None, None, kv_heads=B_HEADS, groups=1,
                              dq=B_QK_PAD, dv=B_V, tq=256)
            xc = _outproj(oc_a, oc_b, oc_c, pc, lw["c_out_norm"], lw["w_out"], xc, mod, True)
            xc = _ffn(xc, mod, norm2_g[l], lw["w_gate"], lw["w_up"], lw["w_down"], True)
    return x
```

```python
import functools

import numpy as np
import jax
import jax.numpy as jnp
from jax import lax
from jax.experimental import pallas as pl
from jax.experimental.pallas import tpu as pltpu

F32 = jnp.float32
BF16 = jnp.bfloat16

D_MODEL = 2048
GRID_W = 64
ROPE_THETA = 10000.0
NORM_EPS = 1e-6

A_HEADS = 8
A_KV_HEADS = 2
A_HEAD_DIM = 128
B_HEADS = 4
B_Q_RANK = 512
B_KV_RANK = 256
B_NOPE = 128
B_ROPE = 64
B_V = 128
B_QK_PAD = 256
C_HEADS = 4
C_DK = 64
C_DV = 128
C_GATE_RANK = 16
C_GATE_NORM = 16.0

D_FF = 5632
P_COLS = 4096

COL_AQ = 0
COL_AK = 1024
COL_AV = 1280
COL_BCQ = 1536
COL_BCKV = 2048
COL_BKR = 2304
COL_GATE = 2432
COL_CQ = 2560
COL_CK = 2816
COL_CV = 3072
COL_CG = 3584

GLA_CHUNK = 128

VMEM_LIMIT = 56 * 2**20
LOG2E = 1.4426950408889634


def _cparams(sem, vmem=VMEM_LIMIT):
    return pltpu.CompilerParams(dimension_semantics=sem, vmem_limit_bytes=vmem)


def _silu(x):
    return x / (1.0 + jnp.exp(-x))


def _rms(x, g):
    ms = jnp.mean(x * x, axis=-1, keepdims=True)
    return x * lax.rsqrt(ms + NORM_EPS) * g


def _mod_kernel(c_ref, w_ref, b_ref, o_ref):
    s = _silu(c_ref[...]).astype(BF16)
    o_ref[...] = jnp.dot(s, w_ref[...].astype(BF16), preferred_element_type=F32) + b_ref[...]


def _modulation(cpad, w_mod, b_mod):
    d = cpad.shape[1]
    n = w_mod.shape[1]
    tn = 1024
    out = pl.pallas_call(
        _mod_kernel,
        out_shape=jax.ShapeDtypeStruct((8, n), F32),
        grid=(n // tn,),
        in_specs=[pl.BlockSpec((8, d), lambda j: (0, 0)),
                  pl.BlockSpec((d, tn), lambda j: (0, j)),
                  pl.BlockSpec((1, tn), lambda j: (0, j))],
        out_specs=pl.BlockSpec((8, tn), lambda j: (0, j)),
        compiler_params=_cparams(("arbitrary",)),
        name="modulation",
    )(cpad, w_mod, b_mod.reshape(1, n))
    return out.reshape(8, 6, d)


def _mod_index(ctx):
    if ctx:
        return lambda b, i, *_: (4, 0, 0)
    return lambda b, i, *_: (b, 0, 0)


def _inproj_kernel(x_ref, mod_ref, g_ref, w_ref, o_ref, h_ref):
    @pl.when(pl.program_id(2) == 0)
    def _():
        y = _rms(x_ref[0], g_ref[...])
        h_ref[...] = (y * (1.0 + mod_ref[0, 1:2, :]) + mod_ref[0, 0:1, :]).astype(BF16)

    o_ref[0] = jnp.dot(h_ref[...], w_ref[...], preferred_element_type=F32)


def _inproj(x, mod, g, w, ctx):
    bsz, n, d = x.shape
    tm = min(512, n)
    tn = 1024
    return pl.pallas_call(
        _inproj_kernel,
        out_shape=jax.ShapeDtypeStruct((bsz, n, P_COLS), F32),
        grid=(bsz, n // tm, P_COLS // tn),
        in_specs=[pl.BlockSpec((1, tm, d), lambda b, i, j: (b, i, 0)),
                  pl.BlockSpec((1, 6, d), _mod_index(ctx)),
                  pl.BlockSpec((1, d), lambda b, i, j: (0, 0)),
                  pl.BlockSpec((d, tn), lambda b, i, j: (0, j))],
        out_specs=pl.BlockSpec((1, tm, tn), lambda b, i, j: (b, i, j)),
        scratch_shapes=[pltpu.VMEM((tm, d), BF16)],
        compiler_params=_cparams(("parallel", "parallel", "arbitrary")),
        name="inproj",
    )(x, mod, g.reshape(1, d), w)


def _swap_halves(x, lane, width):
    return jnp.where((lane // width) % 2 == 0,
                     pltpu.roll(x, 128 - width, 1), pltpu.roll(x, width, 1))


def _post_kernel(*refs, rope):
    if rope:
        (pq_ref, pkv_ref, pcq_ref, pmix_ref, cosa_ref, sina_ref, cosb_ref, sinb_ref,
         gaq_ref, gak_ref, gql_ref, gkvl_ref, gqn_ref, gkn_ref, gqr_ref, gkr_ref,
         wuq_ref, wukv_ref, wgk_ref, bgk_ref,
         aq_ref, ak_ref, av_ref, bq_ref, bk_ref, bv_ref, gd_ref) = refs
    else:
        (pq_ref, pkv_ref, pcq_ref, pmix_ref,
         gaq_ref, gak_ref, gql_ref, gkvl_ref, gqn_ref, gkn_ref, gqr_ref, gkr_ref,
         wuq_ref, wukv_ref, wgk_ref, bgk_ref,
         aq_ref, ak_ref, av_ref, bq_ref, bk_ref, bv_ref, gd_ref) = refs
    tm = pq_ref.shape[1]
    lane = lax.broadcasted_iota(jnp.int32, (tm, 128), 1)
    lo = lane < 64

    def rope_a(x):
        if not rope:
            return x
        return x * cosa_ref[...] + _swap_halves(x, lane, 32) * sina_ref[...]

    def rope_b(x):
        if not rope:
            return x
        return x * cosb_ref[...] + _swap_halves(x, lane, 16) * sinb_ref[...]

    a_scale = LOG2E * A_HEAD_DIM ** -0.5
    for h in range(A_HEADS):
        x = pq_ref[0, :, h * 128:(h + 1) * 128]
        aq_ref[0, :, h * 128:(h + 1) * 128] = (rope_a(_rms(x, gaq_ref[...])) * a_scale).astype(BF16)
    for h in range(A_KV_HEADS):
        x = pkv_ref[0, :, h * 128:(h + 1) * 128]
        ak_ref[0, :, h * 128:(h + 1) * 128] = rope_a(_rms(x, gak_ref[...])).astype(BF16)
    av_ref[0] = pkv_ref[0, :, 256:512].astype(BF16)

    b_scale = LOG2E * (B_NOPE + B_ROPE) ** -0.5
    cq = _rms(pcq_ref[0], gql_ref[...]).astype(BF16)
    qb = jnp.dot(cq, wuq_ref[...], preferred_element_type=F32)
    for h in range(B_HEADS):
        x = qb[:, h * 128:(h + 1) * 128]
        bq_ref[0, :, h * B_QK_PAD:h * B_QK_PAD + 128] = (_rms(x, gqn_ref[...]) * b_scale).astype(BF16)
    for p in range(B_HEADS // 2):
        r = qb[:, 512 + p * 128:512 + (p + 1) * 128]
        sq = r * r
        ms_lo = jnp.sum(jnp.where(lo, sq, 0.0), axis=-1, keepdims=True)
        ms_hi = jnp.sum(jnp.where(lo, 0.0, sq), axis=-1, keepdims=True)
        ms = jnp.where(lo, ms_lo, ms_hi) * (1.0 / B_ROPE)
        rn = rope_b(r * lax.rsqrt(ms + NORM_EPS) * gqr_ref[...]) * b_scale
        h0 = 2 * p
        bq_ref[0, :, h0 * B_QK_PAD + 128:(h0 + 1) * B_QK_PAD] = jnp.where(lo, rn, 0.0).astype(BF16)
        bq_ref[0, :, (h0 + 1) * B_QK_PAD + 128:(h0 + 2) * B_QK_PAD] = jnp.where(lo, 0.0, rn).astype(BF16)

    ckv = _rms(pmix_ref[0, :, 0:256], gkvl_ref[...]).astype(BF16)
    kv = jnp.dot(ckv, wukv_ref[...], preferred_element_type=F32)
    kr = pmix_ref[0, :, 256:384]
    krn = rope_b(_rms(kr, gkr_ref[...]))
    kr_even = jnp.where(lo, krn, 0.0).astype(BF16)
    kr_odd = jnp.where(lo, 0.0, krn).astype(BF16)
    for h in range(B_HEADS):
        x = kv[:, h * 128:(h + 1) * 128]
        bk_ref[0, :, h * B_QK_PAD:h * B_QK_PAD + 128] = _rms(x, gkn_ref[...]).astype(BF16)
        bk_ref[0, :, h * B_QK_PAD + 128:(h + 1) * B_QK_PAD] = kr_even if h % 2 == 0 else kr_odd
    bv_ref[0] = kv[:, 512:1024].astype(BF16)

    z = jnp.dot(pmix_ref[0, :, 384:512], wgk_ref[...], preferred_element_type=F32,
                precision=lax.Precision.HIGHEST) + bgk_ref[...]
    gd_ref[0] = (jnp.minimum(z, 0.0) - jnp.log(1.0 + jnp.exp(-jnp.abs(z)))) * (1.0 / C_GATE_NORM)


def _post(proj, tabs, lw, rope):
    bsz, n, _ = proj.shape
    tm = min(512, n)
    row = lambda b, i: (b, i, 0)
    const = lambda b, i: (0, 0)
    in_specs = [pl.BlockSpec((1, tm, 1024), lambda b, i: (b, i, COL_AQ // 1024)),
                pl.BlockSpec((1, tm, 512), lambda b, i: (b, i, COL_AK // 512)),
                pl.BlockSpec((1, tm, 512), lambda b, i: (b, i, COL_BCQ // 512)),
                pl.BlockSpec((1, tm, 512), lambda b, i: (b, i, COL_BCKV // 512))]
    args = [proj, proj, proj, proj]
    if rope:
        in_specs += [pl.BlockSpec((tm, 128), lambda b, i: (i, 0))] * 4
        args += list(tabs)
    small = [lw["a_q_norm"], lw["a_k_norm"], lw["b_q_lora_norm"], lw["b_kv_lora_norm"],
             lw["b_q_nope_norm"], lw["b_k_nope_norm"], lw["b_q_rope_norm2"], lw["b_k_rope_norm2"],
             lw["w_uq"], lw["w_ukv"], lw["w_gk"], lw["b_gk"]]
    in_specs += [pl.BlockSpec(a.shape, const) for a in small]
    args += small
    out_shape = [jax.ShapeDtypeStruct((bsz, n, 1024), BF16),
                 jax.ShapeDtypeStruct((bsz, n, 256), BF16),
                 jax.ShapeDtypeStruct((bsz, n, 256), BF16),
                 jax.ShapeDtypeStruct((bsz, n, B_HEADS * B_QK_PAD), BF16),
                 jax.ShapeDtypeStruct((bsz, n, B_HEADS * B_QK_PAD), BF16),
                 jax.ShapeDtypeStruct((bsz, n, B_HEADS * B_V), BF16),
                 jax.ShapeDtypeStruct((bsz, n, 512), F32)]
    out_specs = [pl.BlockSpec((1, tm, s.shape[2]), row) for s in out_shape]
    return pl.pallas_call(
        functools.partial(_post_kernel, rope=rope),
        out_shape=out_shape,
        grid=(bsz, n // tm),
        in_specs=in_specs,
        out_specs=out_specs,
        compiler_params=_cparams(("parallel", "parallel")),
        name="post_lat" if rope else "post_ctx",
    )(*args)


def _attn_kernel(*refs, groups, dq, dv, tk, rsub, has_lat):
    if has_lat:
        q_ref, kc_ref, vc_ref, kl_ref, vl_ref, o_ref, q_scr, s_scr, p_scr, m_scr, l_scr, acc_scr = refs
    else:
        q_ref, kc_ref, vc_ref, o_ref, q_scr, s_scr, p_scr, m_scr, l_scr, acc_scr = refs
    tq = q_ref.shape[1]
    m_rows = groups * tq
    nt = (((1,), (1,)), ((), ()))
    for g in range(groups):
        q_scr[g * tq:(g + 1) * tq, :] = q_ref[0, :, g * dq:(g + 1) * dq]
    m_scr[...] = jnp.full(m_scr.shape, -jnp.inf, F32)
    l_scr[...] = jnp.zeros(l_scr.shape, F32)
    acc_scr[...] = jnp.zeros(acc_scr.shape, F32)

    def scores(slot, k):
        s_scr[slot, :, 0:k.shape[0]] = lax.dot_general(q_scr[...], k, nt, preferred_element_type=F32)

    def softmax_pv(slot, v):
        w = v.shape[0]
        for r in range(m_rows // rsub):
            rs = slice(r * rsub, (r + 1) * rsub)
            cols = [s_scr[slot, rs, j * 128:(j + 1) * 128] for j in range(w // 128)]
            mx = functools.reduce(jnp.maximum, cols)
            m_old = m_scr[rs, :]
            m_new = jnp.maximum(m_old, jnp.max(mx, axis=-1, keepdims=True))
            ps = [jnp.exp2(c - m_new) for c in cols]
            a = jnp.exp2(m_old - m_new)
            l_scr[rs, :] = a * l_scr[rs, :] + functools.reduce(jnp.add, ps)
            m_scr[rs, :] = m_new
            acc_scr[rs, :] = a * acc_scr[rs, :]
            for j, p in enumerate(ps):
                p_scr[rs, j * 128:(j + 1) * 128] = p.astype(BF16)
        acc_scr[...] += jnp.dot(p_scr[:, 0:w], v, preferred_element_type=F32)

    scores(0, kc_ref[0])
    if has_lat:
        nlc = kl_ref.shape[1] // tk

        def lat(ref, c):
            return ref[0, pl.ds(pl.multiple_of(c * tk, tk), tk), :]

        scores(1, lat(kl_ref, 0))
        softmax_pv(0, vc_ref[0])

        def body(i, carry):
            scores(0, lat(kl_ref, 2 * i + 1))
            softmax_pv(1, lat(vl_ref, 2 * i))
            scores(1, lat(kl_ref, 2 * i + 2))
            softmax_pv(0, lat(vl_ref, 2 * i + 1))
            return carry

        lax.fori_loop(0, nlc // 2 - 1, body, 0)
        scores(0, lat(kl_ref, nlc - 1))
        softmax_pv(1, lat(vl_ref, nlc - 2))
        softmax_pv(0, lat(vl_ref, nlc - 1))
    else:
        softmax_pv(0, vc_ref[0])
    for g in range(groups):
        gs = slice(g * tq, (g + 1) * tq)
        l = jnp.sum(l_scr[gs, :], axis=-1, keepdims=True)
        o_ref[0, :, g * dv:(g + 1) * dv] = (acc_scr[gs, :] * (1.0 / l)).astype(o_ref.dtype)


def _attention(q, kc, vc, kl, vl, *, kv_heads, groups, dq, dv, tq, tk=512, rsub=64):
    assert dv == 128, "the accumulator rescale reuses the 128-lane replicated running max"
    bsz, n, _ = q.shape
    nc = kc.shape[1]
    has_lat = kl is not None
    tq = min(tq, n)
    m_rows = groups * tq
    wmax = max(nc, tk) if has_lat else nc
    in_specs = [pl.BlockSpec((1, tq, groups * dq), lambda b, h, i: (b, i, h)),
                pl.BlockSpec((1, nc, dq), lambda b, h, i: (b, 0, h)),
                pl.BlockSpec((1, nc, dv), lambda b, h, i: (b, 0, h))]
    args = [q, kc, vc]
    if has_lat:
        nl = kl.shape[1]
        in_specs += [pl.BlockSpec((1, nl, dq), lambda b, h, i: (b, 0, h)),
                     pl.BlockSpec((1, nl, dv), lambda b, h, i: (b, 0, h))]
        args += [kl, vl]
    return pl.pallas_call(
        functools.partial(_attn_kernel, groups=groups, dq=dq, dv=dv, tk=tk, rsub=rsub, has_lat=has_lat),
        out_shape=jax.ShapeDtypeStruct((bsz, n, kv_heads * groups * dv), BF16),
        grid=(bsz, kv_heads, n // tq),
        in_specs=in_specs,
        out_specs=pl.BlockSpec((1, tq, groups * dv), lambda b, h, i: (b, i, h)),
        scratch_shapes=[pltpu.VMEM((m_rows, dq), BF16), pltpu.VMEM((2, m_rows, wmax), F32),
                        pltpu.VMEM((m_rows, wmax), BF16), pltpu.VMEM((m_rows, 128), F32),
                        pltpu.VMEM((m_rows, 128), F32), pltpu.VMEM((m_rows, dv), F32)],
        compiler_params=_cparams(("parallel", "parallel", "arbitrary")),
        name="attention",
    )(*args)


def _gla_kernel(q_ref, k_ref, v_ref, g_ref, s0_ref, o_ref, sfin_ref, st_ref, *, chunk):
    d = pl.program_id(0)
    i = pl.program_id(2)
    t_rows = q_ref.shape[1]
    nch = t_rows // chunk

    @pl.when(i == 0)
    def _():
        st_ref[...] = s0_ref[0, 0]

    row = lax.broadcasted_iota(jnp.int32, (chunk, chunk), 0)
    col = lax.broadcasted_iota(jnp.int32, (chunk, chunk), 1)
    tri = jnp.where(d == 0, col - row, row - col) <= 0
    tri_bf = jnp.where(tri, 1.0, 0.0).astype(BF16)
    row2 = lax.broadcasted_iota(jnp.int32, (chunk, 2 * chunk), 0)
    col2 = lax.broadcasted_iota(jnp.int32, (chunk, 2 * chunk), 1)
    col2 = jnp.where(col2 >= chunk, col2 - chunk, col2)
    tri2 = jnp.where(d == 0, col2 - row2, row2 - col2) <= 0
    lo =lax.broadcasted_iota(jnp.int32, (chunk, 128), 1) < C_DK
    vlo = lax.broadcasted_iota(jnp.int32, (chunk, 2 * C_DV), 1) < C_DV
    srow = lax.broadcasted_iota(jnp.int32, (2 * C_DV, 2 * C_DK), 0) // C_DV
    scol = lax.broadcasted_iota(jnp.int32, (2 * C_DV, 2 * C_DK), 1) // C_DK
    same_head = srow == scol
    nt = (((1,), (1,)), ((), ()))
    tn = (((0,), (0,)), ((), ()))

    for c in range(nch):
        cc = c + d * (nch - 1 - 2 * c)
        r0 = pl.multiple_of(cc * chunk, chunk)
        g = g_ref[0, pl.ds(r0, chunk), :]
        g_hi = g.astype(BF16)
        g_lo = (g - g_hi.astype(F32)).astype(BF16)
        cb = (jnp.dot(tri_bf, g_hi, preferred_element_type=F32)
              + jnp.dot(tri_bf, g_lo, preferred_element_type=F32))
        tot = jnp.sum(g, axis=0, keepdims=True)
        q = q_ref[0, pl.ds(r0, chunk), :] * (C_DK ** -0.5)
        k = k_ref[0, pl.ds(r0, chunk), :]
        v = v_ref[0, pl.ds(r0, chunk), :].astype(BF16)
        qe = (q * jnp.exp(cb)).astype(BF16)
        kd = (k * jnp.exp(tot - cb)).astype(BF16)
        ke = (k * jnp.exp(-cb)).astype(BF16)
        dec = jnp.exp(tot)
        for p in range(C_HEADS // 2):
            ls = slice(p * 128, (p + 1) * 128)
            vs = slice(p * 2 * C_DV, (p + 1) * 2 * C_DV)
            qe_p, kd_p, ke_p, v_p = qe[:, ls], kd[:, ls], ke[:, ls], v[:, vs]
            zk = jnp.zeros_like(ke_p)
            ke_bd = jnp.concatenate([jnp.where(lo, ke_p, zk), jnp.where(lo, zk, ke_p)], axis=0)
            att = lax.dot_general(qe_p, ke_bd, nt, preferred_element_type=F32)
            att = jnp.where(tri2, att, 0.0).astype(BF16)
            zv = jnp.zeros_like(v_p)
            v_bd = jnp.concatenate([jnp.where(vlo, v_p, zv), jnp.where(vlo, zv, v_p)], axis=0)
            st = st_ref[p]
            o = (jnp.dot(att, v_bd, preferred_element_type=F32)
                 + lax.dot_general(qe_p, st.astype(BF16), nt, preferred_element_type=F32))
            o_ref[0, 0, pl.ds(r0, chunk), vs] = o
            u = lax.dot_general(v_p, kd_p, tn, preferred_element_type=F32)
            st_ref[p] = dec[:, ls] * st + jnp.where(same_head, u, 0.0)

    @pl.when(i == pl.num_programs(2) - 1)
    def _():
        sfin_ref[0, 0] = st_ref[...]


def _gla(proj, gdec, s0):
    bsz, n, _ = proj.shape
    t_rows = min(512, n)
    nb = n // t_rows
    rb = lambda d, i: i + d * (nb - 1 - 2 * i)
    o, sfin = pl.pallas_call(
        functools.partial(_gla_kernel, chunk=GLA_CHUNK),
        out_shape=[jax.ShapeDtypeStruct((2, bsz, n, C_HEADS * C_DV), F32),
                   jax.ShapeDtypeStruct(s0.shape, F32)],
        grid=(2, bsz, nb),
        in_specs=[pl.BlockSpec((1, t_rows, 256), lambda d, b, i: (b, rb(d, i), COL_CQ // 256)),
                  pl.BlockSpec((1, t_rows, 256), lambda d, b, i: (b, rb(d, i), COL_CK // 256)),
                  pl.BlockSpec((1, t_rows, 512), lambda d, b, i: (b, rb(d, i), COL_CV // 512)),
                  pl.BlockSpec((1, t_rows, 256), lambda d, b, i: (b, rb(d, i), d)),
                  pl.BlockSpec((1, 1, 2, 256, 128), lambda d, b, i: (d, b, 0, 0, 0))],
        out_specs=[pl.BlockSpec((1, 1, t_rows, 512), lambda d, b, i: (d, b, rb(d, i), 0)),
                   pl.BlockSpec((1, 1, 2, 256, 128), lambda d, b, i: (d, b, 0, 0, 0))],
        scratch_shapes=[pltpu.VMEM((2, 256, 128), F32)],
        compiler_params=_cparams(("parallel", "parallel", "arbitrary")),
        name="gla",
    )(proj, proj, proj, gdec, s0)
    return o, sfin


def _outproj_kernel(oa_ref, ob_ref, oc_ref, cg_ref, gn_ref, w_ref, x_ref, mod_ref, o_ref, z_ref):
    z_ref[:, 0:1024] = oa_ref[0]
    z_ref[:, 1024:1536] = ob_ref[0]
    for h in range(C_HEADS):
        hs = slice(h * C_DV, (h + 1) * C_DV)
        oc = oc_ref[0, 0, :, hs] + oc_ref[1, 0, :, hs]
        z_ref[:, 1536 + h * C_DV:1536 + (h + 1) * C_DV] = (
            _rms(oc, gn_ref[...]) * _silu(cg_ref[0, :, hs])).astype(BF16)
    y = jnp.dot(z_ref[...], w_ref[...], preferred_element_type=F32)
    o_ref[0] = x_ref[0] + mod_ref[0, 2:3, :] * y


def _outproj(oa, ob, oc, proj, gn, w, x, mod, ctx):
    bsz, n, d = x.shape
    tm = min(512, n)
    return pl.pallas_call(
        _outproj_kernel,
        out_shape=jax.ShapeDtypeStruct((bsz, n, d), F32),
        grid=(bsz, n // tm),
        in_specs=[pl.BlockSpec((1, tm, 1024), lambda b, i: (b, i, 0)),
                  pl.BlockSpec((1, tm, 512), lambda b, i: (b, i, 0)),
                  pl.BlockSpec((2, 1, tm, 512), lambda b, i: (0, b, i, 0)),
                  pl.BlockSpec((1, tm, 512), lambda b, i: (b, i, COL_CG // 512)),
                  pl.BlockSpec((1, C_DV), lambda b, i: (0, 0)),
                  pl.BlockSpec(w.shape, lambda b, i: (0, 0)),
                  pl.BlockSpec((1, tm, d), lambda b, i: (b, i, 0)),
                  pl.BlockSpec((1, 6, d), _mod_index(ctx))],
        out_specs=pl.BlockSpec((1, tm, d), lambda b, i: (b, i, 0)),
        scratch_shapes=[pltpu.VMEM((tm, w.shape[0]), BF16)],
        compiler_params=_cparams(("parallel", "parallel")),
        name="outproj",
    )(oa, ob, oc, proj, gn, w, x, mod)


def _ffn_kernel(x_ref, mod_ref, g_ref, wg_ref, wu_ref, wd_ref, o_ref, h_ref, acc_ref):
    j = pl.program_id(2)

    @pl.when(j == 0)
    def _():
        y = _rms(x_ref[0], g_ref[...])
        h_ref[...] = (y * (1.0 + mod_ref[0, 4:5, :]) + mod_ref[0, 3:4, :]).astype(BF16)
        acc_ref[...] = jnp.zeros_like(acc_ref)

    h = h_ref[...]
    a = jnp.dot(h, wg_ref[...], preferred_element_type=F32)
    u = jnp.dot(h, wu_ref[...], preferred_element_type=F32)
    t = (_silu(a) * u).astype(BF16)
    acc_ref[...] += jnp.dot(t, wd_ref[...], preferred_element_type=F32)

    @pl.when(j == pl.num_programs(2) - 1)
    def _():
        o_ref[0] = x_ref[0] + mod_ref[0, 5:6, :] * acc_ref[...]


def _ffn(x, mod, g, wg, wu, wd, ctx):
    bsz, n, d = x.shape
    f = wg.shape[1]
    tm = min(512, n)
    tf = 512
    return pl.pallas_call(
        _ffn_kernel,
        out_shape=jax.ShapeDtypeStruct((bsz, n, d), F32),
        grid=(bsz, n // tm, f // tf),
        in_specs=[pl.BlockSpec((1, tm, d), lambda b, i, j: (b, i, 0)),
                  pl.BlockSpec((1, 6, d), _mod_index(ctx)),
                  pl.BlockSpec((1, d), lambda b, i, j: (0, 0)),
                  pl.BlockSpec((d, tf), lambda b, i, j: (0, j)),
                  pl.BlockSpec((d, tf), lambda b, i, j: (0, j)),
                  pl.BlockSpec((tf, d), lambda b, i, j: (j, 0))],
        out_specs=pl.BlockSpec((1, tm, d), lambda b, i, j: (b, i, 0)),
        scratch_shapes=[pltpu.VMEM((tm, d), BF16), pltpu.VMEM((tm, d), F32)],
        compiler_params=_cparams(("parallel", "parallel", "arbitrary")),
        name="ffn",
    )(x, mod, g.reshape(1, d), wg, wu, wd)


def _rope_tables(n):
    pos = np.arange(n)
    row = (pos // GRID_W).astype(np.float64)[:, None]
    col = (pos % GRID_W).astype(np.float64)[:, None]

    def table(half):
        inv = ROPE_THETA ** (-np.arange(half, dtype=np.float64) / half)
        ar, ac = row * inv[None, :], col * inv[None, :]
        cos = np.concatenate([np.cos(ar), np.cos(ar), np.cos(ac), np.cos(ac)], axis=1)
        sin = np.concatenate([-np.sin(ar), np.sin(ar), -np.sin(ac), np.sin(ac)], axis=1)
        return cos, sin

    cos_a, sin_a = table(A_HEAD_DIM // 4)
    cos_b, sin_b = table(B_ROPE // 4)
    cos_b, sin_b = np.tile(cos_b, (1, 2)), np.tile(sin_b, (1, 2))
    return tuple(jnp.asarray(t, F32) for t in (cos_a, sin_a, cos_b, sin_b))


def _layer_weights(l, w_in, w_uq, w_ukv, w_gk_f, b_gk_f, w_gk_b, b_gk_b, w_out, w_gate, w_up, w_down, smalls):
    w = w_in[l]
    d = w.shape[0]
    w_r = jnp.concatenate([
        w[:, 0:2304],
        w[:, 2304:2368], w[:, 2304:2368],
        w[:, 3904:3936], jnp.zeros((d, 96), w.dtype),
        w[:, 2368:3904],
    ], axis=1).astype(BF16)
    uq = w_uq[l].reshape(B_Q_RANK, B_HEADS, B_NOPE + B_ROPE)
    uq = jnp.concatenate([uq[:, :, :B_NOPE].reshape(B_Q_RANK, -1), uq[:, :, B_NOPE:].reshape(B_Q_RANK, -1)], axis=1)
    ukv = w_ukv[l].reshape(B_KV_RANK, B_HEADS, B_NOPE + B_V)
    ukv = jnp.concatenate([ukv[:, :, :B_NOPE].reshape(B_KV_RANK, -1), ukv[:, :, B_NOPE:].reshape(B_KV_RANK, -1)], axis=1)
    nk = C_HEADS * C_DK
    w_gk = jnp.zeros((128, 2 * nk), F32)
    w_gk = w_gk.at[0:C_GATE_RANK, 0:nk].set(w_gk_f[l]).at[C_GATE_RANK:2 * C_GATE_RANK, nk:].set(w_gk_b[l])
    lw = {k: v[l].reshape(1, -1) for k, v in smalls.items()}
    lw["b_q_rope_norm2"] = jnp.tile(lw["b_q_rope_norm"], (1, 2))
    lw["b_k_rope_norm2"] = jnp.tile(lw["b_k_rope_norm"], (1, 2))
    lw.update(w_in=w_r, w_uq=uq.astype(BF16), w_ukv=ukv.astype(BF16), w_gk=w_gk,
              b_gk=jnp.concatenate([b_gk_f[l], b_gk_b[l]]).reshape(1, -1),
              w_out=w_out[l].astype(BF16), w_gate=w_gate[l].astype(BF16),
              w_up=w_up[l].astype(BF16), w_down=w_down[l].astype(BF16))
    return lw


def kernel(x, c, ctx, c_ctx, w_mod, b_mod, norm1_g, norm2_g, w_in, a_q_norm, a_k_norm, b_q_lora_norm, b_kv_lora_norm, w_uq, w_ukv, b_q_nope_norm, b_k_nope_norm, b_q_rope_norm, b_k_rope_norm, w_gk_f, b_gk_f, w_gk_b, b_gk_b, c_out_norm, w_out, w_gate, w_up, w_down):
    bsz, n, d = x.shape
    depth = w_in.shape[0]
    tabs = _rope_tables(n)
    cpad = jnp.concatenate([c, c_ctx[None, :], jnp.zeros((8 - bsz - 1, d), F32)], axis=0)
    smalls = dict(a_q_norm=a_q_norm, a_k_norm=a_k_norm, b_q_lora_norm=b_q_lora_norm,
                  b_kv_lora_norm=b_kv_lora_norm, b_q_nope_norm=b_q_nope_norm, b_k_nope_norm=b_k_nope_norm,
                  b_q_rope_norm=b_q_rope_norm, b_k_rope_norm=b_k_rope_norm, c_out_norm=c_out_norm)
    s_zero = jnp.zeros((2, bsz, 2, 2 * C_DV, 2 * C_DK), F32)
    xc = ctx
    for l in range(depth):
        ctx_out = l < depth - 1
        lw = _layer_weights(l, w_in, w_uq, w_ukv, w_gk_f, b_gk_f, w_gk_b, b_gk_b,
                            w_out, w_gate, w_up, w_down, smalls)
        mod = _modulation(cpad, w_mod[l], b_mod[l])

        pc = _inproj(xc, mod, norm1_g[l], lw["w_in"], True)
        pl_ = _inproj(x, mod, norm1_g[l], lw["w_in"], False)
        caq, cak, cav, cbq, cbk, cbv, cgd = _post(pc, None, lw, False)
        laq, lak, lav, lbq, lbk, lbv, lgd = _post(pl_, tabs, lw, True)

        o_a = _attention(laq, cak, cav, lak, lav, kv_heads=A_KV_HEADS, groups=A_HEADS // A_KV_HEADS,
                         dq=A_HEAD_DIM, dv=A_HEAD_DIM, tq=256)
        o_b = _attention(lbq, cbk, cbv, lbk, lbv, kv_heads=B_HEADS, groups=1,
                         dq=B_QK_PAD, dv=B_V, tq=512)
        oc_c, s_ctx = _gla(pc, cgd, s_zero)
        o_c, _ = _gla(pl_, lgd, s_ctx)

        x = _outproj(o_a, o_b, o_c, pl_, lw["c_out_norm"], lw["w_out"], x, mod, False)
        x = _ffn(x, mod, norm2_g[l], lw["w_gate"], lw["w_up"], lw["w_down"], False)
        if ctx_out:
            oc_a = _attention(caq, cak, cav, None, None, kv_heads=A_KV_HEADS, groups=A_HEADS // A_KV_HEADS,
                              dq=A_HEAD_DIM, dv=A_HEAD_DIM, tq=256)
            oc_b = _attention(cbq, cbk, cbv, None, None, kv_heads=B_HEADS, groups=1,
                              dq=B_QK_PAD, dv=B_V, tq=256)
            xc = _outproj(oc_a, oc_b, oc_c, pc, lw["c_out_norm"], lw["w_out"], xc, mod, True)
            xc = _ffn(xc, mod, norm2_g[l], lw["w_gate"], lw["w_up"], lw["w_down"], True)
    return x
```

```python
import functools

import numpy as np
import jax
import jax.numpy as jnp
from jax import lax
from jax.experimental import pallas as pl
from jax.experimental.pallas import tpu as pltpu

F32 = jnp.float32
BF16 = jnp.bfloat16

D_MODEL = 2048
GRID_W = 64
ROPE_THETA = 10000.0
NORM_EPS = 1e-6

A_HEADS = 8
A_KV_HEADS = 2
A_HEAD_DIM = 128
B_HEADS = 4
B_Q_RANK = 512
B_KV_RANK = 256
B_NOPE = 128
B_ROPE = 64
B_V = 128
B_QK_PAD = 256
C_HEADS = 4
C_DK = 64
C_DV = 128
C_GATE_RANK = 16
C_GATE_NORM = 16.0

D_FF = 5632
P_COLS = 4096

COL_AQ = 0
COL_AK = 1024
COL_AV = 1280
COL_BCQ = 1536
COL_BCKV = 2048
COL_BKR = 2304
COL_GATE = 2432
COL_CQ = 2560
COL_CK = 2816
COL_CV = 3072
COL_CG = 3584

GLA_CHUNK = 128

VMEM_LIMIT = 56 * 2**20
LOG2E = 1.4426950408889634


def _cparams(sem, vmem=VMEM_LIMIT):
    return pltpu.CompilerParams(dimension_semantics=sem, vmem_limit_bytes=vmem)


def _silu(x):
    return x / (1.0 + jnp.exp(-x))


def _rms(x, g):
    ms = jnp.mean(x * x, axis=-1, keepdims=True)
    return x * lax.rsqrt(ms + NORM_EPS) * g


def _mod_kernel(c_ref, w_ref, b_ref, o_ref):
    s = _silu(c_ref[...]).astype(BF16)
    o_ref[...] = jnp.dot(s, w_ref[...].astype(BF16), preferred_element_type=F32) + b_ref[...]


def _modulation(cpad, w_mod, b_mod):
    d = cpad.shape[1]
    n = w_mod.shape[1]
    tn = 1024
    out = pl.pallas_call(
        _mod_kernel,
        out_shape=jax.ShapeDtypeStruct((8, n), F32),
        grid=(n // tn,),
        in_specs=[pl.BlockSpec((8, d), lambda j: (0, 0)),
                  pl.BlockSpec((d, tn), lambda j: (0, j)),
                  pl.BlockSpec((1, tn), lambda j: (0, j))],
        out_specs=pl.BlockSpec((8, tn), lambda j: (0, j)),
        compiler_params=_cparams(("arbitrary",)),
        name="modulation",
    )(cpad, w_mod, b_mod.reshape(1, n))
    return out.reshape(8, 6, d)


def _mod_index(ctx):
    if ctx:
        return lambda b, i, *_: (4, 0, 0)
    return lambda b, i, *_: (b, 0, 0)


def _inproj_kernel(x_ref, mod_ref, g_ref, w_ref, o_ref, h_ref):
    @pl.when(pl.program_id(2) == 0)
    def _():
        y = _rms(x_ref[0], g_ref[...])
        h_ref[...] = (y * (1.0 + mod_ref[0, 1:2, :]) + mod_ref[0, 0:1, :]).astype(BF16)

    o_ref[0] = jnp.dot(h_ref[...], w_ref[...], preferred_element_type=F32)


def _inproj(x, mod, g, w, ctx):
    bsz, n, d = x.shape
    tm = min(512, n)
    tn = 1024
    return pl.pallas_call(
        _inproj_kernel,
        out_shape=jax.ShapeDtypeStruct((bsz, n, P_COLS), F32),
        grid=(bsz, n // tm, P_COLS // tn),
        in_specs=[pl.BlockSpec((1, tm, d), lambda b, i, j: (b, i, 0)),
                  pl.BlockSpec((1, 6, d), _mod_index(ctx)),
                  pl.BlockSpec((1, d), lambda b, i, j: (0, 0)),
                  pl.BlockSpec((d, tn), lambda b, i, j: (0, j))],
        out_specs=pl.BlockSpec((1, tm, tn), lambda b, i, j: (b, i, j)),
        scratch_shapes=[pltpu.VMEM((tm, d), BF16)],
        compiler_params=_cparams(("parallel", "parallel", "arbitrary")),
        name="inproj",
    )(x, mod, g.reshape(1, d), w)


def _swap_halves(x, lane, width):
    return jnp.where((lane // width) % 2 == 0,
                     pltpu.roll(x, 128 - width, 1), pltpu.roll(x, width, 1))


def _post_kernel(*refs, rope):
    if rope:
        (pq_ref, pkv_ref, pcq_ref, pmix_ref, cosa_ref, sina_ref, cosb_ref, sinb_ref,
         gaq_ref, gak_ref, gql_ref, gkvl_ref, gqn_ref, gkn_ref, gqr_ref, gkr_ref,
         wuq_ref, wukv_ref, wgk_ref, bgk_ref,
         aq_ref, ak_ref, av_ref, bq_ref, bk_ref, bv_ref, gd_ref) = refs
    else:
        (pq_ref, pkv_ref, pcq_ref, pmix_ref,
         gaq_ref, gak_ref, gql_ref, gkvl_ref, gqn_ref, gkn_ref, gqr_ref, gkr_ref,
         wuq_ref, wukv_ref, wgk_ref, bgk_ref,
         aq_ref, ak_ref, av_ref, bq_ref, bk_ref, bv_ref, gd_ref) = refs
    tm = pq_ref.shape[1]
    lane = lax.broadcasted_iota(jnp.int32, (tm, 128), 1)
    lo = lane < 64

    def rope_a(x):
        if not rope:
            return x
        return x * cosa_ref[...] + _swap_halves(x, lane, 32) * sina_ref[...]

    def rope_b(x):
        if not rope:
            return x
        return x * cosb_ref[...] + _swap_halves(x, lane, 16) * sinb_ref[...]

    a_scale = LOG2E * A_HEAD_DIM ** -0.5
    for h in range(A_HEADS):
        x = pq_ref[0, :, h * 128:(h + 1) * 128]
        aq_ref[0, :, h * 128:(h + 1) * 128] = (rope_a(_rms(x, gaq_ref[...])) * a_scale).astype(BF16)
    for h in range(A_KV_HEADS):
        x = pkv_ref[0, :, h * 128:(h + 1) * 128]
        ak_ref[0, :, h * 128:(h + 1) * 128] = rope_a(_rms(x, gak_ref[...])).astype(BF16)
    av_ref[0] = pkv_ref[0, :, 256:512].astype(BF16)

    b_scale = LOG2E * (B_NOPE + B_ROPE) ** -0.5
    cq = _rms(pcq_ref[0], gql_ref[...]).astype(BF16)
    qb = jnp.dot(cq, wuq_ref[...], preferred_element_type=F32)
    for h in range(B_HEADS):
        x = qb[:, h * 128:(h + 1) * 128]
        bq_ref[0, :, h * B_QK_PAD:h * B_QK_PAD + 128] = (_rms(x, gqn_ref[...]) * b_scale).astype(BF16)
    for p in range(B_HEADS // 2):
        r = qb[:, 512 + p * 128:512 + (p + 1) * 128]
        sq = r * r
        ms_lo = jnp.sum(jnp.where(lo, sq, 0.0), axis=-1, keepdims=True)
        ms_hi = jnp.sum(jnp.where(lo, 0.0, sq), axis=-1, keepdims=True)
        ms = jnp.where(lo, ms_lo, ms_hi) * (1.0 / B_ROPE)
        rn = rope_b(r * lax.rsqrt(ms + NORM_EPS) * gqr_ref[...]) * b_scale
        h0 = 2 * p
        bq_ref[0, :, h0 * B_QK_PAD + 128:(h0 + 1) * B_QK_PAD] = jnp.where(lo, rn, 0.0).astype(BF16)
        bq_ref[0, :, (h0 + 1) * B_QK_PAD + 128:(h0 + 2) * B_QK_PAD] = jnp.where(lo, 0.0, rn).astype(BF16)

    ckv = _rms(pmix_ref[0, :, 0:256], gkvl_ref[...]).astype(BF16)
    kv = jnp.dot(ckv, wukv_ref[...], preferred_element_type=F32)
    kr = pmix_ref[0, :, 256:384]
    krn = rope_b(_rms(kr, gkr_ref[...]))
    kr_even = jnp.where(lo, krn, 0.0).astype(BF16)
    kr_odd = jnp.where(lo, 0.0, krn).astype(BF16)
    for h in range(B_HEADS):
        x = kv[:, h * 128:(h + 1) * 128]
        bk_ref[0, :, h * B_QK_PAD:h * B_QK_PAD + 128] = _rms(x, gkn_ref[...]).astype(BF16)
        bk_ref[0, :, h * B_QK_PAD + 128:(h + 1) * B_QK_PAD] = kr_even if h % 2 == 0 else kr_odd
    bv_ref[0] = kv[:, 512:1024].astype(BF16)

    z = jnp.dot(pmix_ref[0, :, 384:512], wgk_ref[...], preferred_element_type=F32,
                precision=lax.Precision.HIGHEST) + bgk_ref[...]
    gd_ref[0] = (jnp.minimum(z, 0.0) - jnp.log(1.0 + jnp.exp(-jnp.abs(z)))) * (1.0 / C_GATE_NORM)


def _post(proj, tabs, lw, rope):
    bsz, n, _ = proj.shape
    tm = min(512, n)
    row = lambda b, i: (b, i, 0)
    const = lambda b, i: (0, 0)
    in_specs = [pl.BlockSpec((1, tm, 1024), lambda b, i: (b, i, COL_AQ // 1024)),
                pl.BlockSpec((1, tm, 512), lambda b, i: (b, i, COL_AK // 512)),
                pl.BlockSpec((1, tm, 512), lambda b, i: (b, i, COL_BCQ // 512)),
                pl.BlockSpec((1, tm, 512), lambda b, i: (b, i, COL_BCKV // 512))]
    args = [proj, proj, proj, proj]
    if rope:
        in_specs += [pl.BlockSpec((tm, 128), lambda b, i: (i, 0))] * 4
        args += list(tabs)
    small = [lw["a_q_norm"], lw["a_k_norm"], lw["b_q_lora_norm"], lw["b_kv_lora_norm"],
             lw["b_q_nope_norm"], lw["b_k_nope_norm"], lw["b_q_rope_norm2"], lw["b_k_rope_norm2"],
             lw["w_uq"], lw["w_ukv"], lw["w_gk"], lw["b_gk"]]
    in_specs += [pl.BlockSpec(a.shape, const) for a in small]
    args += small
    out_shape = [jax.ShapeDtypeStruct((bsz, n, 1024), BF16),
                 jax.ShapeDtypeStruct((bsz, n, 256), BF16),
                 jax.ShapeDtypeStruct((bsz, n, 256), BF16),
                 jax.ShapeDtypeStruct((bsz, n, B_HEADS * B_QK_PAD), BF16),
                 jax.ShapeDtypeStruct((bsz, n, B_HEADS * B_QK_PAD), BF16),
                 jax.ShapeDtypeStruct((bsz, n, B_HEADS * B_V), BF16),
                 jax.ShapeDtypeStruct((bsz, n, 512), F32)]
    out_specs = [pl.BlockSpec((1, tm, s.shape[2]), row) for s in out_shape]
    return pl.pallas_call(
        functools.partial(_post_kernel, rope=rope),
        out_shape=out_shape,
        grid=(bsz, n // tm),
        in_specs=in_specs,
        out_specs=out_specs,
        compiler_params=_cparams(("parallel", "parallel")),
        name="post_lat" if rope else "post_ctx",
    )(*args)


def _attn_kernel(*refs, groups, dq, dv, tk, rsub, has_lat):
    if has_lat:
        q_ref, kc_ref, vc_ref, kl_ref, vl_ref, o_ref, q_scr, s_scr, p_scr, m_scr, l_scr, acc_scr = refs
    else:
        q_ref, kc_ref, vc_ref, o_ref, q_scr, s_scr, p_scr, m_scr, l_scr, acc_scr = refs
    tq = q_ref.shape[1]
    m_rows = groups * tq
    nt = (((1,), (1,)), ((), ()))
    for g in range(groups):
        q_scr[g * tq:(g + 1) * tq, :] = q_ref[0, :, g * dq:(g + 1) * dq]
    m_scr[...] = jnp.full(m_scr.shape, -jnp.inf, F32)
    l_scr[...] = jnp.zeros(l_scr.shape, F32)
    acc_scr[...] = jnp.zeros(acc_scr.shape, F32)

    def scores(slot, k):
        s_scr[slot, :, 0:k.shape[0]] = lax.dot_general(q_scr[...], k, nt, preferred_element_type=F32)

    def softmax_pv(slot, v):
        w = v.shape[0]
        for r in range(m_rows // rsub):
            rs = slice(r * rsub, (r + 1) * rsub)
            cols = [s_scr[slot, rs, j * 128:(j + 1) * 128] for j in range(w // 128)]
            mx = functools.reduce(jnp.maximum, cols)
            m_old = m_scr[rs, :]
            m_new = jnp.maximum(m_old, jnp.max(mx, axis=-1, keepdims=True))
            ps = [jnp.exp2(c - m_new) for c in cols]
            a = jnp.exp2(m_old - m_new)
            l_scr[rs, :] = a * l_scr[rs, :] + functools.reduce(jnp.add, ps)
            m_scr[rs, :] = m_new
            acc_scr[rs, :] = a * acc_scr[rs, :]
            for j, p in enumerate(ps):
                p_scr[rs, j * 128:(j + 1) * 128] = p.astype(BF16)
        acc_scr[...] += jnp.dot(p_scr[:, 0:w], v, preferred_element_type=F32)

    scores(0, kc_ref[0])
    if has_lat:
        nlc = kl_ref.shape[1] // tk

        def lat(ref, c):
            return ref[0, pl.ds(pl.multiple_of(c * tk, tk), tk), :]

        scores(1, lat(kl_ref, 0))
        softmax_pv(0, vc_ref[0])

        def body(i, carry):
            scores(0, lat(kl_ref, 2 * i + 1))
            softmax_pv(1, lat(vl_ref, 2 * i))
            scores(1, lat(kl_ref, 2 * i + 2))
            softmax_pv(0, lat(vl_ref, 2 * i + 1))
            return carry

        lax.fori_loop(0, nlc // 2 - 1, body, 0)
        scores(0, lat(kl_ref, nlc - 1))
        softmax_pv(1, lat(vl_ref, nlc - 2))
        softmax_pv(0, lat(vl_ref, nlc - 1))
    else:
        softmax_pv(0, vc_ref[0])
    for g in range(groups):
        gs = slice(g * tq, (g + 1) * tq)
        l = jnp.sum(l_scr[gs, :], axis=-1, keepdims=True)
        o_ref[0, :, g * dv:(g + 1) * dv] = (acc_scr[gs, :] * (1.0 / l)).astype(o_ref.dtype)


def _attention(q, kc, vc, kl, vl, *, kv_heads, groups, dq, dv, tq, tk=1024, rsub=64):
    assert dv == 128, "the accumulator rescale reuses the 128-lane replicated running max"
    bsz, n, _ = q.shape
    nc = kc.shape[1]
    has_lat = kl is not None
    tq = min(tq, n)
    m_rows = groups * tq
    wmax = max(nc, tk) if has_lat else nc
    in_specs = [pl.BlockSpec((1, tq, groups * dq), lambda b, h, i: (b, i, h)),
                pl.BlockSpec((1, nc, dq), lambda b, h, i: (b, 0, h)),
                pl.BlockSpec((1, nc, dv), lambda b, h, i: (b, 0, h))]
    args = [q, kc, vc]
    if has_lat:
        nl = kl.shape[1]
        in_specs += [pl.BlockSpec((1, nl, dq), lambda b, h, i: (b, 0, h)),
                     pl.BlockSpec((1, nl, dv), lambda b, h, i: (b, 0, h))]
        args += [kl, vl]
    return pl.pallas_call(
        functools.partial(_attn_kernel, groups=groups, dq=dq, dv=dv, tk=tk, rsub=rsub, has_lat=has_lat),
        out_shape=jax.ShapeDtypeStruct((bsz, n, kv_heads * groups * dv), BF16),
        grid=(bsz, kv_heads, n // tq),
        in_specs=in_specs,
        out_specs=pl.BlockSpec((1, tq, groups * dv), lambda b, h, i: (b, i, h)),
        scratch_shapes=[pltpu.VMEM((m_rows, dq), BF16), pltpu.VMEM((2, m_rows, wmax), F32),
                        pltpu.VMEM((m_rows, wmax), BF16), pltpu.VMEM((m_rows, 128), F32),
                        pltpu.VMEM((m_rows, 128), F32), pltpu.VMEM((m_rows, dv), F32)],
        compiler_params=_cparams(("parallel", "parallel", "arbitrary")),
        name="attention",
    )(*args)


def _gla_kernel(q_ref, k_ref, v_ref, g_ref, s0_ref, o_ref, sfin_ref, st_ref, *, chunk):
    d = pl.program_id(0)
    i = pl.program_id(2)
    t_rows = q_ref.shape[1]
    nch = t_rows // chunk

    @pl.when(i == 0)
    def _():
        st_ref[...] = s0_ref[0, 0]

    row = lax.broadcasted_iota(jnp.int32, (chunk, chunk), 0)
    col = lax.broadcasted_iota(jnp.int32, (chunk, chunk), 1)
    tri = jnp.where(d == 0, col - row, row - col) <= 0
    tri_bf = jnp.where(tri, 1.0, 0.0).astype(BF16)
    row2 = lax.broadcasted_iota(jnp.int32, (chunk, 2 * chunk), 0)
    col2 = lax.broadcasted_iota(jnp.int32, (chunk, 2 * chunk), 1)
    col2 = jnp.where(col2 >= chunk, col2 - chunk, col2)
    tri2 = jnp.where(d == 0, col2 - row2, row2 - col2) <= 0
    lo =lax.broadcasted_iota(jnp.int32, (chunk, 128), 1) < C_DK
    vlo = lax.broadcasted_iota(jnp.int32, (chunk, 2 * C_DV), 1) < C_DV
    srow = lax.broadcasted_iota(jnp.int32, (2 * C_DV, 2 * C_DK), 0) // C_DV
    scol = lax.broadcasted_iota(jnp.int32, (2 * C_DV, 2 * C_DK), 1) // C_DK
    same_head = srow == scol
    nt = (((1,), (1,)), ((), ()))
    tn = (((0,), (0,)), ((), ()))

    for c in range(nch):
        cc = c + d * (nch - 1 - 2 * c)
        r0 = pl.multiple_of(cc * chunk, chunk)
        g = g_ref[0, pl.ds(r0, chunk), :]
        g_hi = g.astype(BF16)
        g_lo = (g - g_hi.astype(F32)).astype(BF16)
        cb = (jnp.dot(tri_bf, g_hi, preferred_element_type=F32)
              + jnp.dot(tri_bf, g_lo, preferred_element_type=F32))
        tot = jnp.sum(g, axis=0, keepdims=True)
        q = q_ref[0, pl.ds(r0, chunk), :] * (C_DK ** -0.5)
        k = k_ref[0, pl.ds(r0, chunk), :]
        v = v_ref[0, pl.ds(r0, chunk), :].astype(BF16)
        qe = (q * jnp.exp(cb)).astype(BF16)
        kd = (k * jnp.exp(tot - cb)).astype(BF16)
        ke = (k * jnp.exp(-cb)).astype(BF16)
        dec = jnp.exp(tot)
        for p in range(C_HEADS // 2):
            ls = slice(p * 128, (p + 1) * 128)
            vs = slice(p * 2 * C_DV, (p + 1) * 2 * C_DV)
            qe_p, kd_p, ke_p, v_p = qe[:, ls], kd[:, ls], ke[:, ls], v[:, vs]
            zk = jnp.zeros_like(ke_p)
            ke_bd = jnp.concatenate([jnp.where(lo, ke_p, zk), jnp.where(lo, zk, ke_p)], axis=0)
            att = lax.dot_general(qe_p, ke_bd, nt, preferred_element_type=F32)
            att = jnp.where(tri2, att, 0.0).astype(BF16)
            zv = jnp.zeros_like(v_p)
            v_bd = jnp.concatenate([jnp.where(vlo, v_p, zv), jnp.where(vlo, zv, v_p)], axis=0)
            st = st_ref[p]
            o = (jnp.dot(att, v_bd, preferred_element_type=F32)
                 + lax.dot_general(qe_p, st.astype(BF16), nt, preferred_element_type=F32))
            o_ref[0, 0, pl.ds(r0, chunk), vs] = o
            u = lax.dot_general(v_p, kd_p, tn, preferred_element_type=F32)
            st_ref[p] = dec[:, ls] * st + jnp.where(same_head, u, 0.0)

    @pl.when(i == pl.num_programs(2) - 1)
    def _():
        sfin_ref[0, 0] = st_ref[...]


def _gla(proj, gdec, s0):
    bsz, n, _ = proj.shape
    t_rows = min(512, n)
    nb = n // t_rows
    rb = lambda d, i: i + d * (nb - 1 - 2 * i)
    o, sfin = pl.pallas_call(
        functools.partial(_gla_kernel, chunk=GLA_CHUNK),
        out_shape=[jax.ShapeDtypeStruct((2, bsz, n, C_HEADS * C_DV), F32),
                   jax.ShapeDtypeStruct(s0.shape, F32)],
        grid=(2, bsz, nb),
        in_specs=[pl.BlockSpec((1, t_rows, 256), lambda d, b, i: (b, rb(d, i), COL_CQ // 256)),
                  pl.BlockSpec((1, t_rows, 256), lambda d, b, i: (b, rb(d, i), COL_CK // 256)),
                  pl.BlockSpec((1, t_rows, 512), lambda d, b, i: (b, rb(d, i), COL_CV // 512)),
                  pl.BlockSpec((1, t_rows, 256), lambda d, b, i: (b, rb(d, i), d)),
                  pl.BlockSpec((1, 1, 2, 256, 128), lambda d, b, i: (d, b, 0, 0, 0))],
        out_specs=[pl.BlockSpec((1, 1, t_rows, 512), lambda d, b, i: (d, b, rb(d, i), 0)),
                   pl.BlockSpec((1, 1, 2, 256, 128), lambda d, b, i: (d, b, 0, 0, 0))],
        scratch_shapes=[pltpu.VMEM((2, 256, 128), F32)],
        compiler_params=_cparams(("parallel", "parallel", "arbitrary")),
        name="gla",
    )(proj, proj, proj, gdec, s0)
    return o, sfin


def _outproj_kernel(oa_ref, ob_ref, oc_ref, cg_ref, gn_ref, w_ref, x_ref, mod_ref, o_ref, z_ref):
    z_ref[:, 0:1024] = oa_ref[0]
    z_ref[:, 1024:1536] = ob_ref[0]
    for h in range(C_HEADS):
        hs = slice(h * C_DV, (h + 1) * C_DV)
        oc = oc_ref[0, 0, :, hs] + oc_ref[1, 0, :, hs]
        z_ref[:, 1536 + h * C_DV:1536 + (h + 1) * C_DV] = (
            _rms(oc, gn_ref[...]) * _silu(cg_ref[0, :, hs])).astype(BF16)
    y = jnp.dot(z_ref[...], w_ref[...], preferred_element_type=F32)
    o_ref[0] = x_ref[0] + mod_ref[0, 2:3, :] * y


def _outproj(oa, ob, oc, proj, gn, w, x, mod, ctx):
    bsz, n, d = x.shape
    tm = min(512, n)
    return pl.pallas_call(
        _outproj_kernel,
        out_shape=jax.ShapeDtypeStruct((bsz, n, d), F32),
        grid=(bsz, n // tm),
        in_specs=[pl.BlockSpec((1, tm, 1024), lambda b, i: (b, i, 0)),
                  pl.BlockSpec((1, tm, 512), lambda b, i: (b, i, 0)),
                  pl.BlockSpec((2, 1, tm, 512), lambda b, i: (0, b, i, 0)),
                  pl.BlockSpec((1, tm, 512), lambda b, i: (b, i, COL_CG // 512)),
                  pl.BlockSpec((1, C_DV), lambda b, i: (0, 0)),
                  pl.BlockSpec(w.shape, lambda b, i: (0, 0)),
                  pl.BlockSpec((1, tm, d), lambda b, i: (b, i, 0)),
                  pl.BlockSpec((1, 6, d), _mod_index(ctx))],
        out_specs=pl.BlockSpec((1, tm, d), lambda b, i: (b, i, 0)),
        scratch_shapes=[pltpu.VMEM((tm, w.shape[0]), BF16)],
        compiler_params=_cparams(("parallel", "parallel")),
        name="outproj",
    )(oa, ob, oc, proj, gn, w, x, mod)


def _ffn_kernel(x_ref, mod_ref, g_ref, wg_ref, wu_ref, wd_ref, o_ref, h_ref, acc_ref):
    j = pl.program_id(2)

    @pl.when(j == 0)
    def _():
        y = _rms(x_ref[0], g_ref[...])
        h_ref[...] = (y * (1.0 + mod_ref[0, 4:5, :]) + mod_ref[0, 3:4, :]).astype(BF16)
        acc_ref[...] = jnp.zeros_like(acc_ref)

    h = h_ref[...]
    a = jnp.dot(h, wg_ref[...], preferred_element_type=F32)
    u = jnp.dot(h, wu_ref[...], preferred_element_type=F32)
    t = (_silu(a) * u).astype(BF16)
    acc_ref[...] += jnp.dot(t, wd_ref[...], preferred_element_type=F32)

    @pl.when(j == pl.num_programs(2) - 1)
    def _():
        o_ref[0] = x_ref[0] + mod_ref[0, 5:6, :] * acc_ref[...]


def _ffn(x, mod, g, wg, wu, wd, ctx):
    bsz, n, d = x.shape
    f = wg.shape[1]
    tm = min(512, n)
    tf = 512
    return pl.pallas_call(
        _ffn_kernel,
        out_shape=jax.ShapeDtypeStruct((bsz, n, d), F32),
        grid=(bsz, n // tm, f // tf),
        in_specs=[pl.BlockSpec((1, tm, d), lambda b, i, j: (b, i, 0)),
                  pl.BlockSpec((1, 6, d), _mod_index(ctx)),
                  pl.BlockSpec((1, d), lambda b, i, j: (0, 0)),
                  pl.BlockSpec((d, tf), lambda b, i, j: (0, j)),
                  pl.BlockSpec((d, tf), lambda b, i, j: (0, j)),
                  pl.BlockSpec((tf, d), lambda b, i, j: (j, 0))],
        out_specs=pl.BlockSpec((1, tm, d), lambda b, i, j: (b, i, 0)),
        scratch_shapes=[pltpu.VMEM((tm, d), BF16), pltpu.VMEM((tm, d), F32)],
        compiler_params=_cparams(("parallel", "parallel", "arbitrary")),
        name="ffn",
    )(x, mod, g.reshape(1, d), wg, wu, wd)


def _rope_tables(n):
    pos = np.arange(n)
    row = (pos // GRID_W).astype(np.float64)[:, None]
    col = (pos % GRID_W).astype(np.float64)[:, None]

    def table(half):
        inv = ROPE_THETA ** (-np.arange(half, dtype=np.float64) / half)
        ar, ac = row * inv[None, :], col * inv[None, :]
        cos = np.concatenate([np.cos(ar), np.cos(ar), np.cos(ac), np.cos(ac)], axis=1)
        sin = np.concatenate([-np.sin(ar), np.sin(ar), -np.sin(ac), np.sin(ac)], axis=1)
        return cos, sin

    cos_a, sin_a = table(A_HEAD_DIM // 4)
    cos_b, sin_b = table(B_ROPE // 4)
    cos_b, sin_b = np.tile(cos_b, (1, 2)), np.tile(sin_b, (1, 2))
    return tuple(jnp.asarray(t, F32) for t in (cos_a, sin_a, cos_b, sin_b))


def _layer_weights(l, w_in, w_uq, w_ukv, w_gk_f, b_gk_f, w_gk_b, b_gk_b, w_out, w_gate, w_up, w_down, smalls):
    w = w_in[l]
    d = w.shape[0]
    w_r = jnp.concatenate([
        w[:, 0:2304],
        w[:, 2304:2368], w[:, 2304:2368],
        w[:, 3904:3936], jnp.zeros((d, 96), w.dtype),
        w[:, 2368:3904],
    ], axis=1).astype(BF16)
    uq = w_uq[l].reshape(B_Q_RANK, B_HEADS, B_NOPE + B_ROPE)
    uq = jnp.concatenate([uq[:, :, :B_NOPE].reshape(B_Q_RANK, -1), uq[:, :, B_NOPE:].reshape(B_Q_RANK, -1)], axis=1)
    ukv = w_ukv[l].reshape(B_KV_RANK, B_HEADS, B_NOPE + B_V)
    ukv = jnp.concatenate([ukv[:, :, :B_NOPE].reshape(B_KV_RANK, -1), ukv[:, :, B_NOPE:].reshape(B_KV_RANK, -1)], axis=1)
    nk = C_HEADS * C_DK
    w_gk = jnp.zeros((128, 2 * nk), F32)
    w_gk = w_gk.at[0:C_GATE_RANK, 0:nk].set(w_gk_f[l]).at[C_GATE_RANK:2 * C_GATE_RANK, nk:].set(w_gk_b[l])
    lw = {k: v[l].reshape(1, -1) for k, v in smalls.items()}
    lw["b_q_rope_norm2"] = jnp.tile(lw["b_q_rope_norm"], (1, 2))
    lw["b_k_rope_norm2"] = jnp.tile(lw["b_k_rope_norm"], (1, 2))
    lw.update(w_in=w_r, w_uq=uq.astype(BF16), w_ukv=ukv.astype(BF16), w_gk=w_gk,
              b_gk=jnp.concatenate([b_gk_f[l], b_gk_b[l]]).reshape(1, -1),
              w_out=w_out[l].astype(BF16), w_gate=w_gate[l].astype(BF16),
              w_up=w_up[l].astype(BF16), w_down=w_down[l].astype(BF16))
    return lw


def kernel(x, c, ctx, c_ctx, w_mod, b_mod, norm1_g, norm2_g, w_in, a_q_norm, a_k_norm, b_q_lora_norm, b_kv_lora_norm, w_uq, w_ukv, b_q_nope_norm, b_k_nope_norm, b_q_rope_norm, b_k_rope_norm, w_gk_f, b_gk_f, w_gk_b, b_gk_b, c_out_norm, w_out, w_gate, w_up, w_down):
    bsz, n, d = x.shape
    depth = w_in.shape[0]
    tabs = _rope_tables(n)
    cpad = jnp.concatenate([c, c_ctx[None, :], jnp.zeros((8 - bsz - 1, d), F32)], axis=0)
    smalls = dict(a_q_norm=a_q_norm, a_k_norm=a_k_norm, b_q_lora_norm=b_q_lora_norm,
                  b_kv_lora_norm=b_kv_lora_norm, b_q_nope_norm=b_q_nope_norm, b_k_nope_norm=b_k_nope_norm,
                  b_q_rope_norm=b_q_rope_norm, b_k_rope_norm=b_k_rope_norm, c_out_norm=c_out_norm)
    s_zero = jnp.zeros((2, bsz, 2, 2 * C_DV, 2 * C_DK), F32)
    xc = ctx
    for l in range(depth):
        ctx_out = l < depth - 1
        lw = _layer_weights(l, w_in, w_uq, w_ukv, w_gk_f, b_gk_f, w_gk_b, b_gk_b,
                            w_out, w_gate, w_up, w_down, smalls)
        mod = _modulation(cpad, w_mod[l], b_mod[l])

        pc = _inproj(xc, mod, norm1_g[l], lw["w_in"], True)
        pl_ = _inproj(x, mod, norm1_g[l], lw["w_in"], False)
        caq, cak, cav, cbq, cbk, cbv, cgd = _post(pc, None, lw, False)
        laq, lak, lav, lbq, lbk, lbv, lgd = _post(pl_, tabs, lw, True)

        o_a = _attention(laq, cak, cav, lak, lav, kv_heads=A_KV_HEADS, groups=A_HEADS // A_KV_HEADS,
                         dq=A_HEAD_DIM, dv=A_HEAD_DIM, tq=256)
        o_b = _attention(lbq, cbk, cbv, lbk, lbv, kv_heads=B_HEADS, groups=1,
                         dq=B_QK_PAD, dv=B_V, tq=512)
        oc_c, s_ctx = _gla(pc, cgd, s_zero)
        o_c, _ = _gla(pl_, lgd, s_ctx)

        x = _outproj(o_a, o_b, o_c, pl_, lw["c_out_norm"], lw["w_out"], x, mod, False)
        x = _ffn(x, mod, norm2_g[l], lw["w_gate"], lw["w_up"], lw["w_down"], False)
        if ctx_out:
            oc_a = _attention(caq, cak, cav, None, None, kv_heads=A_KV_HEADS, groups=A_HEADS // A_KV_HEADS,
                              dq=A_HEAD_DIM, dv=A_HEAD_DIM, tq=256)
            oc_b = _attention(cbq, cbk, cbv, None, None, kv_heads=B_HEADS, groups=1,
                              dq=B_QK_PAD, dv=B_V, tq=256)
            xc = _outproj(oc_a, oc_b, oc_c, pc, lw["c_out_norm"], lw["w_out"], xc, mod, True)
            xc = _ffn(xc, mod, norm2_g[l], lw["w_gate"], lw["w_up"], lw["w_down"], True)
    return x
```

```python
import functools

import numpy as np
import jax
import jax.numpy as jnp
from jax import lax
from jax.experimental import pallas as pl
from jax.experimental.pallas import tpu as pltpu

F32 = jnp.float32
BF16 = jnp.bfloat16

D_MODEL = 2048
GRID_W = 64
ROPE_THETA = 10000.0
NORM_EPS = 1e-6

A_HEADS = 8
A_KV_HEADS = 2
A_HEAD_DIM = 128
B_HEADS = 4
B_Q_RANK = 512
B_KV_RANK = 256
B_NOPE = 128
B_ROPE = 64
B_V = 128
B_QK_PAD = 256
C_HEADS = 4
C_DK = 64
C_DV = 128
C_GATE_RANK = 16
C_GATE_NORM = 16.0

D_FF = 5632
P_COLS = 4096

COL_AQ = 0
COL_AK = 1024
COL_BCQ = 1536
COL_BCKV = 2048
COL_CQ = 2560
COL_CV = 3072
COL_CG = 3584

GLA_CHUNK = 128

VMEM_LIMIT = 56 * 2**20
LOG2E = 1.4426950408889634


def _cparams(sem, vmem=VMEM_LIMIT):
    return pltpu.CompilerParams(dimension_semantics=sem, vmem_limit_bytes=vmem)


def _silu(x):
    return x / (1.0 + jnp.exp(-x))


def _rms(x, g):
    ms = jnp.mean(x * x, axis=-1, keepdims=True)
    return x * lax.rsqrt(ms + NORM_EPS) * g


def _mod_kernel(c_ref, w_ref, b_ref, o_ref):
    s = _silu(c_ref[...]).astype(BF16)
    o_ref[...] = jnp.dot(s, w_ref[...].astype(BF16), preferred_element_type=F32) + b_ref[...]


def _modulation(cpad, w_mod, b_mod, layer):
    d = cpad.shape[1]
    n = w_mod.shape[2]
    tn = 1024
    out = pl.pallas_call(
        _mod_kernel,
        out_shape=jax.ShapeDtypeStruct((8, n), F32),
        grid=(n // tn,),
        in_specs=[pl.BlockSpec((8, d), lambda j: (0, 0)),
                  pl.BlockSpec((None, d, tn), lambda j: (layer, 0, j)),
                  pl.BlockSpec((None, 1, tn), lambda j: (layer, 0, j))],
        out_specs=pl.BlockSpec((8, tn), lambda j: (0, j)),
        compiler_params=_cparams(("arbitrary",)),
        name="modulation",
    )(cpad, w_mod, b_mod)
    return out.reshape(8, 6, d)


def _mod_index(ctx):
    if ctx:
        return lambda b, i, *_: (0, 0, 0)
    return lambda b, i, *_: (b + 1, 0, 0)


def _swap_halves(x, lane, width):
    return jnp.where((lane // width) % 2 == 0,
                     pltpu.roll(x, 128 - width, 1), pltpu.roll(x, width, 1))


def _proj_kernel(*refs, rope):
    x_ref, mod_ref, g1_ref, w_ref = refs[:4]
    refs = refs[4:]
    if rope:
        cosa_ref, sina_ref, cosb_ref, sinb_ref = refs[:4]
        refs = refs[4:]
    (gaq_ref, gak_ref, gql_ref, gkvl_ref, gqn_ref, gkn_ref, gqr_ref, gkr_ref,
     wuq_ref, wukv_ref, wgk_ref, bgk_ref,
     aq_ref, ak_ref, av_ref, bq_ref, bk_ref, bv_ref, gd_ref, cq_ref, ck_ref, cv_ref, cg_ref) = refs
    tm = x_ref.shape[1]
    lane = lax.broadcasted_iota(jnp.int32, (tm, 128), 1)
    lo = lane < 64

    y = _rms(x_ref[0], g1_ref[...])
    h = (y * (1.0 + mod_ref[0, 1:2, :]) + mod_ref[0, 0:1, :]).astype(BF16)

    def proj(c0, width):
        return jnp.dot(h, w_ref[:, c0:c0 + width], preferred_element_type=F32)

    def rope_a(x):
        if not rope:
            return x
        return x * cosa_ref[...] + _swap_halves(x, lane, 32) * sina_ref[...]

    def rope_b(x):
        if not rope:
            return x
        return x * cosb_ref[...] + _swap_halves(x, lane, 16) * sinb_ref[...]

    a_scale = LOG2E * A_HEAD_DIM ** -0.5
    for half in range(2):
        pq = proj(COL_AQ + half * 512, 512)
        for j in range(4):
            hh = half * 4 + j
            x = pq[:, j * 128:(j + 1) * 128]
            aq_ref[0, :, hh * 128:(hh + 1) * 128] = (rope_a(_rms(x, gaq_ref[...])) * a_scale).astype(BF16)
    pkv = proj(COL_AK, 512)
    for hh in range(A_KV_HEADS):
        x = pkv[:, hh * 128:(hh + 1) * 128]
        ak_ref[0, :, hh * 128:(hh + 1) * 128] = rope_a(_rms(x, gak_ref[...])).astype(BF16)
    av_ref[0] = pkv[:, 256:512].astype(BF16)

    b_scale = LOG2E * (B_NOPE + B_ROPE) ** -0.5
    cq = _rms(proj(COL_BCQ, 512), gql_ref[...]).astype(BF16)
    qb = jnp.dot(cq, wuq_ref[...], preferred_element_type=F32)
    for hh in range(B_HEADS):
        x = qb[:, hh * 128:(hh + 1) * 128]
        bq_ref[0, :, hh * B_QK_PAD:hh * B_QK_PAD + 128] = (_rms(x, gqn_ref[...]) * b_scale).astype(BF16)
    for p in range(B_HEADS // 2):
        r = qb[:, 512 + p * 128:512 + (p + 1) * 128]
        sq = r * r
        ms_lo = jnp.sum(jnp.where(lo, sq, 0.0), axis=-1, keepdims=True)
        ms_hi = jnp.sum(jnp.where(lo, 0.0, sq), axis=-1, keepdims=True)
        ms = jnp.where(lo, ms_lo, ms_hi) * (1.0 / B_ROPE)
        rn = rope_b(r * lax.rsqrt(ms + NORM_EPS) * gqr_ref[...]) * b_scale
        h0 = 2 * p
        bq_ref[0, :, h0 * B_QK_PAD + 128:(h0 + 1) * B_QK_PAD] = jnp.where(lo, rn, 0.0).astype(BF16)
        bq_ref[0, :, (h0 + 1) * B_QK_PAD + 128:(h0 + 2) * B_QK_PAD] = jnp.where(lo, 0.0, rn).astype(BF16)

    pmix = proj(COL_BCKV, 512)
    ckv = _rms(pmix[:, 0:256], gkvl_ref[...]).astype(BF16)
    kv = jnp.dot(ckv, wukv_ref[...], preferred_element_type=F32)
    krn = rope_b(_rms(pmix[:, 256:384], gkr_ref[...]))
    kr_even = jnp.where(lo, krn, 0.0).astype(BF16)
    kr_odd = jnp.where(lo, 0.0, krn).astype(BF16)
    for hh in range(B_HEADS):
        x = kv[:, hh * 128:(hh + 1) * 128]
        bk_ref[0, :, hh * B_QK_PAD:hh * B_QK_PAD + 128] = _rms(x, gkn_ref[...]).astype(BF16)
        bk_ref[0, :, hh * B_QK_PAD + 128:(hh + 1) * B_QK_PAD] = kr_even if hh % 2 == 0 else kr_odd
    bv_ref[0] = kv[:, 512:1024].astype(BF16)

    z = jnp.dot(pmix[:, 384:512], wgk_ref[...], preferred_element_type=F32,
                precision=lax.Precision.HIGHEST) + bgk_ref[...]
    gd_ref[0] = (jnp.minimum(z, 0.0) - jnp.log(1.0 + jnp.exp(-jnp.abs(z)))) * (1.0 / C_GATE_NORM)
    pqk = proj(COL_CQ, 512)
    cq_ref[0] = pqk[:, 0:256]
    ck_ref[0] = pqk[:, 256:512]
    cv_ref[0] = proj(COL_CV, 512).astype(BF16)
    cg_ref[0] = proj(COL_CG, 512)


def _proj(x, mod, g1, tabs, lw, layer, ctx):
    bsz, n, d = x.shape
    tm = min(512, n)
    rope = not ctx
    row = lambda b, i: (b, i, 0)
    once = pl.Buffered(1)
    in_specs = [pl.BlockSpec((1, tm, d), row),
                pl.BlockSpec((1, 6, d), _mod_index(ctx)),
                pl.BlockSpec((None, 1, d), lambda b, i: (layer, 0, 0)),
                pl.BlockSpec((None, d, P_COLS), lambda b, i: (layer, 0, 0), pipeline_mode=once)]
    args = [x, mod, g1, lw["w_in"]]
    if rope:
        in_specs += [pl.BlockSpec((tm, 128), lambda b, i: (i, 0))] * 4
        args += list(tabs)
    small = [lw["a_q_norm"], lw["a_k_norm"], lw["b_q_lora_norm"], lw["b_kv_lora_norm"],
             lw["b_q_nope_norm"], lw["b_k_nope_norm"], lw["b_q_rope_norm2"], lw["b_k_rope_norm2"],
             lw["w_uq"], lw["w_ukv"], lw["w_gk"], lw["b_gk"]]
    in_specs += [pl.BlockSpec((None,) + a.shape[1:], lambda b, i: (layer, 0, 0), pipeline_mode=once) for a in small]
    args += small
    out_shape = [jax.ShapeDtypeStruct((bsz, n, 1024), BF16),
                 jax.ShapeDtypeStruct((bsz, n, 256), BF16),
                 jax.ShapeDtypeStruct((bsz, n, 256), BF16),
                 jax.ShapeDtypeStruct((bsz, n, B_HEADS * B_QK_PAD), BF16),
                 jax.ShapeDtypeStruct((bsz, n, B_HEADS * B_QK_PAD), BF16),
                 jax.ShapeDtypeStruct((bsz, n, B_HEADS * B_V), BF16),
                 jax.ShapeDtypeStruct((bsz, n, 512), F32),
                 jax.ShapeDtypeStruct((bsz, n, 256), F32),
                 jax.ShapeDtypeStruct((bsz, n, 256), F32),
                 jax.ShapeDtypeStruct((bsz, n, 512), BF16),
                 jax.ShapeDtypeStruct((bsz, n, 512), F32)]
    out_specs = [pl.BlockSpec((1, tm, s.shape[2]), row) for s in out_shape]
    return pl.pallas_call(
        functools.partial(_proj_kernel, rope=rope),
        out_shape=out_shape,
        grid=(bsz, n // tm),
        in_specs=in_specs,
        out_specs=out_specs,
        compiler_params=_cparams(("parallel", "parallel")),
        name="proj_ctx" if ctx else "proj_lat",
    )(*args)


def _attn_kernel(*refs, groups, dq, dv, tk, rsub, has_lat):
    if has_lat:
        q_ref, kc_ref, vc_ref, kl_ref, vl_ref, o_ref, q_scr, s_scr, p_scr, m_scr, l_scr, acc_scr = refs
    else:
        q_ref, kc_ref, vc_ref, o_ref, q_scr, s_scr, p_scr, m_scr, l_scr, acc_scr = refs
    tq = q_ref.shape[1]
    m_rows = groups * tq
    nt = (((1,), (1,)), ((), ()))
    for g in range(groups):
        q_scr[g * tq:(g + 1) * tq, :] = q_ref[0, :, g * dq:(g + 1) * dq]
    m_scr[...] = jnp.full(m_scr.shape, -jnp.inf, F32)
    l_scr[...] = jnp.zeros(l_scr.shape, F32)
    acc_scr[...] = jnp.zeros(acc_scr.shape, F32)

    def scores(slot, k):
        s_scr[slot, :, 0:k.shape[0]] = lax.dot_general(q_scr[...], k, nt, preferred_element_type=F32)

    def softmax_pv(slot, v):
        w = v.shape[0]
        for r in range(m_rows // rsub):
            rs = slice(r * rsub, (r + 1) * rsub)
            cols = [s_scr[slot, rs, j * 128:(j + 1) * 128] for j in range(w // 128)]
            mx = functools.reduce(jnp.maximum, cols)
            m_old = m_scr[rs, :]
            m_new = jnp.maximum(m_old, jnp.max(mx, axis=-1, keepdims=True))
            ps = [jnp.exp2(c - m_new) for c in cols]
            a = jnp.exp2(m_old - m_new)
            l_scr[rs, :] = a * l_scr[rs, :] + functools.reduce(jnp.add, ps)
            m_scr[rs, :] = m_new
            acc_scr[rs, :] = a * acc_scr[rs, :]
            for j, p in enumerate(ps):
                p_scr[rs, j * 128:(j + 1) * 128] = p.astype(BF16)
        acc_scr[...] += jnp.dot(p_scr[:, 0:w], v, preferred_element_type=F32)

    scores(0, kc_ref[0])
    if has_lat:
        nlc = kl_ref.shape[1] // tk

        def lat(ref, c):
            return ref[0, pl.ds(pl.multiple_of(c * tk, tk), tk), :]

        scores(1, lat(kl_ref, 0))
        softmax_pv(0, vc_ref[0])

        def body(i, carry):
            scores(0, lat(kl_ref, 2 * i + 1))
            softmax_pv(1, lat(vl_ref, 2 * i))
            scores(1, lat(kl_ref, 2 * i + 2))
            softmax_pv(0, lat(vl_ref, 2 * i + 1))
            return carry

        lax.fori_loop(0, nlc // 2 - 1, body, 0)
        scores(0, lat(kl_ref, nlc - 1))
        softmax_pv(1, lat(vl_ref, nlc - 2))
        softmax_pv(0, lat(vl_ref, nlc - 1))
    else:
        softmax_pv(0, vc_ref[0])
    for g in range(groups):
        gs = slice(g * tq, (g + 1) * tq)
        l = jnp.sum(l_scr[gs, :], axis=-1, keepdims=True)
        o_ref[0, :, g * dv:(g + 1) * dv] = (acc_scr[gs, :] * (1.0 / l)).astype(o_ref.dtype)


def _attention(q, kc, vc, kl, vl, *, kv_heads, groups, dq, dv, tq, tk=1024, rsub=64):
    assert dv == 128, "the accumulator rescale reuses the 128-lane replicated running max"
    bsz, n, _ = q.shape
    nc = kc.shape[1]
    has_lat = kl is not None
    tq = min(tq, n)
    m_rows = groups * tq
    wmax = max(nc, tk) if has_lat else nc
    in_specs = [pl.BlockSpec((1, tq, groups * dq), lambda b, h, i: (b, i, h)),
                pl.BlockSpec((1, nc, dq), lambda b, h, i: (b, 0, h)),
                pl.BlockSpec((1, nc, dv), lambda b, h, i: (b, 0, h))]
    args = [q, kc, vc]
    if has_lat:
        nl = kl.shape[1]
        in_specs += [pl.BlockSpec((1, nl, dq), lambda b, h, i: (b, 0, h)),
                     pl.BlockSpec((1, nl, dv), lambda b, h, i: (b, 0, h))]
        args += [kl, vl]
    return pl.pallas_call(
        functools.partial(_attn_kernel, groups=groups, dq=dq, dv=dv, tk=tk, rsub=rsub, has_lat=has_lat),
        out_shape=jax.ShapeDtypeStruct((bsz, n, kv_heads * groups * dv), BF16),
        grid=(bsz, kv_heads, n // tq),
        in_specs=in_specs,
        out_specs=pl.BlockSpec((1, tq, groups * dv), lambda b, h, i: (b, i, h)),
        scratch_shapes=[pltpu.VMEM((m_rows, dq), BF16), pltpu.VMEM((2, m_rows, wmax), F32),
                        pltpu.VMEM((m_rows, wmax), BF16), pltpu.VMEM((m_rows, 128), F32),
                        pltpu.VMEM((m_rows, 128), F32), pltpu.VMEM((m_rows, dv), F32)],
        compiler_params=_cparams(("parallel", "parallel", "arbitrary")),
        name="attention",
    )(*args)


def _gla_kernel(q_ref, k_ref, v_ref, g_ref, s0_ref, o_ref, sfin_ref, st_ref, *, chunk):
    d = pl.program_id(0)
    i = pl.program_id(2)
    t_rows = q_ref.shape[1]
    nch = t_rows // chunk

    @pl.when(i == 0)
    def _():
        st_ref[...] = s0_ref[0, 0]

    row = lax.broadcasted_iota(jnp.int32, (chunk, chunk), 0)
    col = lax.broadcasted_iota(jnp.int32, (chunk, chunk), 1)
    tri = jnp.where(d == 0, col - row, row - col) <= 0
    tri_bf = jnp.where(tri, 1.0, 0.0).astype(BF16)
    row2 = lax.broadcasted_iota(jnp.int32, (chunk, 2 * chunk), 0)
    col2 = lax.broadcasted_iota(jnp.int32, (chunk, 2 * chunk), 1)
    col2 = jnp.where(col2 >= chunk, col2 - chunk, col2)
    tri2 = jnp.where(d == 0, col2 - row2, row2 - col2) <= 0
    lo = lax.broadcasted_iota(jnp.int32, (chunk, 128), 1) < C_DK
    vlo = lax.broadcasted_iota(jnp.int32, (chunk, 2 * C_DV), 1) < C_DV
    srow = lax.broadcasted_iota(jnp.int32, (2 * C_DV, 2 * C_DK), 0) // C_DV
    scol = lax.broadcasted_iota(jnp.int32, (2 * C_DV, 2 * C_DK), 1) // C_DK
    same_head = srow == scol
    nt = (((1,), (1,)), ((), ()))
    tn = (((0,), (0,)), ((), ()))

    for c in range(nch):
        cc = c + d * (nch - 1 - 2 * c)
        r0 = pl.multiple_of(cc * chunk, chunk)
        g = g_ref[0, pl.ds(r0, chunk), :]
        g_hi = g.astype(BF16)
        g_lo = (g - g_hi.astype(F32)).astype(BF16)
        cb = (jnp.dot(tri_bf, g_hi, preferred_element_type=F32)
              + jnp.dot(tri_bf, g_lo, preferred_element_type=F32))
        tot = jnp.sum(g, axis=0, keepdims=True)
        q = q_ref[0, pl.ds(r0, chunk), :] * (C_DK ** -0.5)
        k = k_ref[0, pl.ds(r0, chunk), :]
        v = v_ref[0, pl.ds(r0, chunk), :]
        qe = (q * jnp.exp(cb)).astype(BF16)
        kd = (k * jnp.exp(tot - cb)).astype(BF16)
        ke = (k * jnp.exp(-cb)).astype(BF16)
        dec = jnp.exp(tot)
        for p in range(C_HEADS // 2):
            ls = slice(p * 128, (p + 1) * 128)
            vs = slice(p * 2 * C_DV, (p + 1) * 2 * C_DV)
            qe_p, kd_p, ke_p, v_p = qe[:, ls], kd[:, ls], ke[:, ls], v[:, vs]
            zk = jnp.zeros_like(ke_p)
            ke_bd = jnp.concatenate([jnp.where(lo, ke_p, zk), jnp.where(lo, zk, ke_p)], axis=0)
            att = lax.dot_general(qe_p, ke_bd, nt, preferred_element_type=F32)
            att = jnp.where(tri2, att, 0.0).astype(BF16)
            zv = jnp.zeros_like(v_p)
            v_bd = jnp.concatenate([jnp.where(vlo, v_p, zv), jnp.where(vlo, zv, v_p)], axis=0)
            st = st_ref[p]
            o = (jnp.dot(att, v_bd, preferred_element_type=F32)
                 + lax.dot_general(qe_p, st.astype(BF16), nt, preferred_element_type=F32))
            o_ref[0, 0, pl.ds(r0, chunk), vs] = o
            u = lax.dot_general(v_p, kd_p, tn, preferred_element_type=F32)
            st_ref[p] = dec[:, ls] * st + jnp.where(same_head, u, 0.0)

    @pl.when(i == pl.num_programs(2) - 1)
    def _():
        sfin_ref[0, 0] = st_ref[...]


def _gla(cq, ck, cv, gdec, s0):
    bsz, n, _ = cq.shape
    t_rows = min(512, n)
    nb = n // t_rows
    rb = lambda d, i: i + d * (nb - 1 - 2 * i)
    o, sfin = pl.pallas_call(
        functools.partial(_gla_kernel, chunk=GLA_CHUNK),
        out_shape=[jax.ShapeDtypeStruct((2, bsz, n, C_HEADS * C_DV), F32),
                   jax.ShapeDtypeStruct(s0.shape, F32)],
        grid=(2, bsz, nb),
        in_specs=[pl.BlockSpec((1, t_rows, 256), lambda d, b, i: (b, rb(d, i), 0)),
                  pl.BlockSpec((1, t_rows, 256), lambda d, b, i: (b, rb(d, i), 0)),
                  pl.BlockSpec((1, t_rows, 512), lambda d, b, i: (b, rb(d, i), 0)),
                  pl.BlockSpec((1, t_rows, 256), lambda d, b, i: (b, rb(d, i), d)),
                  pl.BlockSpec((1, 1, 2, 256, 128), lambda d, b, i: (d, b, 0, 0, 0))],
        out_specs=[pl.BlockSpec((1, 1, t_rows, 512), lambda d, b, i: (d, b, rb(d, i), 0)),
                   pl.BlockSpec((1, 1, 2, 256, 128), lambda d, b, i: (d, b, 0, 0, 0))],
        scratch_shapes=[pltpu.VMEM((2, 256, 128), F32)],
        compiler_params=_cparams(("parallel", "parallel", "arbitrary")),
        name="gla",
    )(cq, ck, cv, gdec, s0)
    return o, sfin


def _outproj_kernel(oa_ref, ob_ref, oc_ref, cg_ref, gn_ref, w_ref, x_ref, mod_ref, o_ref, z_ref):
    z_ref[:, 0:1024] = oa_ref[0]
    z_ref[:, 1024:1536] = ob_ref[0]
    for h in range(C_HEADS):
        hs = slice(h * C_DV, (h + 1) * C_DV)
        oc = oc_ref[0, 0, :, hs] + oc_ref[1, 0, :, hs]
        z_ref[:, 1536 + h * C_DV:1536 + (h + 1) * C_DV] = (
            _rms(oc, gn_ref[...]) * _silu(cg_ref[0, :, hs])).astype(BF16)
    y = jnp.dot(z_ref[...], w_ref[...], preferred_element_type=F32)
    o_ref[0] = x_ref[0] + mod_ref[0, 2:3, :] * y


def _outproj(oa, ob, oc, cg, gn, w, x, mod, layer, ctx):
    bsz, n, d = x.shape
    tm = min(512, n)
    return pl.pallas_call(
        _outproj_kernel,
        out_shape=jax.ShapeDtypeStruct((bsz, n, d), F32),
        grid=(bsz, n // tm),
        in_specs=[pl.BlockSpec((1, tm, 1024), lambda b, i: (b, i, 0)),
                  pl.BlockSpec((1, tm, 512), lambda b, i: (b, i, 0)),
                  pl.BlockSpec((2, 1, tm, 512), lambda b, i: (0, b, i, 0)),
                  pl.BlockSpec((1, tm, 512), lambda b, i: (b, i, 0)),
                  pl.BlockSpec((None, 1, C_DV), lambda b, i: (layer, 0, 0)),
                  pl.BlockSpec((None,) + w.shape[1:], lambda b, i: (layer, 0, 0), pipeline_mode=pl.Buffered(1)),
                  pl.BlockSpec((1, tm, d), lambda b, i: (b, i, 0)),
                  pl.BlockSpec((1, 6, d), _mod_index(ctx))],
        out_specs=pl.BlockSpec((1, tm, d), lambda b, i: (b, i, 0)),
        scratch_shapes=[pltpu.VMEM((tm, w.shape[1]), BF16)],
        compiler_params=_cparams(("parallel", "parallel")),
        name="outproj",
    )(oa, ob, oc, cg, gn, w, x, mod)


def _ffn_kernel(x_ref, mod_ref, g_ref, wg_ref, wu_ref, wd_ref, o_ref, h_ref, acc_ref):
    j = pl.program_id(2)

    @pl.when(j == 0)
    def _():
        y = _rms(x_ref[0], g_ref[...])
        h_ref[...] = (y * (1.0 + mod_ref[0, 4:5, :]) + mod_ref[0, 3:4, :]).astype(BF16)
        acc_ref[...] = jnp.zeros_like(acc_ref)

    h = h_ref[...]
    a = jnp.dot(h, wg_ref[...], preferred_element_type=F32)
    u = jnp.dot(h, wu_ref[...], preferred_element_type=F32)
    t = (_silu(a) * u).astype(BF16)
    acc_ref[...] += jnp.dot(t, wd_ref[...], preferred_element_type=F32)

    @pl.when(j == pl.num_programs(2) - 1)
    def _():
        o_ref[0] = x_ref[0] + mod_ref[0, 5:6, :] * acc_ref[...]


def _ffn(x, mod, g, wg, wu, wd, layer, ctx):
    bsz, n, d = x.shape
    f = wg.shape[2]
    tm = min(512, n)
    tf = 512
    return pl.pallas_call(
        _ffn_kernel,
        out_shape=jax.ShapeDtypeStruct((bsz, n, d), F32),
        grid=(bsz, n // tm, f // tf),
        in_specs=[pl.BlockSpec((1, tm, d), lambda b, i, j: (b, i, 0)),
                  pl.BlockSpec((1, 6, d), _mod_index(ctx)),
                  pl.BlockSpec((None, 1, d), lambda b, i, j: (layer, 0, 0)),
                  pl.BlockSpec((None, d, tf), lambda b, i, j: (layer, 0, j)),
                  pl.BlockSpec((None, d, tf), lambda b, i, j: (layer, 0, j)),
                  pl.BlockSpec((None, tf, d), lambda b, i, j: (layer, j, 0))],
        out_specs=pl.BlockSpec((1, tm, d), lambda b, i, j: (b, i, 0)),
        scratch_shapes=[pltpu.VMEM((tm, d), BF16), pltpu.VMEM((tm, d), F32)],
        compiler_params=_cparams(("parallel", "parallel", "arbitrary")),
        name="ffn",
    )(x, mod, g, wg, wu, wd)


def _rope_tables(n):
    pos = np.arange(n)
    row = (pos // GRID_W).astype(np.float64)[:, None]
    col = (pos % GRID_W).astype(np.float64)[:, None]

    def table(half):
        inv = ROPE_THETA ** (-np.arange(half, dtype=np.float64) / half)
        ar, ac = row * inv[None, :], col * inv[None, :]
        cos = np.concatenate([np.cos(ar), np.cos(ar), np.cos(ac), np.cos(ac)], axis=1)
        sin = np.concatenate([-np.sin(ar), np.sin(ar), -np.sin(ac), np.sin(ac)], axis=1)
        return cos, sin

    cos_a, sin_a = table(A_HEAD_DIM // 4)
    cos_b, sin_b = table(B_ROPE // 4)
    cos_b, sin_b = np.tile(cos_b, (1, 2)), np.tile(sin_b, (1, 2))
    return tuple(jnp.asarray(t, F32) for t in (cos_a, sin_a, cos_b, sin_b))


def _prepare_weights(w_in, w_uq, w_ukv, w_gk_f, b_gk_f, w_gk_b, b_gk_b, w_out, w_gate, w_up, w_down, smalls):
    depth, d, _ = w_in.shape
    w_r = jnp.concatenate([
        w_in[:, :, 0:2304],
        w_in[:, :, 2304:2368], w_in[:, :, 2304:2368],
        w_in[:, :, 3904:3936], jnp.zeros((depth, d, 96), w_in.dtype),
        w_in[:, :, 2368:3904],
    ], axis=2).astype(BF16)
    uq = w_uq.reshape(depth, B_Q_RANK, B_HEADS, B_NOPE + B_ROPE)
    uq = jnp.concatenate([uq[..., :B_NOPE].reshape(depth, B_Q_RANK, -1),
                          uq[..., B_NOPE:].reshape(depth, B_Q_RANK, -1)], axis=2)
    ukv = w_ukv.reshape(depth, B_KV_RANK, B_HEADS, B_NOPE + B_V)
    ukv = jnp.concatenate([ukv[..., :B_NOPE].reshape(depth, B_KV_RANK, -1),
                           ukv[..., B_NOPE:].reshape(depth, B_KV_RANK, -1)], axis=2)
    nk = C_HEADS * C_DK
    w_gk = jnp.zeros((depth, 128, 2 * nk), F32)
    w_gk = w_gk.at[:, 0:C_GATE_RANK, 0:nk].set(w_gk_f).at[:, C_GATE_RANK:2 * C_GATE_RANK, nk:].set(w_gk_b)
    lw = {k: v.reshape(depth, 1, -1) for k, v in smalls.items()}
    lw["b_q_rope_norm2"] = jnp.tile(lw["b_q_rope_norm"], (1, 1, 2))
    lw["b_k_rope_norm2"] = jnp.tile(lw["b_k_rope_norm"], (1, 1, 2))
    lw.update(w_in=w_r, w_uq=uq.astype(BF16), w_ukv=ukv.astype(BF16), w_gk=w_gk,
              b_gk=jnp.concatenate([b_gk_f, b_gk_b], axis=1).reshape(depth, 1, -1),
              w_out=w_out.astype(BF16), w_gate=w_gate.astype(BF16),
              w_up=w_up.astype(BF16), w_down=w_down.astype(BF16))
    return lw


def kernel(x, c, ctx, c_ctx, w_mod, b_mod, norm1_g, norm2_g, w_in, a_q_norm, a_k_norm, b_q_lora_norm, b_kv_lora_norm, w_uq, w_ukv, b_q_nope_norm, b_k_nope_norm, b_q_rope_norm, b_k_rope_norm, w_gk_f, b_gk_f, w_gk_b, b_gk_b, c_out_norm, w_out, w_gate, w_up, w_down):
    bsz, n, d = x.shape
    depth = w_in.shape[0]
    tabs = _rope_tables(n)
    assert bsz + 1 <= 8, "the modulation kernel holds the context row and all batch rows in one 8-row tile"
    cpad = jnp.concatenate([c_ctx[None, :], c, jnp.zeros((8 - bsz - 1, d), F32)], axis=0)
    smalls = dict(a_q_norm=a_q_norm, a_k_norm=a_k_norm, b_q_lora_norm=b_q_lora_norm,
                  b_kv_lora_norm=b_kv_lora_norm, b_q_nope_norm=b_q_nope_norm, b_k_nope_norm=b_k_nope_norm,
                  b_q_rope_norm=b_q_rope_norm, b_k_rope_norm=b_k_rope_norm, c_out_norm=c_out_norm,
                  norm1_g=norm1_g, norm2_g=norm2_g)
    lw = _prepare_weights(w_in, w_uq, w_ukv, w_gk_f, b_gk_f, w_gk_b, b_gk_b, w_out, w_gate, w_up, w_down, smalls)
    b_mod3 = b_mod.reshape(depth, 1, -1)
    s_zero = jnp.zeros((2, bsz, 2, 2 * C_DV, 2 * C_DK), F32)
    a_args = dict(kv_heads=A_KV_HEADS, groups=A_HEADS // A_KV_HEADS, dq=A_HEAD_DIM, dv=A_HEAD_DIM)
    b_args = dict(kv_heads=B_HEADS, groups=1, dq=B_QK_PAD, dv=B_V)
    xc = ctx
    for l in range(depth):
        ctx_out = l < depth - 1
        mod = _modulation(cpad, w_mod, b_mod3, l)
        caq, cak, cav, cbq, cbk, cbv, cgd, ccq, cck, ccv, ccg = _proj(xc, mod, lw["norm1_g"], None, lw, l, True)
        laq, lak, lav, lbq, lbk, lbv, lgd, lcq, lck, lcv, lcg = _proj(x, mod, lw["norm1_g"], tabs, lw, l, False)

        o_a = _attention(laq, cak, cav, lak, lav, tq=256, **a_args)
        o_b = _attention(lbq, cbk, cbv, lbk, lbv, tq=512, **b_args)
        oc_c, s_ctx = _gla(ccq, cck, ccv, cgd, s_zero)
        o_c, _ = _gla(lcq, lck, lcv, lgd, s_ctx)

        x = _outproj(o_a, o_b, o_c, lcg, lw["c_out_norm"], lw["w_out"], x, mod, l, False)
        x = _ffn(x, mod, lw["norm2_g"], lw["w_gate"], lw["w_up"], lw["w_down"], l, False)
        if ctx_out:
            oc_a = _attention(caq, cak, cav, None, None, tq=256, **a_args)
            oc_b = _attention(cbq, cbk, cbv, None, None, tq=256, **b_args)
            xc = _outproj(oc_a, oc_b, oc_c, ccg, lw["c_out_norm"], lw["w_out"], xc, mod, l, True)
            xc = _ffn(xc, mod, lw["norm2_g"], lw["w_gate"], lw["w_up"], lw["w_down"], l, True)
    return x
```

```python
import functools

import numpy as np
import jax
import jax.numpy as jnp
from jax import lax
from jax.experimental import pallas as pl
from jax.experimental.pallas import tpu as pltpu

F32 = jnp.float32
BF16 = jnp.bfloat16

D_MODEL = 2048
GRID_W = 64
ROPE_THETA = 10000.0
NORM_EPS = 1e-6

A_HEADS = 8
A_KV_HEADS = 2
A_HEAD_DIM = 128
B_HEADS = 4
B_Q_RANK = 512
B_KV_RANK = 256
B_NOPE = 128
B_ROPE = 64
B_V = 128
B_QK_PAD = 256
C_HEADS = 4
C_DK = 64
C_DV = 128
C_GATE_RANK = 16
C_GATE_NORM = 16.0

D_FF = 5632
P_COLS = 4096

COL_AQ = 0
COL_AK = 1024
COL_BCQ = 1536
COL_BCKV = 2048
COL_CQ = 2560
COL_CV = 3072
COL_CG = 3584

GLA_CHUNK = 128
GLA_SAFE_DECAY = 60.0

VMEM_LIMIT = 56 * 2**20
LOG2E = 1.4426950408889634


def _cparams(sem, vmem=VMEM_LIMIT):
    return pltpu.CompilerParams(dimension_semantics=sem, vmem_limit_bytes=vmem)


def _silu(x):
    return x / (1.0 + jnp.exp(-x))


def _rms(x, g):
    ms = jnp.mean(x * x, axis=-1, keepdims=True)
    return x * lax.rsqrt(ms + NORM_EPS) * g


def _mod_kernel(c_ref, w_ref, b_ref, o_ref):
    s = _silu(c_ref[...]).astype(BF16)
    o_ref[...] = jnp.dot(s, w_ref[...].astype(BF16), preferred_element_type=F32) + b_ref[...]


def _modulation(cpad, w_mod, b_mod, layer):
    d = cpad.shape[1]
    n = w_mod.shape[2]
    tn = 1024
    out = pl.pallas_call(
        _mod_kernel,
        out_shape=jax.ShapeDtypeStruct((8, n), F32),
        grid=(n // tn,),
        in_specs=[pl.BlockSpec((8, d), lambda j: (0, 0)),
                  pl.BlockSpec((None, d, tn), lambda j: (layer, 0, j)),
                  pl.BlockSpec((None, 1, tn), lambda j: (layer, 0, j))],
        out_specs=pl.BlockSpec((8, tn), lambda j: (0, j)),
        compiler_params=_cparams(("arbitrary",)),
        name="modulation",
    )(cpad, w_mod, b_mod)
    return out.reshape(8, 6, d)


def _mod_index(ctx):
    if ctx:
        return lambda b, i, *_: (0, 0, 0)
    return lambda b, i, *_: (b + 1, 0, 0)


def _swap_halves(x, lane, width):
    return jnp.where((lane // width) % 2 == 0,
                     pltpu.roll(x, 128 - width, 1), pltpu.roll(x, width, 1))


def _proj_kernel(*refs, rope):
    x_ref, mod_ref, g1_ref, w_ref = refs[:4]
    refs = refs[4:]
    if rope:
        cosa_ref, sina_ref, cosb_ref, sinb_ref = refs[:4]
        refs = refs[4:]
    (gaq_ref, gak_ref, gql_ref, gkvl_ref, gqn_ref, gkn_ref, gqr_ref, gkr_ref,
     wuq_ref, wukv_ref, wgk_ref, bgk_ref,
     aq_ref, ak_ref, av_ref, bq_ref, bk_ref, bv_ref, gd_ref, cq_ref, ck_ref, cv_ref, cg_ref) = refs
    tm = x_ref.shape[1]
    lane = lax.broadcasted_iota(jnp.int32, (tm, 128), 1)
    lo = lane < 64

    y = _rms(x_ref[0], g1_ref[...])
    h = (y * (1.0 + mod_ref[0, 1:2, :]) + mod_ref[0, 0:1, :]).astype(BF16)

    def proj(c0, width):
        return jnp.dot(h, w_ref[:, c0:c0 + width], preferred_element_type=F32)

    def rope_a(x):
        if not rope:
            return x
        return x * cosa_ref[...] + _swap_halves(x, lane, 32) * sina_ref[...]

    def rope_b(x):
        if not rope:
            return x
        return x * cosb_ref[...] + _swap_halves(x, lane, 16) * sinb_ref[...]

    a_scale = LOG2E * A_HEAD_DIM ** -0.5
    for half in range(2):
        pq = proj(COL_AQ + half * 512, 512)
        for j in range(4):
            hh = half * 4 + j
            x = pq[:, j * 128:(j + 1) * 128]
            aq_ref[0, :, hh * 128:(hh + 1) * 128] = (rope_a(_rms(x, gaq_ref[...])) * a_scale).astype(BF16)
    pkv = proj(COL_AK, 512)
    for hh in range(A_KV_HEADS):
        x = pkv[:, hh * 128:(hh + 1) * 128]
        ak_ref[0, :, hh * 128:(hh + 1) * 128] = rope_a(_rms(x, gak_ref[...])).astype(BF16)
    av_ref[0] = pkv[:, 256:512].astype(BF16)

    b_scale = LOG2E * (B_NOPE + B_ROPE) ** -0.5
    cq = _rms(proj(COL_BCQ, 512), gql_ref[...]).astype(BF16)
    qb = jnp.dot(cq, wuq_ref[...], preferred_element_type=F32)
    for hh in range(B_HEADS):
        x = qb[:, hh * 128:(hh + 1) * 128]
        bq_ref[0, :, hh * B_QK_PAD:hh * B_QK_PAD + 128] = (_rms(x, gqn_ref[...]) * b_scale).astype(BF16)
    for p in range(B_HEADS // 2):
        r = qb[:, 512 + p * 128:512 + (p + 1) * 128]
        sq = r * r
        ms_lo = jnp.sum(jnp.where(lo, sq, 0.0), axis=-1, keepdims=True)
        ms_hi = jnp.sum(jnp.where(lo, 0.0, sq), axis=-1, keepdims=True)
        ms = jnp.where(lo, ms_lo, ms_hi) * (1.0 / B_ROPE)
        rn = rope_b(r * lax.rsqrt(ms + NORM_EPS) * gqr_ref[...]) * b_scale
        h0 = 2 * p
        bq_ref[0, :, h0 * B_QK_PAD + 128:(h0 + 1) * B_QK_PAD] = jnp.where(lo, rn, 0.0).astype(BF16)
        bq_ref[0, :, (h0 + 1) * B_QK_PAD + 128:(h0 + 2) * B_QK_PAD] = jnp.where(lo, 0.0, rn).astype(BF16)

    pmix = proj(COL_BCKV, 512)
    ckv = _rms(pmix[:, 0:256], gkvl_ref[...]).astype(BF16)
    kv = jnp.dot(ckv, wukv_ref[...], preferred_element_type=F32)
    krn = rope_b(_rms(pmix[:, 256:384], gkr_ref[...]))
    kr_even = jnp.where(lo, krn, 0.0).astype(BF16)
    kr_odd = jnp.where(lo, 0.0, krn).astype(BF16)
    for hh in range(B_HEADS):
        x = kv[:, hh * 128:(hh + 1) * 128]
        bk_ref[0, :, hh * B_QK_PAD:hh * B_QK_PAD + 128] = _rms(x, gkn_ref[...]).astype(BF16)
        bk_ref[0, :, hh * B_QK_PAD + 128:(hh + 1) * B_QK_PAD] = kr_even if hh % 2 == 0 else kr_odd
    bv_ref[0] = kv[:, 512:1024].astype(BF16)

    z = jnp.dot(pmix[:, 384:512], wgk_ref[...], preferred_element_type=F32,
                precision=lax.Precision.HIGHEST) + bgk_ref[...]
    gd_ref[0] = (jnp.minimum(z, 0.0) - jnp.log(1.0 + jnp.exp(-jnp.abs(z)))) * (1.0 / C_GATE_NORM)
    pqk = proj(COL_CQ, 512)
    cq_ref[0] = pqk[:, 0:256]
    ck_ref[0] = pqk[:, 256:512]
    cv_ref[0] = proj(COL_CV, 512).astype(BF16)
    cg_ref[0] = proj(COL_CG, 512)


def _proj(x, mod, g1, tabs, lw, layer, ctx):
    bsz, n, d = x.shape
    tm = min(512, n)
    rope = not ctx
    row = lambda b, i: (b, i, 0)
    once = pl.Buffered(1)
    in_specs = [pl.BlockSpec((1, tm, d), row),
                pl.BlockSpec((1, 6, d), _mod_index(ctx)),
                pl.BlockSpec((None, 1, d), lambda b, i: (layer, 0, 0)),
                pl.BlockSpec((None, d, P_COLS), lambda b, i: (layer, 0, 0), pipeline_mode=once)]
    args = [x, mod, g1, lw["w_in"]]
    if rope:
        in_specs += [pl.BlockSpec((tm, 128), lambda b, i: (i, 0))] * 4
        args += list(tabs)
    small = [lw["a_q_norm"], lw["a_k_norm"], lw["b_q_lora_norm"], lw["b_kv_lora_norm"],
             lw["b_q_nope_norm"], lw["b_k_nope_norm"], lw["b_q_rope_norm2"], lw["b_k_rope_norm2"],
             lw["w_uq"], lw["w_ukv"], lw["w_gk"], lw["b_gk"]]
    in_specs += [pl.BlockSpec((None,) + a.shape[1:], lambda b, i: (layer, 0, 0), pipeline_mode=once) for a in small]
    args += small
    out_shape = [jax.ShapeDtypeStruct((bsz, n, 1024), BF16),
                 jax.ShapeDtypeStruct((bsz, n, 256), BF16),
                 jax.ShapeDtypeStruct((bsz, n, 256), BF16),
                 jax.ShapeDtypeStruct((bsz, n, B_HEADS * B_QK_PAD), BF16),
                 jax.ShapeDtypeStruct((bsz, n, B_HEADS * B_QK_PAD), BF16),
                 jax.ShapeDtypeStruct((bsz, n, B_HEADS * B_V), BF16),
                 jax.ShapeDtypeStruct((bsz, n, 512), F32),
                 jax.ShapeDtypeStruct((bsz, n, 256), F32),
                 jax.ShapeDtypeStruct((bsz, n, 256), F32),
                 jax.ShapeDtypeStruct((bsz, n, 512), BF16),
                 jax.ShapeDtypeStruct((bsz, n, 512), F32)]
    out_specs = [pl.BlockSpec((1, tm, s.shape[2]), row) for s in out_shape]
    return pl.pallas_call(
        functools.partial(_proj_kernel, rope=rope),
        out_shape=out_shape,
        grid=(bsz, n // tm),
        in_specs=in_specs,
        out_specs=out_specs,
        compiler_params=_cparams(("parallel", "parallel")),
        name="proj_ctx" if ctx else "proj_lat",
    )(*args)


def _attn_kernel(*refs, groups, dq, dv, tk, rsub, has_lat):
    if has_lat:
        q_ref, kc_ref, vc_ref, kl_ref, vl_ref, o_ref, q_scr, s_scr, p_scr, m_scr, l_scr, acc_scr = refs
    else:
        q_ref, kc_ref, vc_ref, o_ref, q_scr, s_scr, p_scr, m_scr, l_scr, acc_scr = refs
    tq = q_ref.shape[1]
    m_rows = groups * tq
    nt = (((1,), (1,)), ((), ()))
    for g in range(groups):
        q_scr[g * tq:(g + 1) * tq, :] = q_ref[0, :, g * dq:(g + 1) * dq]
    m_scr[...] = jnp.full(m_scr.shape, -jnp.inf, F32)
    l_scr[...] = jnp.zeros(l_scr.shape, F32)
    acc_scr[...] = jnp.zeros(acc_scr.shape, F32)

    def scores(slot, k):
        s_scr[slot, :, 0:k.shape[0]] = lax.dot_general(q_scr[...], k, nt, preferred_element_type=F32)

    def softmax_pv(slot, v):
        w = v.shape[0]
        for r in range(m_rows // rsub):
            rs = slice(r * rsub, (r + 1) * rsub)
            cols = [s_scr[slot, rs, j * 128:(j + 1) * 128] for j in range(w // 128)]
            mx = functools.reduce(jnp.maximum, cols)
            m_old = m_scr[rs, :]
            m_new = jnp.maximum(m_old, jnp.max(mx, axis=-1, keepdims=True))
            ps = [jnp.exp2(c - m_new) for c in cols]
            a = jnp.exp2(m_old - m_new)
            l_scr[rs, :] = a * l_scr[rs, :] + functools.reduce(jnp.add, ps)
            m_scr[rs, :] = m_new
            acc_scr[rs, :] = a * acc_scr[rs, :]
            for j, p in enumerate(ps):
                p_scr[rs, j * 128:(j + 1) * 128] = p.astype(BF16)
        acc_scr[...] += jnp.dot(p_scr[:, 0:w], v, preferred_element_type=F32)

    scores(0, kc_ref[0])
    if has_lat:
        nlc = kl_ref.shape[1] // tk

        def lat(ref, c):
            return ref[0, pl.ds(pl.multiple_of(c * tk, tk), tk), :]

        scores(1, lat(kl_ref, 0))
        softmax_pv(0, vc_ref[0])

        def body(i, carry):
            scores(0, lat(kl_ref, 2 * i + 1))
            softmax_pv(1, lat(vl_ref, 2 * i))
            scores(1, lat(kl_ref, 2 * i + 2))
            softmax_pv(0, lat(vl_ref, 2 * i + 1))
            return carry

        lax.fori_loop(0, nlc // 2 - 1, body, 0)
        scores(0, lat(kl_ref, nlc - 1))
        softmax_pv(1, lat(vl_ref, nlc - 2))
        softmax_pv(0, lat(vl_ref, nlc - 1))
    else:
        softmax_pv(0, vc_ref[0])
    for g in range(groups):
        gs = slice(g * tq, (g + 1) * tq)
        l = jnp.sum(l_scr[gs, :], axis=-1, keepdims=True)
        o_ref[0, :, g * dv:(g + 1) * dv] = (acc_scr[gs, :] * (1.0 / l)).astype(o_ref.dtype)


def _attention(q, kc, vc, kl, vl, *, kv_heads, groups, dq, dv, tq, tk=1024, rsub=64):
    assert dv == 128, "the accumulator rescale reuses the 128-lane replicated running max"
    bsz, n, _ = q.shape
    nc = kc.shape[1]
    has_lat = kl is not None
    tq = min(tq, n)
    m_rows = groups * tq
    wmax = max(nc, tk) if has_lat else nc
    in_specs = [pl.BlockSpec((1, tq, groups * dq), lambda b, h, i: (b, i, h)),
                pl.BlockSpec((1, nc, dq), lambda b, h, i: (b, 0, h)),
                pl.BlockSpec((1, nc, dv), lambda b, h, i: (b, 0, h))]
    args = [q, kc, vc]
    if has_lat:
        nl = kl.shape[1]
        in_specs += [pl.BlockSpec((1, nl, dq), lambda b, h, i: (b, 0, h)),
                     pl.BlockSpec((1, nl, dv), lambda b, h, i: (b, 0, h))]
        args += [kl, vl]
    return pl.pallas_call(
        functools.partial(_attn_kernel, groups=groups, dq=dq, dv=dv, tk=tk, rsub=rsub, has_lat=has_lat),
        out_shape=jax.ShapeDtypeStruct((bsz, n, kv_heads * groups * dv), BF16),
        grid=(bsz, kv_heads, n // tq),
        in_specs=in_specs,
        out_specs=pl.BlockSpec((1, tq, groups * dv), lambda b, h, i: (b, i, h)),
        scratch_shapes=[pltpu.VMEM((m_rows, dq), BF16), pltpu.VMEM((2, m_rows, wmax), F32),
                        pltpu.VMEM((m_rows, wmax), BF16), pltpu.VMEM((m_rows, 128), F32),
                        pltpu.VMEM((m_rows, 128), F32), pltpu.VMEM((m_rows, dv), F32)],
        compiler_params=_cparams(("parallel", "parallel", "arbitrary")),
        name="attention",
    )(*args)


def _gla_kernel(q_ref, k_ref, v_ref, g_ref, s0_ref, o_ref, sfin_ref, st_ref, att_ref, cb_ref, kk_ref, *, chunk):
    d = pl.program_id(0)
    i = pl.program_id(2)
    t_rows = q_ref.shape[1]
    nch = t_rows // chunk

    @pl.when(i == 0)
    def _():
        st_ref[...] = s0_ref[0, 0]

    row = lax.broadcasted_iota(jnp.int32, (chunk, chunk), 0)
    col = lax.broadcasted_iota(jnp.int32, (chunk, chunk), 1)
    tri = jnp.where(d == 0, col - row, row - col) <= 0
    tri_bf = jnp.where(tri, 1.0, 0.0).astype(BF16)
    row2 = lax.broadcasted_iota(jnp.int32, (chunk, 2 * chunk), 0)
    lane2 = lax.broadcasted_iota(jnp.int32, (chunk, 2 * chunk), 1)
    col2 = jnp.where(lane2 >= chunk, lane2 - chunk, lane2)
    tri2 = jnp.where(d == 0, col2 - row2, row2 - col2) <= 0
    lo = lax.broadcasted_iota(jnp.int32, (chunk, 128), 1) < C_DK
    vlo = lax.broadcasted_iota(jnp.int32, (chunk, 2 * C_DV), 1) < C_DV
    srow = lax.broadcasted_iota(jnp.int32, (2 * C_DV, 2 * C_DK), 0) // C_DV
    scol = lax.broadcasted_iota(jnp.int32, (2 * C_DV, 2 * C_DK), 1) // C_DK
    same_head = srow == scol
    nt = (((1,), (1,)), ((), ()))
    tn = (((0,), (0,)), ((), ()))

    def intra_scores_mxu(q_p, k_p, cb_p, qe_p):
        ke_p = (k_p * jnp.exp(-cb_p)).astype(BF16)
        zk = jnp.zeros_like(ke_p)
        ke_bd = jnp.concatenate([jnp.where(lo, ke_p, zk), jnp.where(lo, zk, ke_p)], axis=0)
        return lax.dot_general(qe_p, ke_bd, nt, preferred_element_type=F32)

    def intra_scores_guarded(q_p, k_p, cb_p, qe_p):
        cb_ref[...] = cb_p
        kk_ref[...] = k_p
        att_ref[...] = jnp.zeros(att_ref.shape, F32)

        def one_source(s, carry):
            cb_s = cb_ref[pl.ds(s, 1), :]
            w = q_p * jnp.exp(jnp.minimum(cb_p - cb_s, 0.0)) * kk_ref[pl.ds(s, 1), :]
            w0 = jnp.sum(jnp.where(lo, w, 0.0), axis=-1, keepdims=True)
            w1 = jnp.sum(jnp.where(lo, 0.0, w), axis=-1, keepdims=True)
            att_ref[...] += jnp.where(lane2 == s, w0, 0.0) + jnp.where(lane2 == s + chunk, w1, 0.0)
            return carry

        lax.fori_loop(0, chunk, one_source, 0)
        return att_ref[...]

    def scan_block(intra_scores):
        for c in range(nch):
            cc = c + d * (nch - 1 - 2 * c)
            r0 = pl.multiple_of(cc * chunk, chunk)
            g = g_ref[0, pl.ds(r0, chunk), :]
            g_hi = g.astype(BF16)
            g_lo = (g - g_hi.astype(F32)).astype(BF16)
            cb = (jnp.dot(tri_bf, g_hi, preferred_element_type=F32)
                  + jnp.dot(tri_bf, g_lo, preferred_element_type=F32))
            tot = jnp.sum(g, axis=0, keepdims=True)
            q = q_ref[0, pl.ds(r0, chunk), :] * (C_DK ** -0.5)
            k = k_ref[0, pl.ds(r0, chunk), :]
            v = v_ref[0, pl.ds(r0, chunk), :]
            qe = (q * jnp.exp(cb)).astype(BF16)
            kd = (k * jnp.exp(tot - cb)).astype(BF16)
            dec = jnp.exp(tot)
            for p in range(C_HEADS // 2):
                ls = slice(p * 128, (p + 1) * 128)
                vs = slice(p * 2 * C_DV, (p + 1) * 2 * C_DV)
                qe_p, kd_p, v_p = qe[:, ls], kd[:, ls], v[:, vs]
                att = intra_scores(q[:, ls], k[:, ls], cb[:, ls], qe_p)
                att = jnp.where(tri2, att, 0.0).astype(BF16)
                zv = jnp.zeros_like(v_p)
                v_bd = jnp.concatenate([jnp.where(vlo, v_p, zv), jnp.where(vlo, zv, v_p)], axis=0)
                st = st_ref[p]
                o = (jnp.dot(att, v_bd, preferred_element_type=F32)
                     + lax.dot_general(qe_p, st.astype(BF16), nt, preferred_element_type=F32))
                o_ref[0, 0, pl.ds(r0, chunk), vs] = o
                u = lax.dot_general(v_p, kd_p, tn, preferred_element_type=F32)
                st_ref[p] = dec[:, ls] * st + jnp.where(same_head, u, 0.0)

    worst = functools.reduce(jnp.maximum, [
        jnp.max(-jnp.sum(g_ref[0, c * chunk:(c + 1) * chunk, :], axis=0, keepdims=True)) for c in range(nch)])
    safe = worst <= GLA_SAFE_DECAY

    @pl.when(safe)
    def _():
        scan_block(intra_scores_mxu)

    @pl.when(jnp.logical_not(safe))
    def _():
        scan_block(intra_scores_guarded)

    @pl.when(i == pl.num_programs(2) - 1)
    def _():
        sfin_ref[0, 0] = st_ref[...]


def _gla(cq, ck, cv, gdec, s0):
    bsz, n, _ = cq.shape
    t_rows = min(512, n)
    nb = n // t_rows
    rb = lambda d, i: i + d * (nb - 1 - 2 * i)
    o, sfin = pl.pallas_call(
        functools.partial(_gla_kernel, chunk=GLA_CHUNK),
        out_shape=[jax.ShapeDtypeStruct((2, bsz, n, C_HEADS * C_DV), F32),
                   jax.ShapeDtypeStruct(s0.shape, F32)],
        grid=(2, bsz, nb),
        in_specs=[pl.BlockSpec((1, t_rows, 256), lambda d, b, i: (b, rb(d, i), 0)),
                  pl.BlockSpec((1, t_rows, 256), lambda d, b, i: (b, rb(d, i), 0)),
                  pl.BlockSpec((1, t_rows, 512), lambda d, b, i: (b, rb(d, i), 0)),
                  pl.BlockSpec((1, t_rows, 256), lambda d, b, i: (b, rb(d, i), d)),
                  pl.BlockSpec((1, 1, 2, 256, 128), lambda d, b, i: (d, b, 0, 0, 0))],
        out_specs=[pl.BlockSpec((1, 1, t_rows, 512), lambda d, b, i: (d, b, rb(d, i), 0)),
                   pl.BlockSpec((1, 1, 2, 256, 128), lambda d, b, i: (d, b, 0, 0, 0))],
        scratch_shapes=[pltpu.VMEM((2, 256, 128), F32), pltpu.VMEM((GLA_CHUNK, 2 * GLA_CHUNK), F32),
                        pltpu.VMEM((GLA_CHUNK, 128), F32), pltpu.VMEM((GLA_CHUNK, 128), F32)],
        compiler_params=_cparams(("parallel", "parallel", "arbitrary")),
        name="gla",
    )(cq, ck, cv, gdec, s0)
    return o, sfin


def _outproj_kernel(oa_ref, ob_ref, oc_ref, cg_ref, gn_ref, w_ref, x_ref, mod_ref, o_ref, z_ref):
    z_ref[:, 0:1024] = oa_ref[0]
    z_ref[:, 1024:1536] = ob_ref[0]
    for h in range(C_HEADS):
        hs = slice(h * C_DV, (h + 1) * C_DV)
        oc = oc_ref[0, 0, :, hs] + oc_ref[1, 0, :, hs]
        z_ref[:, 1536 + h * C_DV:1536 + (h + 1) * C_DV] = (
            _rms(oc, gn_ref[...]) * _silu(cg_ref[0, :, hs])).astype(BF16)
    y = jnp.dot(z_ref[...], w_ref[...], preferred_element_type=F32)
    o_ref[0] = x_ref[0] + mod_ref[0, 2:3, :] * y


def _outproj(oa, ob, oc, cg, gn, w, x, mod, layer, ctx):
    bsz, n, d = x.shape
    tm = min(512, n)
    return pl.pallas_call(
        _outproj_kernel,
        out_shape=jax.ShapeDtypeStruct((bsz, n, d), F32),
        grid=(bsz, n // tm),
        in_specs=[pl.BlockSpec((1, tm, 1024), lambda b, i: (b, i, 0)),
                  pl.BlockSpec((1, tm, 512), lambda b, i: (b, i, 0)),
                  pl.BlockSpec((2, 1, tm, 512), lambda b, i: (0, b, i, 0)),
                  pl.BlockSpec((1, tm, 512), lambda b, i: (b, i, 0)),
                  pl.BlockSpec((None, 1, C_DV), lambda b, i: (layer, 0, 0)),
                  pl.BlockSpec((None,) + w.shape[1:], lambda b, i: (layer, 0, 0), pipeline_mode=pl.Buffered(1)),
                  pl.BlockSpec((1, tm, d), lambda b, i: (b, i, 0)),
                  pl.BlockSpec((1, 6, d), _mod_index(ctx))],
        out_specs=pl.BlockSpec((1, tm, d), lambda b, i: (b, i, 0)),
        scratch_shapes=[pltpu.VMEM((tm, w.shape[1]), BF16)],
        compiler_params=_cparams(("parallel", "parallel")),
        name="outproj",
    )(oa, ob, oc, cg, gn, w, x, mod)


def _ffn_kernel(x_ref, mod_ref, g_ref, wg_ref, wu_ref, wd_ref, o_ref, h_ref):
    j = pl.program_id(2)

    @pl.when(j == 0)
    def _():
        y = _rms(x_ref[0], g_ref[...])
        h_ref[...] = (y * (1.0 + mod_ref[0, 4:5, :]) + mod_ref[0, 3:4, :]).astype(BF16)
        o_ref[0] = jnp.zeros(o_ref.shape[1:], F32)

    h = h_ref[...]
    a = jnp.dot(h, wg_ref[...], preferred_element_type=F32)
    u = jnp.dot(h, wu_ref[...], preferred_element_type=F32)
    t = (_silu(a) * u).astype(BF16)
    o_ref[0] += jnp.dot(t, wd_ref[...], preferred_element_type=F32)

    @pl.when(j == pl.num_programs(2) - 1)
    def _():
        o_ref[0] = x_ref[0] + mod_ref[0, 5:6, :] * o_ref[0]


def _ffn(x, mod, g, wg, wu, wd, layer, ctx):
    bsz, n, d = x.shape
    f = wg.shape[2]
    tm = min(1024, n)
    tf = 512
    return pl.pallas_call(
        _ffn_kernel,
        out_shape=jax.ShapeDtypeStruct((bsz, n, d), F32),
        grid=(bsz, n // tm, f // tf),
        in_specs=[pl.BlockSpec((1, tm, d), lambda b, i, j: (b, i, 0), pipeline_mode=pl.Buffered(1)),
                  pl.BlockSpec((1, 6, d), _mod_index(ctx)),
                  pl.BlockSpec((None, 1, d), lambda b, i, j: (layer, 0, 0)),
                  pl.BlockSpec((None, d, tf), lambda b, i, j: (layer, 0, j)),
                  pl.BlockSpec((None, d, tf), lambda b, i, j: (layer, 0, j)),
                  pl.BlockSpec((None, tf, d), lambda b, i, j: (layer, j, 0))],
        out_specs=pl.BlockSpec((1, tm, d), lambda b, i, j: (b, i, 0)),
        scratch_shapes=[pltpu.VMEM((tm, d), BF16)],
        compiler_params=_cparams(("parallel", "parallel", "arbitrary")),
        name="ffn",
    )(x, mod, g, wg, wu, wd)


def _rope_tables(n):
    pos = np.arange(n)
    row = (pos // GRID_W).astype(np.float64)[:, None]
    col = (pos % GRID_W).astype(np.float64)[:, None]

    def table(half):
        inv = ROPE_THETA ** (-np.arange(half, dtype=np.float64) / half)
        ar, ac = row * inv[None, :], col * inv[None, :]
        cos = np.concatenate([np.cos(ar), np.cos(ar), np.cos(ac), np.cos(ac)], axis=1)
        sin = np.concatenate([-np.sin(ar), np.sin(ar), -np.sin(ac), np.sin(ac)], axis=1)
        return cos, sin

    cos_a, sin_a = table(A_HEAD_DIM // 4)
    cos_b, sin_b = table(B_ROPE // 4)
    cos_b, sin_b = np.tile(cos_b, (1, 2)), np.tile(sin_b, (1, 2))
    return tuple(jnp.asarray(t, F32) for t in (cos_a, sin_a, cos_b, sin_b))


def _prepare_weights(w_in, w_uq, w_ukv, w_gk_f, b_gk_f, w_gk_b, b_gk_b, w_out, w_gate, w_up, w_down, smalls):
    depth, d, _ = w_in.shape
    w_r = jnp.concatenate([
        w_in[:, :, 0:2304],
        w_in[:, :, 2304:2368], w_in[:, :, 2304:2368],
        w_in[:, :, 3904:3936], jnp.zeros((depth, d, 96), w_in.dtype),
        w_in[:, :, 2368:3904],
    ], axis=2).astype(BF16)
    uq = w_uq.reshape(depth, B_Q_RANK, B_HEADS, B_NOPE + B_ROPE)
    uq = jnp.concatenate([uq[..., :B_NOPE].reshape(depth, B_Q_RANK, -1),
                          uq[..., B_NOPE:].reshape(depth, B_Q_RANK, -1)], axis=2)
    ukv = w_ukv.reshape(depth, B_KV_RANK, B_HEADS, B_NOPE + B_V)
    ukv = jnp.concatenate([ukv[..., :B_NOPE].reshape(depth, B_KV_RANK, -1),
                           ukv[..., B_NOPE:].reshape(depth, B_KV_RANK, -1)], axis=2)
    nk = C_HEADS * C_DK
    w_gk = jnp.zeros((depth, 128, 2 * nk), F32)
    w_gk = w_gk.at[:, 0:C_GATE_RANK, 0:nk].set(w_gk_f).at[:, C_GATE_RANK:2 * C_GATE_RANK, nk:].set(w_gk_b)
    lw = {k: v.reshape(depth, 1, -1) for k, v in smalls.items()}
    lw["b_q_rope_norm2"] = jnp.tile(lw["b_q_rope_norm"], (1, 1, 2))
    lw["b_k_rope_norm2"] = jnp.tile(lw["b_k_rope_norm"], (1, 1, 2))
    lw.update(w_in=w_r, w_uq=uq.astype(BF16), w_ukv=ukv.astype(BF16), w_gk=w_gk,
              b_gk=jnp.concatenate([b_gk_f, b_gk_b], axis=1).reshape(depth, 1, -1),
              w_out=w_out.astype(BF16), w_gate=w_gate.astype(BF16),
              w_up=w_up.astype(BF16), w_down=w_down.astype(BF16))
    return lw


def kernel(x, c, ctx, c_ctx, w_mod, b_mod, norm1_g, norm2_g, w_in, a_q_norm, a_k_norm, b_q_lora_norm, b_kv_lora_norm, w_uq, w_ukv, b_q_nope_norm, b_k_nope_norm, b_q_rope_norm, b_k_rope_norm, w_gk_f, b_gk_f, w_gk_b, b_gk_b, c_out_norm, w_out, w_gate, w_up, w_down):
    bsz, n, d = x.shape
    depth = w_in.shape[0]
    tabs = _rope_tables(n)
    assert bsz + 1 <= 8, "the modulation kernel holds the context row and all batch rows in one 8-row tile"
    cpad = jnp.concatenate([c_ctx[None, :], c, jnp.zeros((8 - bsz - 1, d), F32)], axis=0)
    smalls = dict(a_q_norm=a_q_norm, a_k_norm=a_k_norm, b_q_lora_norm=b_q_lora_norm,
                  b_kv_lora_norm=b_kv_lora_norm, b_q_nope_norm=b_q_nope_norm, b_k_nope_norm=b_k_nope_norm,
                  b_q_rope_norm=b_q_rope_norm, b_k_rope_norm=b_k_rope_norm, c_out_norm=c_out_norm,
                  norm1_g=norm1_g, norm2_g=norm2_g)
    lw = _prepare_weights(w_in, w_uq, w_ukv, w_gk_f, b_gk_f, w_gk_b, b_gk_b, w_out, w_gate, w_up, w_down, smalls)
    b_mod3 = b_mod.reshape(depth, 1, -1)
    s_zero = jnp.zeros((2, bsz, 2, 2 * C_DV, 2 * C_DK), F32)
    a_args = dict(kv_heads=A_KV_HEADS, groups=A_HEADS // A_KV_HEADS, dq=A_HEAD_DIM, dv=A_HEAD_DIM)
    b_args = dict(kv_heads=B_HEADS, groups=1, dq=B_QK_PAD, dv=B_V)
    xc = ctx
    for l in range(depth):
        ctx_out = l < depth - 1
        mod = _modulation(cpad, w_mod, b_mod3, l)
        caq, cak, cav, cbq, cbk, cbv, cgd, ccq, cck, ccv, ccg = _proj(xc, mod, lw["norm1_g"], None, lw, l, True)
        laq, lak, lav, lbq, lbk, lbv, lgd, lcq, lck, lcv, lcg = _proj(x, mod, lw["norm1_g"], tabs, lw, l, False)

        o_a = _attention(laq, cak, cav, lak, lav, tq=256, **a_args)
        o_b = _attention(lbq, cbk, cbv, lbk, lbv, tq=512, **b_args)
        oc_c, s_ctx = _gla(ccq, cck, ccv, cgd, s_zero)
        o_c, _ = _gla(lcq, lck, lcv, lgd, s_ctx)

        x = _outproj(o_a, o_b, o_c, lcg, lw["c_out_norm"], lw["w_out"], x, mod, l, False)
        x = _ffn(x, mod, lw["norm2_g"], lw["w_gate"], lw["w_up"], lw["w_down"], l, False)
        if ctx_out:
            oc_a = _attention(caq, cak, cav, None, None, tq=256, **a_args)
            oc_b = _attention(cbq, cbk, cbv, None, None, tq=256, **b_args)
            xc = _outproj(oc_a, oc_b, oc_c, ccg, lw["c_out_norm"], lw["w_out"], xc, mod, l, True)
            xc = _ffn(xc, mod, lw["norm2_g"], lw["w_gate"], lw["w_up"], lw["w_down"], l, True)
    return x
```

```python
import functools

import numpy as np
import jax
import jax.numpy as jnp
from jax import lax
from jax.experimental import pallas as pl
from jax.experimental.pallas import tpu as pltpu

F32 = jnp.float32
BF16 = jnp.bfloat16

D_MODEL = 2048
GRID_W = 64
ROPE_THETA = 10000.0
NORM_EPS = 1e-6

A_HEADS = 8
A_KV_HEADS = 2
A_HEAD_DIM = 128
B_HEADS = 4
B_Q_RANK = 512
B_KV_RANK = 256
B_NOPE = 128
B_ROPE = 64
B_V = 128
B_QK_PAD = 256
C_HEADS = 4
C_DK = 64
C_DV = 128
C_GATE_RANK = 16
C_GATE_NORM = 16.0

D_FF = 5632
P_COLS = 4096

COL_AQ = 0
COL_AK = 1024
COL_BCQ = 1536
COL_BCKV = 2048
COL_CQ = 2560
COL_CV = 3072
COL_CG = 3584

GLA_CHUNK = 128
GLA_SAFE_DECAY = 60.0

VMEM_LIMIT = 56 * 2**20
LOG2E = 1.4426950408889634


def _cparams(sem, vmem=VMEM_LIMIT):
    return pltpu.CompilerParams(dimension_semantics=sem, vmem_limit_bytes=vmem)


def _silu(x):
    return x / (1.0 + jnp.exp(-x))


def _rms(x, g):
    ms = jnp.mean(x * x, axis=-1, keepdims=True)
    return x * lax.rsqrt(ms + NORM_EPS) * g


def _mod_kernel(c_ref, w_ref, b_ref, o_ref):
    s = _silu(c_ref[...]).astype(BF16)
    o_ref[...] = jnp.dot(s, w_ref[...].astype(BF16), preferred_element_type=F32) + b_ref[...]


def _modulation(cpad, w_mod, b_mod, layer):
    d = cpad.shape[1]
    n = w_mod.shape[2]
    tn = 1024
    out = pl.pallas_call(
        _mod_kernel,
        out_shape=jax.ShapeDtypeStruct((8, n), F32),
        grid=(n // tn,),
        in_specs=[pl.BlockSpec((8, d), lambda j: (0, 0)),
                  pl.BlockSpec((None, d, tn), lambda j: (layer, 0, j)),
                  pl.BlockSpec((None, 1, tn), lambda j: (layer, 0, j))],
        out_specs=pl.BlockSpec((8, tn), lambda j: (0, j)),
        compiler_params=_cparams(("arbitrary",)),
        name="modulation",
    )(cpad, w_mod, b_mod)
    return out.reshape(8, 6, d)


def _mod_index(ctx):
    if ctx:
        return lambda b, i, *_: (0, 0, 0)
    return lambda b, i, *_: (b + 1, 0, 0)


def _swap_halves(x, lane, width):
    return jnp.where((lane // width) % 2 == 0,
                     pltpu.roll(x, 128 - width, 1), pltpu.roll(x, width, 1))


def _proj_kernel(*refs, rope):
    x_ref, mod_ref, g1_ref, w_ref = refs[:4]
    refs = refs[4:]
    if rope:
        cosa_ref, sina_ref, cosb_ref, sinb_ref = refs[:4]
        refs = refs[4:]
    (gaq_ref, gak_ref, gql_ref, gkvl_ref, gqn_ref, gkn_ref, gqr_ref, gkr_ref,
     wuq_ref, wukv_ref, wgk_ref, bgk_ref,
     aq_ref, ak_ref, av_ref, bq_ref, bk_ref, bv_ref, gd_ref, cq_ref, ck_ref, cv_ref, cg_ref) = refs
    tm = x_ref.shape[1]
    lane = lax.broadcasted_iota(jnp.int32, (tm, 128), 1)
    lo = lane < 64

    y = _rms(x_ref[0], g1_ref[...])
    h = (y * (1.0 + mod_ref[0, 1:2, :]) + mod_ref[0, 0:1, :]).astype(BF16)

    def proj(c0, width):
        return jnp.dot(h, w_ref[:, c0:c0 + width], preferred_element_type=F32)

    def rope_a(x):
        if not rope:
            return x
        return x * cosa_ref[...] + _swap_halves(x, lane, 32) * sina_ref[...]

    def rope_b(x):
        if not rope:
            return x
        return x * cosb_ref[...] + _swap_halves(x, lane, 16) * sinb_ref[...]

    a_scale = LOG2E * A_HEAD_DIM ** -0.5
    for half in range(2):
        pq = proj(COL_AQ + half * 512, 512)
        for j in range(4):
            hh = half * 4 + j
            x = pq[:, j * 128:(j + 1) * 128]
            aq_ref[0, :, hh * 128:(hh + 1) * 128] = (rope_a(_rms(x, gaq_ref[...])) * a_scale).astype(BF16)
    pkv = proj(COL_AK, 512)
    for hh in range(A_KV_HEADS):
        x = pkv[:, hh * 128:(hh + 1) * 128]
        ak_ref[0, :, hh * 128:(hh + 1) * 128] = rope_a(_rms(x, gak_ref[...])).astype(BF16)
    av_ref[0] = pkv[:, 256:512].astype(BF16)

    b_scale = LOG2E * (B_NOPE + B_ROPE) ** -0.5
    cq = _rms(proj(COL_BCQ, 512), gql_ref[...]).astype(BF16)
    qb = jnp.dot(cq, wuq_ref[...], preferred_element_type=F32)
    for hh in range(B_HEADS):
        x = qb[:, hh * 128:(hh + 1) * 128]
        bq_ref[0, :, hh * B_QK_PAD:hh * B_QK_PAD + 128] = (_rms(x, gqn_ref[...]) * b_scale).astype(BF16)
    for p in range(B_HEADS // 2):
        r = qb[:, 512 + p * 128:512 + (p + 1) * 128]
        sq = r * r
        ms_lo = jnp.sum(jnp.where(lo, sq, 0.0), axis=-1, keepdims=True)
        ms_hi = jnp.sum(jnp.where(lo, 0.0, sq), axis=-1, keepdims=True)
        ms = jnp.where(lo, ms_lo, ms_hi) * (1.0 / B_ROPE)
        rn = rope_b(r * lax.rsqrt(ms + NORM_EPS) * gqr_ref[...]) * b_scale
        h0 = 2 * p
        bq_ref[0, :, h0 * B_QK_PAD + 128:(h0 + 1) * B_QK_PAD] = jnp.where(lo, rn, 0.0).astype(BF16)
        bq_ref[0, :, (h0 + 1) * B_QK_PAD + 128:(h0 + 2) * B_QK_PAD] = jnp.where(lo, 0.0, rn).astype(BF16)

    pmix = proj(COL_BCKV, 512)
    ckv = _rms(pmix[:, 0:256], gkvl_ref[...]).astype(BF16)
    kv = jnp.dot(ckv, wukv_ref[...], preferred_element_type=F32)
    krn = rope_b(_rms(pmix[:, 256:384], gkr_ref[...]))
    kr_even = jnp.where(lo, krn, 0.0).astype(BF16)
    kr_odd = jnp.where(lo, 0.0, krn).astype(BF16)
    for hh in range(B_HEADS):
        x = kv[:, hh * 128:(hh + 1) * 128]
        bk_ref[0, :, hh * B_QK_PAD:hh * B_QK_PAD + 128] = _rms(x, gkn_ref[...]).astype(BF16)
        bk_ref[0, :, hh * B_QK_PAD + 128:(hh + 1) * B_QK_PAD] = kr_even if hh % 2 == 0 else kr_odd
    bv_ref[0] = kv[:, 512:1024].astype(BF16)

    z = jnp.dot(pmix[:, 384:512], wgk_ref[...], preferred_element_type=F32,
                precision=lax.Precision.HIGHEST) + bgk_ref[...]
    gd_ref[0] = (jnp.minimum(z, 0.0) - jnp.log(1.0 + jnp.exp(-jnp.abs(z)))) * (1.0 / C_GATE_NORM)
    pqk = proj(COL_CQ, 512)
    cq_ref[0] = pqk[:, 0:256]
    ck_ref[0] = pqk[:, 256:512]
    cv_ref[0] = proj(COL_CV, 512).astype(BF16)
    cg_ref[0] = proj(COL_CG, 512)


def _proj(x, mod, g1, tabs, lw, layer, ctx):
    bsz, n, d = x.shape
    tm = min(512, n)
    rope = not ctx
    row = lambda b, i: (b, i, 0)
    once = pl.Buffered(1)
    in_specs = [pl.BlockSpec((1, tm, d), row),
                pl.BlockSpec((1, 6, d), _mod_index(ctx)),
                pl.BlockSpec((None, 1, d), lambda b, i: (layer, 0, 0)),
                pl.BlockSpec((None, d, P_COLS), lambda b, i: (layer, 0, 0), pipeline_mode=once)]
    args = [x, mod, g1, lw["w_in"]]
    if rope:
        in_specs += [pl.BlockSpec((tm, 128), lambda b, i: (i, 0))] * 4
        args += list(tabs)
    small = [lw["a_q_norm"], lw["a_k_norm"], lw["b_q_lora_norm"], lw["b_kv_lora_norm"],
             lw["b_q_nope_norm"], lw["b_k_nope_norm"], lw["b_q_rope_norm2"], lw["b_k_rope_norm2"],
             lw["w_uq"], lw["w_ukv"], lw["w_gk"], lw["b_gk"]]
    in_specs += [pl.BlockSpec((None,) + a.shape[1:], lambda b, i: (layer, 0, 0), pipeline_mode=once) for a in small]
    args += small
    out_shape = [jax.ShapeDtypeStruct((bsz, n, 1024), BF16),
                 jax.ShapeDtypeStruct((bsz, n, 256), BF16),
                 jax.ShapeDtypeStruct((bsz, n, 256), BF16),
                 jax.ShapeDtypeStruct((bsz, n, B_HEADS * B_QK_PAD), BF16),
                 jax.ShapeDtypeStruct((bsz, n, B_HEADS * B_QK_PAD), BF16),
                 jax.ShapeDtypeStruct((bsz, n, B_HEADS * B_V), BF16),
                 jax.ShapeDtypeStruct((bsz, n, 512), F32),
                 jax.ShapeDtypeStruct((bsz, n, 256), F32),
                 jax.ShapeDtypeStruct((bsz, n, 256), F32),
                 jax.ShapeDtypeStruct((bsz, n, 512), BF16),
                 jax.ShapeDtypeStruct((bsz, n, 512), F32)]
    out_specs = [pl.BlockSpec((1, tm, s.shape[2]), row) for s in out_shape]
    return pl.pallas_call(
        functools.partial(_proj_kernel, rope=rope),
        out_shape=out_shape,
        grid=(bsz, n // tm),
        in_specs=in_specs,
        out_specs=out_specs,
        compiler_params=_cparams(("parallel", "parallel")),
        name="proj_ctx" if ctx else "proj_lat",
    )(*args)


def _attn_kernel(*refs, groups, dq, dv, tk, rsub, has_lat):
    if has_lat:
        q_ref, kc_ref, vc_ref, kl_ref, vl_ref, o_ref, q_scr, s_scr, p_scr, m_scr, l_scr, acc_scr = refs
    else:
        q_ref, kc_ref, vc_ref, o_ref, q_scr, s_scr, p_scr, m_scr, l_scr, acc_scr = refs
    tq = q_ref.shape[1]
    m_rows = groups * tq
    nt = (((1,), (1,)), ((), ()))
    for g in range(groups):
        q_scr[g * tq:(g + 1) * tq, :] = q_ref[0, :, g * dq:(g + 1) * dq]
    m_scr[...] = jnp.full(m_scr.shape, -jnp.inf, F32)
    l_scr[...] = jnp.zeros(l_scr.shape, F32)
    acc_scr[...] = jnp.zeros(acc_scr.shape, F32)

    def scores(slot, k):
        s_scr[slot, :, 0:k.shape[0]] = lax.dot_general(q_scr[...], k, nt, preferred_element_type=F32)

    def softmax_pv(slot, v):
        w = v.shape[0]
        for r in range(m_rows // rsub):
            rs = slice(r * rsub, (r + 1) * rsub)
            cols = [s_scr[slot, rs, j * 128:(j + 1) * 128] for j in range(w // 128)]
            mx = functools.reduce(jnp.maximum, cols)
            m_old = m_scr[rs, :]
            m_new = jnp.maximum(m_old, jnp.max(mx, axis=-1, keepdims=True))
            ps = [jnp.exp2(c - m_new) for c in cols]
            a = jnp.exp2(m_old - m_new)
            l_scr[rs, :] = a * l_scr[rs, :] + functools.reduce(jnp.add, ps)
            m_scr[rs, :] = m_new
            acc_scr[rs, :] = a * acc_scr[rs, :]
            for j, p in enumerate(ps):
                p_scr[rs, j * 128:(j + 1) * 128] = p.astype(BF16)
        acc_scr[...] += jnp.dot(p_scr[:, 0:w], v, preferred_element_type=F32)

    scores(0, kc_ref[0])
    if has_lat:
        nlc = kl_ref.shape[1] // tk

        def lat(ref, c):
            return ref[0, pl.ds(pl.multiple_of(c * tk, tk), tk), :]

        scores(1, lat(kl_ref, 0))
        softmax_pv(0, vc_ref[0])

        def body(i, carry):
            scores(0, lat(kl_ref, 2 * i + 1))
            softmax_pv(1, lat(vl_ref, 2 * i))
            scores(1, lat(kl_ref, 2 * i + 2))
            softmax_pv(0, lat(vl_ref, 2 * i + 1))
            return carry

        lax.fori_loop(0, nlc // 2 - 1, body, 0)
        scores(0, lat(kl_ref, nlc - 1))
        softmax_pv(1, lat(vl_ref, nlc - 2))
        softmax_pv(0, lat(vl_ref, nlc - 1))
    else:
        softmax_pv(0, vc_ref[0])
    for g in range(groups):
        gs = slice(g * tq, (g + 1) * tq)
        l = jnp.sum(l_scr[gs, :], axis=-1, keepdims=True)
        o_ref[0, :, g * dv:(g + 1) * dv] = (acc_scr[gs, :] * (1.0 / l)).astype(o_ref.dtype)


def _attention(q, kc, vc, kl, vl, *, kv_heads, groups, dq, dv, tq, tk=1024, rsub=64):
    assert dv == 128, "the accumulator rescale reuses the 128-lane replicated running max"
    bsz, n, _ = q.shape
    nc = kc.shape[1]
    has_lat = kl is not None
    tq = min(tq, n)
    m_rows = groups * tq
    wmax = max(nc, tk) if has_lat else nc
    in_specs = [pl.BlockSpec((1, tq, groups * dq), lambda b, h, i: (b, i, h)),
                pl.BlockSpec((1, nc, dq), lambda b, h, i: (b, 0, h)),
                pl.BlockSpec((1, nc, dv), lambda b, h, i: (b, 0, h))]
    args = [q, kc, vc]
    if has_lat:
        nl = kl.shape[1]
        in_specs += [pl.BlockSpec((1, nl, dq), lambda b, h, i: (b, 0, h)),
                     pl.BlockSpec((1, nl, dv), lambda b, h, i: (b, 0, h))]
        args += [kl, vl]
    return pl.pallas_call(
        functools.partial(_attn_kernel, groups=groups, dq=dq, dv=dv, tk=tk, rsub=rsub, has_lat=has_lat),
        out_shape=jax.ShapeDtypeStruct((bsz, n, kv_heads * groups * dv), BF16),
        grid=(bsz, kv_heads, n // tq),
        in_specs=in_specs,
        out_specs=pl.BlockSpec((1, tq, groups * dv), lambda b, h, i: (b, i, h)),
        scratch_shapes=[pltpu.VMEM((m_rows, dq), BF16), pltpu.VMEM((2, m_rows, wmax), F32),
                        pltpu.VMEM((m_rows, wmax), BF16), pltpu.VMEM((m_rows, 128), F32),
                        pltpu.VMEM((m_rows, 128), F32), pltpu.VMEM((m_rows, dv), F32)],
        compiler_params=_cparams(("parallel", "parallel", "arbitrary")),
        name="attention",
    )(*args)


def _gla_kernel(q_ref, k_ref, v_ref, g_ref, s0_ref, o_ref, sfin_ref, st_ref, att_ref, cb_ref, kk_ref, *, chunk):
    d = pl.program_id(0)
    i = pl.program_id(2)
    t_rows = q_ref.shape[1]
    nch = t_rows // chunk

    @pl.when(i == 0)
    def _():
        st_ref[...] = s0_ref[0, 0]

    row = lax.broadcasted_iota(jnp.int32, (chunk, chunk), 0)
    col = lax.broadcasted_iota(jnp.int32, (chunk, chunk), 1)
    tri = jnp.where(d == 0, col - row, row - col) <= 0
    tri_bf = jnp.where(tri, 1.0, 0.0).astype(BF16)
    row2 = lax.broadcasted_iota(jnp.int32, (chunk, 2 * chunk), 0)
    lane2 = lax.broadcasted_iota(jnp.int32, (chunk, 2 * chunk), 1)
    col2 = jnp.where(lane2 >= chunk, lane2 - chunk, lane2)
    tri2 = jnp.where(d == 0, col2 - row2, row2 - col2) <= 0
    lo = lax.broadcasted_iota(jnp.int32, (chunk, 128), 1) < C_DK
    vlo = lax.broadcasted_iota(jnp.int32, (chunk, 2 * C_DV), 1) < C_DV
    srow = lax.broadcasted_iota(jnp.int32, (2 * C_DV, 2 * C_DK), 0) // C_DV
    scol = lax.broadcasted_iota(jnp.int32, (2 * C_DV, 2 * C_DK), 1) // C_DK
    same_head = srow == scol
    nt = (((1,), (1,)), ((), ()))
    tn = (((0,), (0,)), ((), ()))

    def intra_scores_mxu(q_p, k_p, cb_p, qe_p):
        ke_p = (k_p * jnp.exp(-cb_p)).astype(BF16)
        zk = jnp.zeros_like(ke_p)
        ke_bd = jnp.concatenate([jnp.where(lo, ke_p, zk), jnp.where(lo, zk, ke_p)], axis=0)
        return lax.dot_general(qe_p, ke_bd, nt, preferred_element_type=F32)

    def intra_scores_guarded(q_p, k_p, cb_p, qe_p):
        cb_ref[...] = cb_p
        kk_ref[...] = k_p
        att_ref[...] = jnp.zeros(att_ref.shape, F32)

        def one_source(s, carry):
            cb_s = cb_ref[pl.ds(s, 1), :]
            w = q_p * jnp.exp(jnp.minimum(cb_p - cb_s, 0.0)) * kk_ref[pl.ds(s, 1), :]
            w0 = jnp.sum(jnp.where(lo, w, 0.0), axis=-1, keepdims=True)
            w1 = jnp.sum(jnp.where(lo, 0.0, w), axis=-1, keepdims=True)
            att_ref[...] += jnp.where(lane2 == s, w0, 0.0) + jnp.where(lane2 == s + chunk, w1, 0.0)
            return carry

        lax.fori_loop(0, chunk, one_source, 0)
        return att_ref[...]

    def scan_block(intra_scores):
        for c in range(nch):
            cc = c + d * (nch - 1 - 2 * c)
            r0 = pl.multiple_of(cc * chunk, chunk)
            g = g_ref[0, pl.ds(r0, chunk), :]
            g_hi = g.astype(BF16)
            g_lo = (g - g_hi.astype(F32)).astype(BF16)
            cb = (jnp.dot(tri_bf, g_hi, preferred_element_type=F32)
                  + jnp.dot(tri_bf, g_lo, preferred_element_type=F32))
            tot = jnp.sum(g, axis=0, keepdims=True)
            q = q_ref[0, pl.ds(r0, chunk), :] * (C_DK ** -0.5)
            k = k_ref[0, pl.ds(r0, chunk), :]
            v = v_ref[0, pl.ds(r0, chunk), :]
            qe = (q * jnp.exp(cb)).astype(BF16)
            kd = (k * jnp.exp(tot - cb)).astype(BF16)
            dec = jnp.exp(tot)
            for p in range(C_HEADS // 2):
                ls = slice(p * 128, (p + 1) * 128)
                vs = slice(p * 2 * C_DV, (p + 1) * 2 * C_DV)
                qe_p, kd_p, v_p = qe[:, ls], kd[:, ls], v[:, vs]
                att = intra_scores(q[:, ls], k[:, ls], cb[:, ls], qe_p)
                att = jnp.where(tri2, att, 0.0).astype(BF16)
                zv = jnp.zeros_like(v_p)
                v_bd = jnp.concatenate([jnp.where(vlo, v_p, zv), jnp.where(vlo, zv, v_p)], axis=0)
                st = st_ref[p]
                o = (jnp.dot(att, v_bd, preferred_element_type=F32)
                     + lax.dot_general(qe_p, st.astype(BF16), nt, preferred_element_type=F32))
                o_ref[0, 0, pl.ds(r0, chunk), vs] = o
                u = lax.dot_general(v_p, kd_p, tn, preferred_element_type=F32)
                st_ref[p] = dec[:, ls] * st + jnp.where(same_head, u, 0.0)

    worst = functools.reduce(jnp.maximum, [
        jnp.max(-jnp.sum(g_ref[0, c * chunk:(c + 1) * chunk, :], axis=0, keepdims=True)) for c in range(nch)])
    safe = worst <= GLA_SAFE_DECAY

    @pl.when(safe)
    def _():
        scan_block(intra_scores_mxu)

    @pl.when(jnp.logical_not(safe))
    def _():
        scan_block(intra_scores_guarded)

    @pl.when(i == pl.num_programs(2) - 1)
    def _():
        sfin_ref[0, 0] = st_ref[...]


def _gla(cq, ck, cv, gdec, s0):
    bsz, n, _ = cq.shape
    t_rows = min(512, n)
    nb = n // t_rows
    rb = lambda d, i: i + d * (nb - 1 - 2 * i)
    o, sfin = pl.pallas_call(
        functools.partial(_gla_kernel, chunk=GLA_CHUNK),
        out_shape=[jax.ShapeDtypeStruct((2, bsz, n, C_HEADS * C_DV), F32),
                   jax.ShapeDtypeStruct(s0.shape, F32)],
        grid=(2, bsz, nb),
        in_specs=[pl.BlockSpec((1, t_rows, 256), lambda d, b, i: (b, rb(d, i), 0)),
                  pl.BlockSpec((1, t_rows, 256), lambda d, b, i: (b, rb(d, i), 0)),
                  pl.BlockSpec((1, t_rows, 512), lambda d, b, i: (b, rb(d, i), 0)),
                  pl.BlockSpec((1, t_rows, 256), lambda d, b, i: (b, rb(d, i), d)),
                  pl.BlockSpec((1, 1, 2, 256, 128), lambda d, b, i: (d, b, 0, 0, 0))],
        out_specs=[pl.BlockSpec((1, 1, t_rows, 512), lambda d, b, i: (d, b, rb(d, i), 0)),
                   pl.BlockSpec((1, 1, 2, 256, 128), lambda d, b, i: (d, b, 0, 0, 0))],
        scratch_shapes=[pltpu.VMEM((2, 256, 128), F32), pltpu.VMEM((GLA_CHUNK, 2 * GLA_CHUNK), F32),
                        pltpu.VMEM((GLA_CHUNK, 128), F32), pltpu.VMEM((GLA_CHUNK, 128), F32)],
        compiler_params=_cparams(("parallel", "parallel", "arbitrary")),
        name="gla",
    )(cq, ck, cv, gdec, s0)
    return o, sfin


def _outproj_kernel(oa_ref, ob_ref, oc_ref, cg_ref, gn_ref, w_ref, x_ref, mod_ref, o_ref, z_ref):
    z_ref[:, 0:1024] = oa_ref[0]
    z_ref[:, 1024:1536] = ob_ref[0]
    for h in range(C_HEADS):
        hs = slice(h * C_DV, (h + 1) * C_DV)
        oc = oc_ref[0, 0, :, hs] + oc_ref[1, 0, :, hs]
        z_ref[:, 1536 + h * C_DV:1536 + (h + 1) * C_DV] = (
            _rms(oc, gn_ref[...]) * _silu(cg_ref[0, :, hs])).astype(BF16)
    y = jnp.dot(z_ref[...], w_ref[...], preferred_element_type=F32)
    o_ref[0] = x_ref[0] + mod_ref[0, 2:3, :] * y


def _outproj(oa, ob, oc, cg, gn, w, x, mod, layer, ctx):
    bsz, n, d = x.shape
    tm = min(512, n)
    return pl.pallas_call(
        _outproj_kernel,
        out_shape=jax.ShapeDtypeStruct((bsz, n, d), F32),
        grid=(bsz, n // tm),
        in_specs=[pl.BlockSpec((1, tm, 1024), lambda b, i: (b, i, 0)),
                  pl.BlockSpec((1, tm, 512), lambda b, i: (b, i, 0)),
                  pl.BlockSpec((2, 1, tm, 512), lambda b, i: (0, b, i, 0)),
                  pl.BlockSpec((1, tm, 512), lambda b, i: (b, i, 0)),
                  pl.BlockSpec((None, 1, C_DV), lambda b, i: (layer, 0, 0)),
                  pl.BlockSpec((None,) + w.shape[1:], lambda b, i: (layer, 0, 0), pipeline_mode=pl.Buffered(1)),
                  pl.BlockSpec((1, tm, d), lambda b, i: (b, i, 0)),
                  pl.BlockSpec((1, 6, d), _mod_index(ctx))],
        out_specs=pl.BlockSpec((1, tm, d), lambda b, i: (b, i, 0)),
        scratch_shapes=[pltpu.VMEM((tm, w.shape[1]), BF16)],
        compiler_params=_cparams(("parallel", "parallel")),
        name="outproj",
    )(oa, ob, oc, cg, gn, w, x, mod)


def _ffn_kernel(x_ref, mod_ref, g_ref, wg_ref, wu_ref, wd_ref, o_ref, h_ref):
    j = pl.program_id(2)

    @pl.when(j == 0)
    def _():
        y = _rms(x_ref[0], g_ref[...])
        h_ref[...] = (y * (1.0 + mod_ref[0, 4:5, :]) + mod_ref[0, 3:4, :]).astype(BF16)
        o_ref[0] = jnp.zeros(o_ref.shape[1:], F32)

    h = h_ref[...]
    a = jnp.dot(h, wg_ref[...], preferred_element_type=F32)
    u = jnp.dot(h, wu_ref[...], preferred_element_type=F32)
    t = (_silu(a) * u).astype(BF16)
    o_ref[0] += jnp.dot(t, wd_ref[...], preferred_element_type=F32)

    @pl.when(j == pl.num_programs(2) - 1)
    def _():
        o_ref[0] = x_ref[0] + mod_ref[0, 5:6, :] * o_ref[0]


def _ffn(x, mod, g, wg, wu, wd, layer, ctx):
    bsz, n, d = x.shape
    f = wg.shape[2]
    tm = min(512, n)
    tf = 512
    return pl.pallas_call(
        _ffn_kernel,
        out_shape=jax.ShapeDtypeStruct((bsz, n, d), F32),
        grid=(bsz, n // tm, f // tf),
        in_specs=[pl.BlockSpec((1, tm, d), lambda b, i, j: (b, i, 0)),
                  pl.BlockSpec((1, 6, d), _mod_index(ctx)),
                  pl.BlockSpec((None, 1, d), lambda b, i, j: (layer, 0, 0)),
                  pl.BlockSpec((None, d, tf), lambda b, i, j: (layer, 0, j)),
                  pl.BlockSpec((None, d, tf), lambda b, i, j: (layer, 0, j)),
                  pl.BlockSpec((None, tf, d), lambda b, i, j: (layer, j, 0))],
        out_specs=pl.BlockSpec((1, tm, d), lambda b, i, j: (b, i, 0)),
        scratch_shapes=[pltpu.VMEM((tm, d), BF16)],
        compiler_params=_cparams(("parallel", "parallel", "arbitrary")),
        name="ffn",
    )(x, mod, g, wg, wu, wd)


def _rope_tables(n):
    pos = np.arange(n)
    row = (pos // GRID_W).astype(np.float64)[:, None]
    col = (pos % GRID_W).astype(np.float64)[:, None]

    def table(half):
        inv = ROPE_THETA ** (-np.arange(half, dtype=np.float64) / half)
        ar, ac = row * inv[None, :], col * inv[None, :]
        cos = np.concatenate([np.cos(ar), np.cos(ar), np.cos(ac), np.cos(ac)], axis=1)
        sin = np.concatenate([-np.sin(ar), np.sin(ar), -np.sin(ac), np.sin(ac)], axis=1)
        return cos, sin

    cos_a, sin_a = table(A_HEAD_DIM // 4)
    cos_b, sin_b = table(B_ROPE // 4)
    cos_b, sin_b = np.tile(cos_b, (1, 2)), np.tile(sin_b, (1, 2))
    return tuple(jnp.asarray(t, F32) for t in (cos_a, sin_a, cos_b, sin_b))


def _prepare_weights(w_in, w_uq, w_ukv, w_gk_f, b_gk_f, w_gk_b, b_gk_b, w_out, w_gate, w_up, w_down, smalls):
    depth, d, _ = w_in.shape
    w_r = jnp.concatenate([
        w_in[:, :, 0:2304],
        w_in[:, :, 2304:2368], w_in[:, :, 2304:2368],
        w_in[:, :, 3904:3936], jnp.zeros((depth, d, 96), w_in.dtype),
        w_in[:, :, 2368:3904],
    ], axis=2).astype(BF16)
    uq = w_uq.reshape(depth, B_Q_RANK, B_HEADS, B_NOPE + B_ROPE)
    uq = jnp.concatenate([uq[..., :B_NOPE].reshape(depth, B_Q_RANK, -1),
                          uq[..., B_NOPE:].reshape(depth, B_Q_RANK, -1)], axis=2)
    ukv = w_ukv.reshape(depth, B_KV_RANK, B_HEADS, B_NOPE + B_V)
    ukv = jnp.concatenate([ukv[..., :B_NOPE].reshape(depth, B_KV_RANK, -1),
                           ukv[..., B_NOPE:].reshape(depth, B_KV_RANK, -1)], axis=2)
    nk = C_HEADS * C_DK
    w_gk = jnp.zeros((depth, 128, 2 * nk), F32)
    w_gk = w_gk.at[:, 0:C_GATE_RANK, 0:nk].set(w_gk_f).at[:, C_GATE_RANK:2 * C_GATE_RANK, nk:].set(w_gk_b)
    lw = {k: v.reshape(depth, 1, -1) for k, v in smalls.items()}
    lw["b_q_rope_norm2"] = jnp.tile(lw["b_q_rope_norm"], (1, 1, 2))
    lw["b_k_rope_norm2"] = jnp.tile(lw["b_k_rope_norm"], (1, 1, 2))
    lw.update(w_in=w_r, w_uq=uq.astype(BF16), w_ukv=ukv.astype(BF16), w_gk=w_gk,
              b_gk=jnp.concatenate([b_gk_f, b_gk_b], axis=1).reshape(depth, 1, -1),
              w_out=w_out.astype(BF16), w_gate=w_gate.astype(BF16),
              w_up=w_up.astype(BF16), w_down=w_down.astype(BF16))
    return lw


def kernel(x, c, ctx, c_ctx, w_mod, b_mod, norm1_g, norm2_g, w_in, a_q_norm, a_k_norm, b_q_lora_norm, b_kv_lora_norm, w_uq, w_ukv, b_q_nope_norm, b_k_nope_norm, b_q_rope_norm, b_k_rope_norm, w_gk_f, b_gk_f, w_gk_b, b_gk_b, c_out_norm, w_out, w_gate, w_up, w_down):
    bsz, n, d = x.shape
    depth = w_in.shape[0]
    tabs = _rope_tables(n)
    assert bsz + 1 <= 8, "the modulation kernel holds the context row and all batch rows in one 8-row tile"
    cpad = jnp.concatenate([c_ctx[None, :], c, jnp.zeros((8 - bsz - 1, d), F32)], axis=0)
    smalls = dict(a_q_norm=a_q_norm, a_k_norm=a_k_norm, b_q_lora_norm=b_q_lora_norm,
                  b_kv_lora_norm=b_kv_lora_norm, b_q_nope_norm=b_q_nope_norm, b_k_nope_norm=b_k_nope_norm,
                  b_q_rope_norm=b_q_rope_norm, b_k_rope_norm=b_k_rope_norm, c_out_norm=c_out_norm,
                  norm1_g=norm1_g, norm2_g=norm2_g)
    lw = _prepare_weights(w_in, w_uq, w_ukv, w_gk_f, b_gk_f, w_gk_b, b_gk_b, w_out, w_gate, w_up, w_down, smalls)
    b_mod3 = b_mod.reshape(depth, 1, -1)
    s_zero = jnp.zeros((2, bsz, 2, 2 * C_DV, 2 * C_DK), F32)
    a_args = dict(kv_heads=A_KV_HEADS, groups=A_HEADS // A_KV_HEADS, dq=A_HEAD_DIM, dv=A_HEAD_DIM)
    b_args = dict(kv_heads=B_HEADS, groups=1, dq=B_QK_PAD, dv=B_V)
    xc = ctx
    for l in range(depth):
        ctx_out = l < depth - 1
        mod = _modulation(cpad, w_mod, b_mod3, l)
        caq, cak, cav, cbq, cbk, cbv, cgd, ccq, cck, ccv, ccg = _proj(xc, mod, lw["norm1_g"], None, lw, l, True)
        laq, lak, lav, lbq, lbk, lbv, lgd, lcq, lck, lcv, lcg = _proj(x, mod, lw["norm1_g"], tabs, lw, l, False)

        o_a = _attention(laq, cak, cav, lak, lav, tq=512, **a_args)
        o_b = _attention(lbq, cbk, cbv, lbk, lbv, tq=1024, **b_args)
        oc_c, s_ctx = _gla(ccq, cck, ccv, cgd, s_zero)
        o_c, _ = _gla(lcq, lck, lcv, lgd, s_ctx)

        x = _outproj(o_a, o_b, o_c, lcg, lw["c_out_norm"], lw["w_out"], x, mod, l, False)
        x = _ffn(x, mod, lw["norm2_g"], lw["w_gate"], lw["w_up"], lw["w_down"], l, False)
        if ctx_out:
            oc_a = _attention(caq, cak, cav, None, None, tq=256, **a_args)
            oc_b = _attention(cbq, cbk, cbv, None, None, tq=256, **b_args)
            xc = _outproj(oc_a, oc_b, oc_c, ccg, lw["c_out_norm"], lw["w_out"], xc, mod, l, True)
            xc = _ffn(xc, mod, lw["norm2_g"], lw["w_gate"], lw["w_up"], lw["w_down"], l, True)
    return x
```

```python
import functools

import numpy as np
import jax
import jax.numpy as jnp
from jax import lax
from jax.experimental import pallas as pl
from jax.experimental.pallas import tpu as pltpu

F32 = jnp.float32
BF16 = jnp.bfloat16

D_MODEL = 2048
GRID_W = 64
ROPE_THETA = 10000.0
NORM_EPS = 1e-6

A_HEADS = 8
A_KV_HEADS = 2
A_HEAD_DIM = 128
B_HEADS = 4
B_Q_RANK = 512
B_KV_RANK = 256
B_NOPE = 128
B_ROPE = 64
B_V = 128
B_QK_PAD = 256
C_HEADS = 4
C_DK = 64
C_DV = 128
C_GATE_RANK = 16
C_GATE_NORM = 16.0

D_FF = 5632
P_COLS = 4096

COL_AQ = 0
COL_AK = 1024
COL_BCQ = 1536
COL_BCKV = 2048
COL_CQ = 2560
COL_CV = 3072
COL_CG = 3584

GLA_CHUNK = 128
GLA_SAFE_DECAY = 60.0

VMEM_LIMIT = 56 * 2**20
LOG2E = 1.4426950408889634


def _cparams(sem, vmem=VMEM_LIMIT):
    return pltpu.CompilerParams(dimension_semantics=sem, vmem_limit_bytes=vmem)


def _silu(x):
    return x / (1.0 + jnp.exp(-x))


def _rms(x, g):
    ms = jnp.mean(x * x, axis=-1, keepdims=True)
    return x * lax.rsqrt(ms + NORM_EPS) * g


def _mod_kernel(c_ref, w_ref, b_ref, o_ref):
    s = _silu(c_ref[...]).astype(BF16)
    o_ref[...] = jnp.dot(s, w_ref[...].astype(BF16), preferred_element_type=F32) + b_ref[...]


def _modulation(cpad, w_mod, b_mod, layer):
    d = cpad.shape[1]
    n = w_mod.shape[2]
    tn = 1024
    out = pl.pallas_call(
        _mod_kernel,
        out_shape=jax.ShapeDtypeStruct((8, n), F32),
        grid=(n // tn,),
        in_specs=[pl.BlockSpec((8, d), lambda j: (0, 0)),
                  pl.BlockSpec((None, d, tn), lambda j: (layer, 0, j)),
                  pl.BlockSpec((None, 1, tn), lambda j: (layer, 0, j))],
        out_specs=pl.BlockSpec((8, tn), lambda j: (0, j)),
        compiler_params=_cparams(("arbitrary",)),
        name="modulation",
    )(cpad, w_mod, b_mod)
    return out.reshape(8, 6, d)


def _mod_index(ctx):
    if ctx:
        return lambda b, i, *_: (0, 0, 0)
    return lambda b, i, *_: (b + 1, 0, 0)


def _swap_halves(x, lane, width):
    return jnp.where((lane // width) % 2 == 0,
                     pltpu.roll(x, 128 - width, 1), pltpu.roll(x, width, 1))


def _proj_kernel(*refs, rope):
    x_ref, mod_ref, g1_ref, w_ref = refs[:4]
    refs = refs[4:]
    if rope:
        cosa_ref, sina_ref, cosb_ref, sinb_ref = refs[:4]
        refs = refs[4:]
    (gaq_ref, gak_ref, gql_ref, gkvl_ref, gqn_ref, gkn_ref, gqr_ref, gkr_ref,
     wuq_ref, wukv_ref, wgk_ref, bgk_ref,
     aq_ref, ak_ref, av_ref, bq_ref, bk_ref, bv_ref, gd_ref, cq_ref, ck_ref, cv_ref, cg_ref) = refs
    tm = x_ref.shape[1]
    lane = lax.broadcasted_iota(jnp.int32, (tm, 128), 1)
    lo = lane < 64

    y = _rms(x_ref[0], g1_ref[...])
    h = (y * (1.0 + mod_ref[0, 1:2, :]) + mod_ref[0, 0:1, :]).astype(BF16)

    def proj(c0, width):
        return jnp.dot(h, w_ref[:, c0:c0 + width], preferred_element_type=F32)

    def rope_a(x):
        if not rope:
            return x
        return x * cosa_ref[...] + _swap_halves(x, lane, 32) * sina_ref[...]

    def rope_b(x):
        if not rope:
            return x
        return x * cosb_ref[...] + _swap_halves(x, lane, 16) * sinb_ref[...]

    a_scale = LOG2E * A_HEAD_DIM ** -0.5
    for half in range(2):
        pq = proj(COL_AQ + half * 512, 512)
        for j in range(4):
            hh = half * 4 + j
            x = pq[:, j * 128:(j + 1) * 128]
            aq_ref[0, :, hh * 128:(hh + 1) * 128] = (rope_a(_rms(x, gaq_ref[...])) * a_scale).astype(BF16)
    pkv = proj(COL_AK, 512)
    for hh in range(A_KV_HEADS):
        x = pkv[:, hh * 128:(hh + 1) * 128]
        ak_ref[0, :, hh * 128:(hh + 1) * 128] = rope_a(_rms(x, gak_ref[...])).astype(BF16)
    av_ref[0] = pkv[:, 256:512].astype(BF16)

    b_scale = LOG2E * (B_NOPE + B_ROPE) ** -0.5
    cq = _rms(proj(COL_BCQ, 512), gql_ref[...]).astype(BF16)
    qb = jnp.dot(cq, wuq_ref[...], preferred_element_type=F32)
    for hh in range(B_HEADS):
        x = qb[:, hh * 128:(hh + 1) * 128]
        bq_ref[0, :, hh * B_QK_PAD:hh * B_QK_PAD + 128] = (_rms(x, gqn_ref[...]) * b_scale).astype(BF16)
    for p in range(B_HEADS // 2):
        r = qb[:, 512 + p * 128:512 + (p + 1) * 128]
        sq = r * r
        ms_lo = jnp.sum(jnp.where(lo, sq, 0.0), axis=-1, keepdims=True)
        ms_hi = jnp.sum(jnp.where(lo, 0.0, sq), axis=-1, keepdims=True)
        ms = jnp.where(lo, ms_lo, ms_hi) * (1.0 / B_ROPE)
        rn = rope_b(r * lax.rsqrt(ms + NORM_EPS) * gqr_ref[...]) * b_scale
        h0 = 2 * p
        bq_ref[0, :, h0 * B_QK_PAD + 128:(h0 + 1) * B_QK_PAD] = jnp.where(lo, rn, 0.0).astype(BF16)
        bq_ref[0, :, (h0 + 1) * B_QK_PAD + 128:(h0 + 2) * B_QK_PAD] = jnp.where(lo, 0.0, rn).astype(BF16)

    pmix = proj(COL_BCKV, 512)
    ckv = _rms(pmix[:, 0:256], gkvl_ref[...]).astype(BF16)
    kv = jnp.dot(ckv, wukv_ref[...], preferred_element_type=F32)
    krn = rope_b(_rms(pmix[:, 256:384], gkr_ref[...]))
    kr_even = jnp.where(lo, krn, 0.0).astype(BF16)
    kr_odd = jnp.where(lo, 0.0, krn).astype(BF16)
    for hh in range(B_HEADS):
        x = kv[:, hh * 128:(hh + 1) * 128]
        bk_ref[0, :, hh * B_QK_PAD:hh * B_QK_PAD + 128] = _rms(x, gkn_ref[...]).astype(BF16)
        bk_ref[0, :, hh * B_QK_PAD + 128:(hh + 1) * B_QK_PAD] = kr_even if hh % 2 == 0 else kr_odd
    bv_ref[0] = kv[:, 512:1024].astype(BF16)

    z = jnp.dot(pmix[:, 384:512], wgk_ref[...], preferred_element_type=F32,
                precision=lax.Precision.HIGHEST) + bgk_ref[...]
    gd_ref[0] = (jnp.minimum(z, 0.0) - jnp.log(1.0 + jnp.exp(-jnp.abs(z)))) * (1.0 / C_GATE_NORM)
    pqk = proj(COL_CQ, 512)
    cq_ref[0] = pqk[:, 0:256]
    ck_ref[0] = pqk[:, 256:512]
    cv_ref[0] = proj(COL_CV, 512).astype(BF16)
    cg_ref[0] = proj(COL_CG, 512)


def _proj(x, mod, g1, tabs, lw, layer, ctx):
    bsz, n, d = x.shape
    tm = min(512, n)
    rope = not ctx
    row = lambda b, i: (b, i, 0)
    once = pl.Buffered(1)
    in_specs = [pl.BlockSpec((1, tm, d), row),
                pl.BlockSpec((1, 6, d), _mod_index(ctx)),
                pl.BlockSpec((None, 1, d), lambda b, i: (layer, 0, 0)),
                pl.BlockSpec((None, d, P_COLS), lambda b, i: (layer, 0, 0), pipeline_mode=once)]
    args = [x, mod, g1, lw["w_in"]]
    if rope:
        in_specs += [pl.BlockSpec((tm, 128), lambda b, i: (i, 0))] * 4
        args += list(tabs)
    small = [lw["a_q_norm"], lw["a_k_norm"], lw["b_q_lora_norm"], lw["b_kv_lora_norm"],
             lw["b_q_nope_norm"], lw["b_k_nope_norm"], lw["b_q_rope_norm2"], lw["b_k_rope_norm2"],
             lw["w_uq"], lw["w_ukv"], lw["w_gk"], lw["b_gk"]]
    in_specs += [pl.BlockSpec((None,) + a.shape[1:], lambda b, i: (layer, 0, 0), pipeline_mode=once) for a in small]
    args += small
    out_shape = [jax.ShapeDtypeStruct((bsz, n, 1024), BF16),
                 jax.ShapeDtypeStruct((bsz, n, 256), BF16),
                 jax.ShapeDtypeStruct((bsz, n, 256), BF16),
                 jax.ShapeDtypeStruct((bsz, n, B_HEADS * B_QK_PAD), BF16),
                 jax.ShapeDtypeStruct((bsz, n, B_HEADS * B_QK_PAD), BF16),
                 jax.ShapeDtypeStruct((bsz, n, B_HEADS * B_V), BF16),
                 jax.ShapeDtypeStruct((bsz, n, 512), F32),
                 jax.ShapeDtypeStruct((bsz, n, 256), F32),
                 jax.ShapeDtypeStruct((bsz, n, 256), F32),
                 jax.ShapeDtypeStruct((bsz, n, 512), BF16),
                 jax.ShapeDtypeStruct((bsz, n, 512), F32)]
    out_specs = [pl.BlockSpec((1, tm, s.shape[2]), row) for s in out_shape]
    return pl.pallas_call(
        functools.partial(_proj_kernel, rope=rope),
        out_shape=out_shape,
        grid=(bsz, n // tm),
        in_specs=in_specs,
        out_specs=out_specs,
        compiler_params=_cparams(("parallel", "parallel")),
        name="proj_ctx" if ctx else "proj_lat",
    )(*args)


def _attn_kernel(*refs, groups, dq, dv, tk, rsub, has_lat):
    if has_lat:
        q_ref, kc_ref, vc_ref, kl_ref, vl_ref, o_ref, q_scr, s_scr, p_scr, m_scr, l_scr, acc_scr = refs
    else:
        q_ref, kc_ref, vc_ref, o_ref, q_scr, s_scr, p_scr, m_scr, l_scr, acc_scr = refs
    tq = q_ref.shape[1]
    m_rows = groups * tq
    nt = (((1,), (1,)), ((), ()))
    for g in range(groups):
        q_scr[g * tq:(g + 1) * tq, :] = q_ref[0, :, g * dq:(g + 1) * dq]
    m_scr[...] = jnp.full(m_scr.shape, -jnp.inf, F32)
    l_scr[...] = jnp.zeros(l_scr.shape, F32)
    acc_scr[...] = jnp.zeros(acc_scr.shape, F32)

    def scores(slot, k):
        s_scr[slot, :, 0:k.shape[0]] = lax.dot_general(q_scr[...], k, nt, preferred_element_type=F32)

    def softmax_pv(slot, v):
        w = v.shape[0]
        for r in range(m_rows // rsub):
            rs = slice(r * rsub, (r + 1) * rsub)
            cols = [s_scr[slot, rs, j * 128:(j + 1) * 128] for j in range(w // 128)]
            mx = functools.reduce(jnp.maximum, cols)
            m_old = m_scr[rs, :]
            m_new = jnp.maximum(m_old, jnp.max(mx, axis=-1, keepdims=True))
            ps = [jnp.exp2(c - m_new) for c in cols]
            a = jnp.exp2(m_old - m_new)
            l_scr[rs, :] = a * l_scr[rs, :] + functools.reduce(jnp.add, ps)
            m_scr[rs, :] = m_new
            acc_scr[rs, :] = a * acc_scr[rs, :]
            for j, p in enumerate(ps):
                p_scr[rs, j * 128:(j + 1) * 128] = p.astype(BF16)
        acc_scr[...] += jnp.dot(p_scr[:, 0:w], v, preferred_element_type=F32)

    chunks = []
    if has_lat:
        chunks += [(kl_ref, vl_ref, slice(c * tk, (c + 1) * tk)) for c in range(kl_ref.shape[1] // tk)]
    chunks.append((kc_ref, vc_ref, slice(0, kc_ref.shape[1])))
    scores(0, chunks[0][0][0, chunks[0][2], :])
    for n, (_, v_ref, rows) in enumerate(chunks):
        if n + 1 < len(chunks):
            k_next, _, rows_next = chunks[n + 1]
            scores((n + 1) % 2, k_next[0, rows_next, :])
        softmax_pv(n % 2, v_ref[0, rows, :])
    for g in range(groups):
        gs = slice(g * tq, (g + 1) * tq)
        l = jnp.sum(l_scr[gs, :], axis=-1, keepdims=True)
        o_ref[0, :, g * dv:(g + 1) * dv] = (acc_scr[gs, :] * (1.0 / l)).astype(o_ref.dtype)


def _attention(q, kc, vc, kl, vl, *, kv_heads, groups, dq, dv, tq, tk=1024, rsub=64):
    assert dv == 128, "the accumulator rescale reuses the 128-lane replicated running max"
    bsz, n, _ = q.shape
    nc = kc.shape[1]
    has_lat = kl is not None
    tq = min(tq, n)
    m_rows = groups * tq
    wmax = max(nc, tk) if has_lat else nc
    in_specs = [pl.BlockSpec((1, tq, groups * dq), lambda b, h, i: (b, i, h)),
                pl.BlockSpec((1, nc, dq), lambda b, h, i: (b, 0, h)),
                pl.BlockSpec((1, nc, dv), lambda b, h, i: (b, 0, h))]
    args = [q, kc, vc]
    if has_lat:
        nl = kl.shape[1]
        in_specs += [pl.BlockSpec((1, nl, dq), lambda b, h, i: (b, 0, h)),
                     pl.BlockSpec((1, nl, dv), lambda b, h, i: (b, 0, h))]
        args += [kl, vl]
    return pl.pallas_call(
        functools.partial(_attn_kernel, groups=groups, dq=dq, dv=dv, tk=tk, rsub=rsub, has_lat=has_lat),
        out_shape=jax.ShapeDtypeStruct((bsz, n, kv_heads * groups * dv), BF16),
        grid=(bsz, kv_heads, n // tq),
        in_specs=in_specs,
        out_specs=pl.BlockSpec((1, tq, groups * dv), lambda b, h, i: (b, i, h)),
        scratch_shapes=[pltpu.VMEM((m_rows, dq), BF16), pltpu.VMEM((2, m_rows, wmax), F32),
                        pltpu.VMEM((m_rows, wmax), BF16), pltpu.VMEM((m_rows, 128), F32),
                        pltpu.VMEM((m_rows, 128), F32), pltpu.VMEM((m_rows, dv), F32)],
        compiler_params=_cparams(("parallel", "parallel", "arbitrary")),
        name="attention",
    )(*args)


def _gla_kernel(q_ref, k_ref, v_ref, g_ref, s0_ref, o_ref, sfin_ref, st_ref, att_ref, cb_ref, kk_ref, *, chunk):
    d = pl.program_id(0)
    i = pl.program_id(2)
    t_rows = q_ref.shape[1]
    nch = t_rows // chunk

    @pl.when(i == 0)
    def _():
        st_ref[...] = s0_ref[0, 0]

    row = lax.broadcasted_iota(jnp.int32, (chunk, chunk), 0)
    col = lax.broadcasted_iota(jnp.int32, (chunk, chunk), 1)
    tri = jnp.where(d == 0, col - row, row - col) <= 0
    tri_bf = jnp.where(tri, 1.0, 0.0).astype(BF16)
    row2 = lax.broadcasted_iota(jnp.int32, (chunk, 2 * chunk), 0)
    lane2 = lax.broadcasted_iota(jnp.int32, (chunk, 2 * chunk), 1)
    col2 = jnp.where(lane2 >= chunk, lane2 - chunk, lane2)
    tri2 = jnp.where(d == 0, col2 - row2, row2 - col2) <= 0
    lo = lax.broadcasted_iota(jnp.int32, (chunk, 128), 1) < C_DK
    vlo = lax.broadcasted_iota(jnp.int32, (chunk, 2 * C_DV), 1) < C_DV
    srow = lax.broadcasted_iota(jnp.int32, (2 * C_DV, 2 * C_DK), 0) // C_DV
    scol = lax.broadcasted_iota(jnp.int32, (2 * C_DV, 2 * C_DK), 1) // C_DK
    same_head = srow == scol
    nt = (((1,), (1,)), ((), ()))
    tn = (((0,), (0,)), ((), ()))

    def intra_scores_mxu(q_p, k_p, cb_p, qe_p):
        ke_p = (k_p * jnp.exp(-cb_p)).astype(BF16)
        zk = jnp.zeros_like(ke_p)
        ke_bd = jnp.concatenate([jnp.where(lo, ke_p, zk), jnp.where(lo, zk, ke_p)], axis=0)
        return lax.dot_general(qe_p, ke_bd, nt, preferred_element_type=F32)

    def intra_scores_guarded(q_p, k_p, cb_p, qe_p):
        cb_ref[...] = cb_p
        kk_ref[...] = k_p
        att_ref[...] = jnp.zeros(att_ref.shape, F32)

        def one_source(s, carry):
            cb_s = cb_ref[pl.ds(s, 1), :]
            w = q_p * jnp.exp(jnp.minimum(cb_p - cb_s, 0.0)) * kk_ref[pl.ds(s, 1), :]
            w0 = jnp.sum(jnp.where(lo, w, 0.0), axis=-1, keepdims=True)
            w1 = jnp.sum(jnp.where(lo, 0.0, w), axis=-1, keepdims=True)
            att_ref[...] += jnp.where(lane2 == s, w0, 0.0) + jnp.where(lane2 == s + chunk, w1, 0.0)
            return carry

        lax.fori_loop(0, chunk, one_source, 0)
        return att_ref[...]

    def scan_block(intra_scores):
        for c in range(nch):
            cc = c + d * (nch - 1 - 2 * c)
            r0 = pl.multiple_of(cc * chunk, chunk)
            g = g_ref[0, pl.ds(r0, chunk), :]
            g_hi = g.astype(BF16)
            g_lo = (g - g_hi.astype(F32)).astype(BF16)
            cb = (jnp.dot(tri_bf, g_hi, preferred_element_type=F32)
                  + jnp.dot(tri_bf, g_lo, preferred_element_type=F32))
            tot = jnp.sum(g, axis=0, keepdims=True)
            q = q_ref[0, pl.ds(r0, chunk), :] * (C_DK ** -0.5)
            k = k_ref[0, pl.ds(r0, chunk), :]
            v = v_ref[0, pl.ds(r0, chunk), :]
            qe = (q * jnp.exp(cb)).astype(BF16)
            kd = (k * jnp.exp(tot - cb)).astype(BF16)
            dec = jnp.exp(tot)
            for p in range(C_HEADS // 2):
                ls = slice(p * 128, (p + 1) * 128)
                vs = slice(p * 2 * C_DV, (p + 1) * 2 * C_DV)
                qe_p, kd_p, v_p = qe[:, ls], kd[:, ls], v[:, vs]
                att = intra_scores(q[:, ls], k[:, ls], cb[:, ls], qe_p)
                att = jnp.where(tri2, att, 0.0).astype(BF16)
                zv = jnp.zeros_like(v_p)
                v_bd = jnp.concatenate([jnp.where(vlo, v_p, zv), jnp.where(vlo, zv, v_p)], axis=0)
                st = st_ref[p]
                o = (jnp.dot(att, v_bd, preferred_element_type=F32)
                     + lax.dot_general(qe_p, st.astype(BF16), nt, preferred_element_type=F32))
                o_ref[0, 0, pl.ds(r0, chunk), vs] = o
                u = lax.dot_general(v_p, kd_p, tn, preferred_element_type=F32)
                st_ref[p] = dec[:, ls] * st + jnp.where(same_head, u, 0.0)

    worst = functools.reduce(jnp.maximum, [
        jnp.max(-jnp.sum(g_ref[0, c * chunk:(c + 1) * chunk, :], axis=0, keepdims=True)) for c in range(nch)])
    safe = worst <= GLA_SAFE_DECAY

    @pl.when(safe)
    def _():
        scan_block(intra_scores_mxu)

    @pl.when(jnp.logical_not(safe))
    def _():
        scan_block(intra_scores_guarded)

    @pl.when(i == pl.num_programs(2) - 1)
    def _():
        sfin_ref[0, 0] = st_ref[...]


def _gla(cq, ck, cv, gdec, s0):
    bsz, n, _ = cq.shape
    t_rows = min(512, n)
    nb = n // t_rows
    rb = lambda d, i: i + d * (nb - 1 - 2 * i)
    o, sfin = pl.pallas_call(
        functools.partial(_gla_kernel, chunk=GLA_CHUNK),
        out_shape=[jax.ShapeDtypeStruct((2, bsz, n, C_HEADS * C_DV), F32),
                   jax.ShapeDtypeStruct(s0.shape, F32)],
        grid=(2, bsz, nb),
        in_specs=[pl.BlockSpec((1, t_rows, 256), lambda d, b, i: (b, rb(d, i), 0)),
                  pl.BlockSpec((1, t_rows, 256), lambda d, b, i: (b, rb(d, i), 0)),
                  pl.BlockSpec((1, t_rows, 512), lambda d, b, i: (b, rb(d, i), 0)),
                  pl.BlockSpec((1, t_rows, 256), lambda d, b, i: (b, rb(d, i), d)),
                  pl.BlockSpec((1, 1, 2, 256, 128), lambda d, b, i: (d, b, 0, 0, 0))],
        out_specs=[pl.BlockSpec((1, 1, t_rows, 512), lambda d, b, i: (d, b, rb(d, i), 0)),
                   pl.BlockSpec((1, 1, 2, 256, 128), lambda d, b, i: (d, b, 0, 0, 0))],
        scratch_shapes=[pltpu.VMEM((2, 256, 128), F32), pltpu.VMEM((GLA_CHUNK, 2 * GLA_CHUNK), F32),
                        pltpu.VMEM((GLA_CHUNK, 128), F32), pltpu.VMEM((GLA_CHUNK, 128), F32)],
        compiler_params=_cparams(("parallel", "parallel", "arbitrary")),
        name="gla",
    )(cq, ck, cv, gdec, s0)
    return o, sfin


def _outproj_kernel(oa_ref, ob_ref, oc_ref, cg_ref, gn_ref, w_ref, x_ref, mod_ref, o_ref, z_ref):
    z_ref[:, 0:1024] = oa_ref[0]
    z_ref[:, 1024:1536] = ob_ref[0]
    for h in range(C_HEADS):
        hs = slice(h * C_DV, (h + 1) * C_DV)
        oc = oc_ref[0, 0, :, hs] + oc_ref[1, 0, :, hs]
        z_ref[:, 1536 + h * C_DV:1536 + (h + 1) * C_DV] = (
            _rms(oc, gn_ref[...]) * _silu(cg_ref[0, :, hs])).astype(BF16)
    y = jnp.dot(z_ref[...], w_ref[...], preferred_element_type=F32)
    o_ref[0] = x_ref[0] + mod_ref[0, 2:3, :] * y


def _outproj(oa, ob, oc, cg, gn, w, x, mod, layer, ctx):
    bsz, n, d = x.shape
    tm = min(512, n)
    return pl.pallas_call(
        _outproj_kernel,
        out_shape=jax.ShapeDtypeStruct((bsz, n, d), F32),
        grid=(bsz, n // tm),
        in_specs=[pl.BlockSpec((1, tm, 1024), lambda b, i: (b, i, 0)),
                  pl.BlockSpec((1, tm, 512), lambda b, i: (b, i, 0)),
                  pl.BlockSpec((2, 1, tm, 512), lambda b, i: (0, b, i, 0)),
                  pl.BlockSpec((1, tm, 512), lambda b, i: (b, i, 0)),
                  pl.BlockSpec((None, 1, C_DV), lambda b, i: (layer, 0, 0)),
                  pl.BlockSpec((None,) + w.shape[1:], lambda b, i: (layer, 0, 0), pipeline_mode=pl.Buffered(1)),
                  pl.BlockSpec((1, tm, d), lambda b, i: (b, i, 0)),
                  pl.BlockSpec((1, 6, d), _mod_index(ctx))],
        out_specs=pl.BlockSpec((1, tm, d), lambda b, i: (b, i, 0)),
        scratch_shapes=[pltpu.VMEM((tm, w.shape[1]), BF16)],
        compiler_params=_cparams(("parallel", "parallel")),
        name="outproj",
    )(oa, ob, oc, cg, gn, w, x, mod)


def _ffn_kernel(x_ref, mod_ref, g_ref, wg_ref, wu_ref, wd_ref, o_ref, h_ref):
    j = pl.program_id(2)

    @pl.when(j == 0)
    def _():
        y = _rms(x_ref[0], g_ref[...])
        h_ref[...] = (y * (1.0 + mod_ref[0, 4:5, :]) + mod_ref[0, 3:4, :]).astype(BF16)
        o_ref[0] = jnp.zeros(o_ref.shape[1:], F32)

    h = h_ref[...]
    a = jnp.dot(h, wg_ref[...], preferred_element_type=F32)
    u = jnp.dot(h, wu_ref[...], preferred_element_type=F32)
    t = (_silu(a) * u).astype(BF16)
    o_ref[0] += jnp.dot(t, wd_ref[...], preferred_element_type=F32)

    @pl.when(j == pl.num_programs(2) - 1)
    def _():
        o_ref[0] = x_ref[0] + mod_ref[0, 5:6, :] * o_ref[0]


def _ffn(x, mod, g, wg, wu, wd, layer, ctx):
    bsz, n, d = x.shape
    f = wg.shape[2]
    tm = min(512, n)
    tf = 512
    return pl.pallas_call(
        _ffn_kernel,
        out_shape=jax.ShapeDtypeStruct((bsz, n, d), F32),
        grid=(bsz, n // tm, f // tf),
        in_specs=[pl.BlockSpec((1, tm, d), lambda b, i, j: (b, i, 0)),
                  pl.BlockSpec((1, 6, d), _mod_index(ctx)),
                  pl.BlockSpec((None, 1, d), lambda b, i, j: (layer, 0, 0)),
                  pl.BlockSpec((None, d, tf), lambda b, i, j: (layer, 0, j)),
                  pl.BlockSpec((None, d, tf), lambda b, i, j: (layer, 0, j)),
                  pl.BlockSpec((None, tf, d), lambda b, i, j: (layer, j, 0))],
        out_specs=pl.BlockSpec((1, tm, d), lambda b, i, j: (b, i, 0)),
        scratch_shapes=[pltpu.VMEM((tm, d), BF16)],
        compiler_params=_cparams(("parallel", "parallel", "arbitrary")),
        name="ffn",
    )(x, mod, g, wg, wu, wd)


def _rope_tables(n):
    pos = np.arange(n)
    row = (pos // GRID_W).astype(np.float64)[:, None]
    col = (pos % GRID_W).astype(np.float64)[:, None]

    def table(half):
        inv = ROPE_THETA ** (-np.arange(half, dtype=np.float64) / half)
        ar, ac = row * inv[None, :], col * inv[None, :]
        cos = np.concatenate([np.cos(ar), np.cos(ar), np.cos(ac), np.cos(ac)], axis=1)
        sin = np.concatenate([-np.sin(ar), np.sin(ar), -np.sin(ac), np.sin(ac)], axis=1)
        return cos, sin

    cos_a, sin_a = table(A_HEAD_DIM // 4)
    cos_b, sin_b = table(B_ROPE // 4)
    cos_b, sin_b = np.tile(cos_b, (1, 2)), np.tile(sin_b, (1, 2))
    return tuple(jnp.asarray(t, F32) for t in (cos_a, sin_a, cos_b, sin_b))


def _prepare_weights(w_in, w_uq, w_ukv, w_gk_f, b_gk_f, w_gk_b, b_gk_b, w_out, w_gate, w_up, w_down, smalls):
    depth, d, _ = w_in.shape
    w_r = jnp.concatenate([
        w_in[:, :, 0:2304],
        w_in[:, :, 2304:2368], w_in[:, :, 2304:2368],
        w_in[:, :, 3904:3936], jnp.zeros((depth, d, 96), w_in.dtype),
        w_in[:, :, 2368:3904],
    ], axis=2).astype(BF16)
    uq = w_uq.reshape(depth, B_Q_RANK, B_HEADS, B_NOPE + B_ROPE)
    uq = jnp.concatenate([uq[..., :B_NOPE].reshape(depth, B_Q_RANK, -1),
                          uq[..., B_NOPE:].reshape(depth, B_Q_RANK, -1)], axis=2)
    ukv = w_ukv.reshape(depth, B_KV_RANK, B_HEADS, B_NOPE + B_V)
    ukv = jnp.concatenate([ukv[..., :B_NOPE].reshape(depth, B_KV_RANK, -1),
                           ukv[..., B_NOPE:].reshape(depth, B_KV_RANK, -1)], axis=2)
    nk = C_HEADS * C_DK
    w_gk = jnp.zeros((depth, 128, 2 * nk), F32)
    w_gk = w_gk.at[:, 0:C_GATE_RANK, 0:nk].set(w_gk_f).at[:, C_GATE_RANK:2 * C_GATE_RANK, nk:].set(w_gk_b)
    lw = {k: v.reshape(depth, 1, -1) for k, v in smalls.items()}
    lw["b_q_rope_norm2"] = jnp.tile(lw["b_q_rope_norm"], (1, 1, 2))
    lw["b_k_rope_norm2"] = jnp.tile(lw["b_k_rope_norm"], (1, 1, 2))
    lw.update(w_in=w_r, w_uq=uq.astype(BF16), w_ukv=ukv.astype(BF16), w_gk=w_gk,
              b_gk=jnp.concatenate([b_gk_f, b_gk_b], axis=1).reshape(depth, 1, -1),
              w_out=w_out.astype(BF16), w_gate=w_gate.astype(BF16),
              w_up=w_up.astype(BF16), w_down=w_down.astype(BF16))
    return lw


def kernel(x, c, ctx, c_ctx, w_mod, b_mod, norm1_g, norm2_g, w_in, a_q_norm, a_k_norm, b_q_lora_norm, b_kv_lora_norm, w_uq, w_ukv, b_q_nope_norm, b_k_nope_norm, b_q_rope_norm, b_k_rope_norm, w_gk_f, b_gk_f, w_gk_b, b_gk_b, c_out_norm, w_out, w_gate, w_up, w_down):
    bsz, n, d = x.shape
    depth = w_in.shape[0]
    tabs = _rope_tables(n)
    assert bsz + 1 <= 8, "the modulation kernel holds the context row and all batch rows in one 8-row tile"
    cpad = jnp.concatenate([c_ctx[None, :], c, jnp.zeros((8 - bsz - 1, d), F32)], axis=0)
    smalls = dict(a_q_norm=a_q_norm, a_k_norm=a_k_norm, b_q_lora_norm=b_q_lora_norm,
                  b_kv_lora_norm=b_kv_lora_norm, b_q_nope_norm=b_q_nope_norm, b_k_nope_norm=b_k_nope_norm,
                  b_q_rope_norm=b_q_rope_norm, b_k_rope_norm=b_k_rope_norm, c_out_norm=c_out_norm,
                  norm1_g=norm1_g, norm2_g=norm2_g)
    lw = _prepare_weights(w_in, w_uq, w_ukv, w_gk_f, b_gk_f, w_gk_b, b_gk_b, w_out, w_gate, w_up, w_down, smalls)
    b_mod3 = b_mod.reshape(depth, 1, -1)
    s_zero = jnp.zeros((2, bsz, 2, 2 * C_DV, 2 * C_DK), F32)
    a_args = dict(kv_heads=A_KV_HEADS, groups=A_HEADS // A_KV_HEADS, dq=A_HEAD_DIM, dv=A_HEAD_DIM)
    b_args = dict(kv_heads=B_HEADS, groups=1, dq=B_QK_PAD, dv=B_V)
    xc = ctx
    for l in range(depth):
        ctx_out = l < depth - 1
        mod = _modulation(cpad, w_mod, b_mod3, l)
        caq, cak, cav, cbq, cbk, cbv, cgd, ccq, cck, ccv, ccg = _proj(xc, mod, lw["norm1_g"], None, lw, l, True)
        laq, lak, lav, lbq, lbk, lbv, lgd, lcq, lck, lcv, lcg = _proj(x, mod, lw["norm1_g"], tabs, lw, l, False)

        o_a = _attention(laq, cak, cav, lak, lav, tq=512, **a_args)
        o_b = _attention(lbq, cbk, cbv, lbk, lbv, tq=2048, **b_args)
        oc_c, s_ctx = _gla(ccq, cck, ccv, cgd, s_zero)
        o_c, _ = _gla(lcq, lck, lcv, lgd, s_ctx)

        x = _outproj(o_a, o_b, o_c, lcg, lw["c_out_norm"], lw["w_out"], x, mod, l, False)
        x = _ffn(x, mod, lw["norm2_g"], lw["w_gate"], lw["w_up"], lw["w_down"], l, False)
        if ctx_out:
            oc_a = _attention(caq, cak, cav, None, None, tq=256, **a_args)
            oc_b = _attention(cbq, cbk, cbv, None, None, tq=256, **b_args)
            xc = _outproj(oc_a, oc_b, oc_c, ccg, lw["c_out_norm"], lw["w_out"], xc, mod, l, True)
            xc = _ffn(xc, mod, lw["norm2_g"], lw["w_gate"], lw["w_up"], lw["w_down"], l, True)
    return x
```

```python
import functools

import numpy as np
import jax
import jax.numpy as jnp
from jax import lax
from jax.experimental import pallas as pl
from jax.experimental.pallas import tpu as pltpu

F32 = jnp.float32
BF16 = jnp.bfloat16

D_MODEL = 2048
GRID_W = 64
ROPE_THETA = 10000.0
NORM_EPS = 1e-6

A_HEADS = 8
A_KV_HEADS = 2
A_HEAD_DIM = 128
B_HEADS = 4
B_Q_RANK = 512
B_KV_RANK = 256
B_NOPE = 128
B_ROPE = 64
B_V = 128
B_QK_PAD = 256
C_HEADS = 4
C_DK = 64
C_DV = 128
C_GATE_RANK = 16
C_GATE_NORM = 16.0

D_FF = 5632
P_COLS = 4096

COL_AQ = 0
COL_AK = 1024
COL_BCQ = 1536
COL_BCKV = 2048
COL_CQ = 2560
COL_CV = 3072
COL_CG = 3584

GLA_CHUNK = 128
GLA_SAFE_DECAY = 60.0

VMEM_LIMIT = 56 * 2**20
LOG2E = 1.4426950408889634


def _cparams(sem, vmem=VMEM_LIMIT):
    return pltpu.CompilerParams(dimension_semantics=sem, vmem_limit_bytes=vmem)


def _silu(x):
    return x / (1.0 + jnp.exp(-x))


def _rms(x, g):
    ms = jnp.mean(x * x, axis=-1, keepdims=True)
    return x * lax.rsqrt(ms + NORM_EPS) * g


def _mod_kernel(c_ref, w_ref, b_ref, o_ref):
    s = _silu(c_ref[...]).astype(BF16)
    o_ref[...] = jnp.dot(s, w_ref[...].astype(BF16), preferred_element_type=F32) + b_ref[...]


def _modulation(cpad, w_mod, b_mod, layer):
    d = cpad.shape[1]
    n = w_mod.shape[2]
    tn = 1024
    out = pl.pallas_call(
        _mod_kernel,
        out_shape=jax.ShapeDtypeStruct((8, n), F32),
        grid=(n // tn,),
        in_specs=[pl.BlockSpec((8, d), lambda j: (0, 0)),
                  pl.BlockSpec((None, d, tn), lambda j: (layer, 0, j)),
                  pl.BlockSpec((None, 1, tn), lambda j: (layer, 0, j))],
        out_specs=pl.BlockSpec((8, tn), lambda j: (0, j)),
        compiler_params=_cparams(("arbitrary",)),
        name="modulation",
    )(cpad, w_mod, b_mod)
    return out.reshape(8, 6, d)


def _mod_index(ctx):
    if ctx:
        return lambda b, i, *_: (0, 0, 0)
    return lambda b, i, *_: (b + 1, 0, 0)


def _swap_halves(x, lane, width):
    return jnp.where((lane // width) % 2 == 0,
                     pltpu.roll(x, 128 - width, 1), pltpu.roll(x, width, 1))


def _proj_kernel(*refs, rope):
    x_ref, mod_ref, g1_ref, w_ref = refs[:4]
    refs = refs[4:]
    if rope:
        cosa_ref, sina_ref, cosb_ref, sinb_ref = refs[:4]
        refs = refs[4:]
    (gaq_ref, gak_ref, gql_ref, gkvl_ref, gqn_ref, gkn_ref, gqr_ref, gkr_ref,
     wuq_ref, wukv_ref, wgk_ref, bgk_ref,
     aq_ref, ak_ref, av_ref, bq_ref, bk_ref, bv_ref, gd_ref, cq_ref, ck_ref, cv_ref, cg_ref) = refs
    tm = x_ref.shape[1]
    lane = lax.broadcasted_iota(jnp.int32, (tm, 128), 1)
    lo = lane < 64

    y = _rms(x_ref[0], g1_ref[...])
    h = (y * (1.0 + mod_ref[0, 1:2, :]) + mod_ref[0, 0:1, :]).astype(BF16)

    def proj(c0, width):
        return jnp.dot(h, w_ref[:, c0:c0 + width], preferred_element_type=F32)

    def rope_a(x):
        if not rope:
            return x
        return x * cosa_ref[...] + _swap_halves(x, lane, 32) * sina_ref[...]

    def rope_b(x):
        if not rope:
            return x
        return x * cosb_ref[...] + _swap_halves(x, lane, 16) * sinb_ref[...]

    a_scale = LOG2E * A_HEAD_DIM ** -0.5
    for half in range(2):
        pq = proj(COL_AQ + half * 512, 512)
        for j in range(4):
            hh = half * 4 + j
            x = pq[:, j * 128:(j + 1) * 128]
            aq_ref[0, :, hh * 128:(hh + 1) * 128] = (rope_a(_rms(x, gaq_ref[...])) * a_scale).astype(BF16)
    pkv = proj(COL_AK, 512)
    for hh in range(A_KV_HEADS):
        x = pkv[:, hh * 128:(hh + 1) * 128]
        ak_ref[0, :, hh * 128:(hh + 1) * 128] = rope_a(_rms(x, gak_ref[...])).astype(BF16)
    av_ref[0] = pkv[:, 256:512].astype(BF16)

    b_scale = LOG2E * (B_NOPE + B_ROPE) ** -0.5
    cq = _rms(proj(COL_BCQ, 512), gql_ref[...]).astype(BF16)
    qb = jnp.dot(cq, wuq_ref[...], preferred_element_type=F32)
    for hh in range(B_HEADS):
        x = qb[:, hh * 128:(hh + 1) * 128]
        bq_ref[0, :, hh * B_QK_PAD:hh * B_QK_PAD + 128] = (_rms(x, gqn_ref[...]) * b_scale).astype(BF16)
    for p in range(B_HEADS // 2):
        r = qb[:, 512 + p * 128:512 + (p + 1) * 128]
        sq = r * r
        ms_lo = jnp.sum(jnp.where(lo, sq, 0.0), axis=-1, keepdims=True)
        ms_hi = jnp.sum(jnp.where(lo, 0.0, sq), axis=-1, keepdims=True)
        ms = jnp.where(lo, ms_lo, ms_hi) * (1.0 / B_ROPE)
        rn = rope_b(r * lax.rsqrt(ms + NORM_EPS) * gqr_ref[...]) * b_scale
        h0 = 2 * p
        bq_ref[0, :, h0 * B_QK_PAD + 128:(h0 + 1) * B_QK_PAD] = jnp.where(lo, rn, 0.0).astype(BF16)
        bq_ref[0, :, (h0 + 1) * B_QK_PAD + 128:(h0 + 2) * B_QK_PAD] = jnp.where(lo, 0.0, rn).astype(BF16)

    pmix = proj(COL_BCKV, 512)
    ckv = _rms(pmix[:, 0:256], gkvl_ref[...]).astype(BF16)
    kv = jnp.dot(ckv, wukv_ref[...], preferred_element_type=F32)
    krn = rope_b(_rms(pmix[:, 256:384], gkr_ref[...]))
    kr_even = jnp.where(lo, krn, 0.0).astype(BF16)
    kr_odd = jnp.where(lo, 0.0, krn).astype(BF16)
    for hh in range(B_HEADS):
        x = kv[:, hh * 128:(hh + 1) * 128]
        bk_ref[0, :, hh * B_QK_PAD:hh * B_QK_PAD + 128] = _rms(x, gkn_ref[...]).astype(BF16)
        bk_ref[0, :, hh * B_QK_PAD + 128:(hh + 1) * B_QK_PAD] = kr_even if hh % 2 == 0 else kr_odd
    bv_ref[0] = kv[:, 512:1024].astype(BF16)

    z = jnp.dot(pmix[:, 384:512], wgk_ref[...], preferred_element_type=F32,
                precision=lax.Precision.HIGHEST) + bgk_ref[...]
    gd_ref[0] = (jnp.minimum(z, 0.0) - jnp.log(1.0 + jnp.exp(-jnp.abs(z)))) * (1.0 / C_GATE_NORM)
    pqk = proj(COL_CQ, 512)
    cq_ref[0] = pqk[:, 0:256]
    ck_ref[0] = pqk[:, 256:512]
    cv_ref[0] = proj(COL_CV, 512).astype(BF16)
    cg_ref[0] = proj(COL_CG, 512)


def _proj(x, mod, g1, tabs, lw, layer, ctx):
    bsz, n, d = x.shape
    tm = min(512, n)
    rope = not ctx
    row = lambda b, i: (b, i, 0)
    once = pl.Buffered(1)
    in_specs = [pl.BlockSpec((1, tm, d), row),
                pl.BlockSpec((1, 6, d), _mod_index(ctx)),
                pl.BlockSpec((None, 1, d), lambda b, i: (layer, 0, 0)),
                pl.BlockSpec((None, d, P_COLS), lambda b, i: (layer, 0, 0), pipeline_mode=once)]
    args = [x, mod, g1, lw["w_in"]]
    if rope:
        in_specs += [pl.BlockSpec((tm, 128), lambda b, i: (i, 0))] * 4
        args += list(tabs)
    small = [lw["a_q_norm"], lw["a_k_norm"], lw["b_q_lora_norm"], lw["b_kv_lora_norm"],
             lw["b_q_nope_norm"], lw["b_k_nope_norm"], lw["b_q_rope_norm2"], lw["b_k_rope_norm2"],
             lw["w_uq"], lw["w_ukv"], lw["w_gk"], lw["b_gk"]]
    in_specs += [pl.BlockSpec((None,) + a.shape[1:], lambda b, i: (layer, 0, 0), pipeline_mode=once) for a in small]
    args += small
    out_shape = [jax.ShapeDtypeStruct((bsz, n, 1024), BF16),
                 jax.ShapeDtypeStruct((bsz, n, 256), BF16),
                 jax.ShapeDtypeStruct((bsz, n, 256), BF16),
                 jax.ShapeDtypeStruct((bsz, n, B_HEADS * B_QK_PAD), BF16),
                 jax.ShapeDtypeStruct((bsz, n, B_HEADS * B_QK_PAD), BF16),
                 jax.ShapeDtypeStruct((bsz, n, B_HEADS * B_V), BF16),
                 jax.ShapeDtypeStruct((bsz, n, 512), F32),
                 jax.ShapeDtypeStruct((bsz, n, 256), F32),
                 jax.ShapeDtypeStruct((bsz, n, 256), F32),
                 jax.ShapeDtypeStruct((bsz, n, 512), BF16),
                 jax.ShapeDtypeStruct((bsz, n, 512), F32)]
    out_specs = [pl.BlockSpec((1, tm, s.shape[2]), row) for s in out_shape]
    return pl.pallas_call(
        functools.partial(_proj_kernel, rope=rope),
        out_shape=out_shape,
        grid=(bsz, n // tm),
        in_specs=in_specs,
        out_specs=out_specs,
        compiler_params=_cparams(("parallel", "parallel")),
        name="proj_ctx" if ctx else "proj_lat",
    )(*args)


def _attn_kernel(*refs, groups, dq, dv, tk, rsub, has_lat, n_cast):
    if has_lat:
        q_ref, kc_ref, vc_ref, kl_ref, vl_ref = refs[:5]
        refs = refs[5:]
    else:
        q_ref, kc_ref, vc_ref = refs[:3]
        refs = refs[3:]
    cast_in, refs = refs[:n_cast], refs[n_cast:]
    o_ref, cast_out = refs[0], refs[1:1 + n_cast]
    q_scr, s_scr, p_scr, m_scr, l_scr, acc_scr = refs[1 + n_cast:]
    for src_ref, dst_ref in zip(cast_in, cast_out):
        dst_ref[...] = src_ref[...].astype(BF16)
    tq = q_ref.shape[1]
    m_rows = groups * tq
    nt = (((1,), (1,)), ((), ()))
    for g in range(groups):
        q_scr[g * tq:(g + 1) * tq, :] = q_ref[0, :, g * dq:(g + 1) * dq]
    m_scr[...] = jnp.full(m_scr.shape, -jnp.inf, F32)
    l_scr[...] = jnp.zeros(l_scr.shape, F32)
    acc_scr[...] = jnp.zeros(acc_scr.shape, F32)

    def scores(slot, k):
        s_scr[slot, :, 0:k.shape[0]] = lax.dot_general(q_scr[...], k, nt, preferred_element_type=F32)

    def softmax_pv(slot, v):
        w = v.shape[0]
        for r in range(m_rows // rsub):
            rs = slice(r * rsub, (r + 1) * rsub)
            cols = [s_scr[slot, rs, j * 128:(j + 1) * 128] for j in range(w // 128)]
            mx = functools.reduce(jnp.maximum, cols)
            m_old = m_scr[rs, :]
            m_new = jnp.maximum(m_old, jnp.max(mx, axis=-1, keepdims=True))
            ps = [jnp.exp2(c - m_new) for c in cols]
            a = jnp.exp2(m_old - m_new)
            l_scr[rs, :] = a * l_scr[rs, :] + functools.reduce(jnp.add, ps)
            m_scr[rs, :] = m_new
            acc_scr[rs, :] = a * acc_scr[rs, :]
            for j, p in enumerate(ps):
                p_scr[rs, j * 128:(j + 1) * 128] = p.astype(BF16)
        acc_scr[...] += jnp.dot(p_scr[:, 0:w], v, preferred_element_type=F32)

    chunks = []
    if has_lat:
        chunks += [(kl_ref, vl_ref, slice(c * tk, (c + 1) * tk)) for c in range(kl_ref.shape[1] // tk)]
    chunks.append((kc_ref, vc_ref, slice(0, kc_ref.shape[1])))
    scores(0, chunks[0][0][0, chunks[0][2], :])
    for n, (_, v_ref, rows) in enumerate(chunks):
        if n + 1 < len(chunks):
            k_next, _, rows_next = chunks[n + 1]
            scores((n + 1) % 2, k_next[0, rows_next, :])
        softmax_pv(n % 2, v_ref[0, rows, :])
    for g in range(groups):
        gs = slice(g * tq, (g + 1) * tq)
        l = jnp.sum(l_scr[gs, :], axis=-1, keepdims=True)
        o_ref[0, :, g * dv:(g + 1) * dv] = (acc_scr[gs, :] * (1.0 / l)).astype(o_ref.dtype)


def _attention(q, kc, vc, kl, vl, *, kv_heads, groups, dq, dv, tq, tk=1024, rsub=64, cast=(), layer=0):
    assert dv == 128, "the accumulator rescale reuses the 128-lane replicated running max"
    bsz, n, _ = q.shape
    nc = kc.shape[1]
    has_lat = kl is not None
    tq = min(tq, n)
    m_rows = groups * tq
    nq = n // tq
    steps = bsz * kv_heads * nq
    wmax = max(nc, tk) if has_lat else nc
    in_specs = [pl.BlockSpec((1, tq, groups * dq), lambda b, h, i: (b, i, h)),
                pl.BlockSpec((1, nc, dq), lambda b, h, i: (b, 0, h)),
                pl.BlockSpec((1, nc, dv), lambda b, h, i: (b, 0, h))]
    args = [q, kc, vc]
    if has_lat:
        nl = kl.shape[1]
        in_specs += [pl.BlockSpec((1, nl, dq), lambda b, h, i: (b, 0, h)),
                     pl.BlockSpec((1, nl, dv), lambda b, h, i: (b, 0, h))]
        args += [kl, vl]
    out_shape = [jax.ShapeDtypeStruct((bsz, n, kv_heads * groups * dv), BF16)]
    out_specs = [pl.BlockSpec((1, tq, groups * dv), lambda b, h, i: (b, i, h))]
    for w in cast:
        rows, cols = w.shape[1] // steps, w.shape[2]
        assert rows * steps == w.shape[1] and rows % 16 == 0
        in_specs.append(pl.BlockSpec((None, rows, cols), lambda b, h, i: (layer, (b * kv_heads + h) * nq + i, 0)))
        out_specs.append(pl.BlockSpec((rows, cols), lambda b, h, i: ((b * kv_heads + h) * nq + i, 0)))
        out_shape.append(jax.ShapeDtypeStruct(w.shape[1:], BF16))
        args.append(w)
    outs = pl.pallas_call(
        functools.partial(_attn_kernel, groups=groups, dq=dq, dv=dv, tk=tk, rsub=rsub, has_lat=has_lat,
                          n_cast=len(cast)),
        out_shape=out_shape,
        grid=(bsz, kv_heads, nq),
        in_specs=in_specs,
        out_specs=out_specs,
        scratch_shapes=[pltpu.VMEM((m_rows, dq), BF16), pltpu.VMEM((2, m_rows, wmax), F32),
                        pltpu.VMEM((m_rows, wmax), BF16), pltpu.VMEM((m_rows, 128), F32),
                        pltpu.VMEM((m_rows, 128), F32), pltpu.VMEM((m_rows, dv), F32)],
        compiler_params=_cparams(("parallel", "parallel", "arbitrary")),
        name="attention",
    )(*args)
    return outs if cast else outs[0]


def _gla_kernel(q_ref, k_ref, v_ref, g_ref, s0_ref, o_ref, sfin_ref, st_ref, att_ref, cb_ref, kk_ref, *, chunk):
    d = pl.program_id(0)
    i = pl.program_id(2)
    t_rows = q_ref.shape[1]
    nch = t_rows // chunk

    @pl.when(i == 0)
    def _():
        st_ref[...] = s0_ref[0, 0]

    row = lax.broadcasted_iota(jnp.int32, (chunk, chunk), 0)
    col = lax.broadcasted_iota(jnp.int32, (chunk, chunk), 1)
    tri = jnp.where(d == 0, col - row, row - col) <= 0
    tri_bf = jnp.where(tri, 1.0, 0.0).astype(BF16)
    row2 = lax.broadcasted_iota(jnp.int32, (chunk, 2 * chunk), 0)
    lane2 = lax.broadcasted_iota(jnp.int32, (chunk, 2 * chunk), 1)
    col2 = jnp.where(lane2 >= chunk, lane2 - chunk, lane2)
    tri2 = jnp.where(d == 0, col2 - row2, row2 - col2) <= 0
    lo = lax.broadcasted_iota(jnp.int32, (chunk, 128), 1) < C_DK
    vlo = lax.broadcasted_iota(jnp.int32, (chunk, 2 * C_DV), 1) < C_DV
    srow = lax.broadcasted_iota(jnp.int32, (2 * C_DV, 2 * C_DK), 0) // C_DV
    scol = lax.broadcasted_iota(jnp.int32, (2 * C_DV, 2 * C_DK), 1) // C_DK
    same_head = srow == scol
    nt = (((1,), (1,)), ((), ()))
    tn = (((0,), (0,)), ((), ()))

    def intra_scores_mxu(q_p, k_p, cb_p, qe_p):
        ke_p = (k_p * jnp.exp(-cb_p)).astype(BF16)
        zk = jnp.zeros_like(ke_p)
        ke_bd = jnp.concatenate([jnp.where(lo, ke_p, zk), jnp.where(lo, zk, ke_p)], axis=0)
        return lax.dot_general(qe_p, ke_bd, nt, preferred_element_type=F32)

    def intra_scores_guarded(q_p, k_p, cb_p, qe_p):
        cb_ref[...] = cb_p
        kk_ref[...] = k_p
        att_ref[...] = jnp.zeros(att_ref.shape, F32)

        def one_source(s, carry):
            cb_s = cb_ref[pl.ds(s, 1), :]
            w = q_p * jnp.exp(jnp.minimum(cb_p - cb_s, 0.0)) * kk_ref[pl.ds(s, 1), :]
            w0 = jnp.sum(jnp.where(lo, w, 0.0), axis=-1, keepdims=True)
            w1 = jnp.sum(jnp.where(lo, 0.0, w), axis=-1, keepdims=True)
            att_ref[...] += jnp.where(lane2 == s, w0, 0.0) + jnp.where(lane2 == s + chunk, w1, 0.0)
            return carry

        lax.fori_loop(0, chunk, one_source, 0)
        return att_ref[...]

    def scan_block(intra_scores):
        for c in range(nch):
            cc = c + d * (nch - 1 - 2 * c)
            r0 = pl.multiple_of(cc * chunk, chunk)
            g = g_ref[0, pl.ds(r0, chunk), :]
            g_hi = g.astype(BF16)
            g_lo = (g - g_hi.astype(F32)).astype(BF16)
            cb = (jnp.dot(tri_bf, g_hi, preferred_element_type=F32)
                  + jnp.dot(tri_bf, g_lo, preferred_element_type=F32))
            tot = jnp.sum(g, axis=0, keepdims=True)
            q = q_ref[0, pl.ds(r0, chunk), :] * (C_DK ** -0.5)
            k = k_ref[0, pl.ds(r0, chunk), :]
            v = v_ref[0, pl.ds(r0, chunk), :]
            qe = (q * jnp.exp(cb)).astype(BF16)
            kd = (k * jnp.exp(tot - cb)).astype(BF16)
            dec = jnp.exp(tot)
            for p in range(C_HEADS // 2):
                ls = slice(p * 128, (p + 1) * 128)
                vs = slice(p * 2 * C_DV, (p + 1) * 2 * C_DV)
                qe_p, kd_p, v_p = qe[:, ls], kd[:, ls], v[:, vs]
                att = intra_scores(q[:, ls], k[:, ls], cb[:, ls], qe_p)
                att = jnp.where(tri2, att, 0.0).astype(BF16)
                zv = jnp.zeros_like(v_p)
                v_bd = jnp.concatenate([jnp.where(vlo, v_p, zv), jnp.where(vlo, zv, v_p)], axis=0)
                st = st_ref[p]
                o = (jnp.dot(att, v_bd, preferred_element_type=F32)
                     + lax.dot_general(qe_p, st.astype(BF16), nt, preferred_element_type=F32))
                o_ref[0, 0, pl.ds(r0, chunk), vs] = o
                u = lax.dot_general(v_p, kd_p, tn, preferred_element_type=F32)
                st_ref[p] = dec[:, ls] * st + jnp.where(same_head, u, 0.0)

    worst = functools.reduce(jnp.maximum, [
        jnp.max(-jnp.sum(g_ref[0, c * chunk:(c + 1) * chunk, :], axis=0, keepdims=True)) for c in range(nch)])
    safe = worst <= GLA_SAFE_DECAY

    @pl.when(safe)
    def _():
        scan_block(intra_scores_mxu)

    @pl.when(jnp.logical_not(safe))
    def _():
        scan_block(intra_scores_guarded)

    @pl.when(i == pl.num_programs(2) - 1)
    def _():
        sfin_ref[0, 0] = st_ref[...]


def _gla(cq, ck, cv, gdec, s0):
    bsz, n, _ = cq.shape
    t_rows = min(512, n)
    nb = n // t_rows
    rb = lambda d, i: i + d * (nb - 1 - 2 * i)
    o, sfin = pl.pallas_call(
        functools.partial(_gla_kernel, chunk=GLA_CHUNK),
        out_shape=[jax.ShapeDtypeStruct((2, bsz, n, C_HEADS * C_DV), F32),
                   jax.ShapeDtypeStruct(s0.shape, F32)],
        grid=(2, bsz, nb),
        in_specs=[pl.BlockSpec((1, t_rows, 256), lambda d, b, i: (b, rb(d, i), 0)),
                  pl.BlockSpec((1, t_rows, 256), lambda d, b, i: (b, rb(d, i), 0)),
                  pl.BlockSpec((1, t_rows, 512), lambda d, b, i: (b, rb(d, i), 0)),
                  pl.BlockSpec((1, t_rows, 256), lambda d, b, i: (b, rb(d, i), d)),
                  pl.BlockSpec((1, 1, 2, 256, 128), lambda d, b, i: (d, b, 0, 0, 0))],
        out_specs=[pl.BlockSpec((1, 1, t_rows, 512), lambda d, b, i: (d, b, rb(d, i), 0)),
                   pl.BlockSpec((1, 1, 2, 256, 128), lambda d, b, i: (d, b, 0, 0, 0))],
        scratch_shapes=[pltpu.VMEM((2, 256, 128), F32), pltpu.VMEM((GLA_CHUNK, 2 * GLA_CHUNK), F32),
                        pltpu.VMEM((GLA_CHUNK, 128), F32), pltpu.VMEM((GLA_CHUNK, 128), F32)],
        compiler_params=_cparams(("parallel", "parallel", "arbitrary")),
        name="gla",
    )(cq, ck, cv, gdec, s0)
    return o, sfin


def _outproj_kernel(oa_ref, ob_ref, oc_ref, cg_ref, gn_ref, w_ref, x_ref, mod_ref, o_ref, z_ref):
    z_ref[:, 0:1024] = oa_ref[0]
    z_ref[:, 1024:1536] = ob_ref[0]
    for h in range(C_HEADS):
        hs = slice(h * C_DV, (h + 1) * C_DV)
        oc = oc_ref[0, 0, :, hs] + oc_ref[1, 0, :, hs]
        z_ref[:, 1536 + h * C_DV:1536 + (h + 1) * C_DV] = (
            _rms(oc, gn_ref[...]) * _silu(cg_ref[0, :, hs])).astype(BF16)
    y = jnp.dot(z_ref[...], w_ref[...], preferred_element_type=F32)
    o_ref[0] = x_ref[0] + mod_ref[0, 2:3, :] * y


def _outproj(oa, ob, oc, cg, gn, w, x, mod, layer, ctx):
    bsz, n, d = x.shape
    tm = min(512, n)
    return pl.pallas_call(
        _outproj_kernel,
        out_shape=jax.ShapeDtypeStruct((bsz, n, d), F32),
        grid=(bsz, n // tm),
        in_specs=[pl.BlockSpec((1, tm, 1024), lambda b, i: (b, i, 0)),
                  pl.BlockSpec((1, tm, 512), lambda b, i: (b, i, 0)),
                  pl.BlockSpec((2, 1, tm, 512), lambda b, i: (0, b, i, 0)),
                  pl.BlockSpec((1, tm, 512), lambda b, i: (b, i, 0)),
                  pl.BlockSpec((None, 1, C_DV), lambda b, i: (layer, 0, 0)),
                  pl.BlockSpec(w.shape, lambda b, i: (0, 0), pipeline_mode=pl.Buffered(1)),
                  pl.BlockSpec((1, tm, d), lambda b, i: (b, i, 0)),
                  pl.BlockSpec((1, 6, d), _mod_index(ctx))],
        out_specs=pl.BlockSpec((1, tm, d), lambda b, i: (b, i, 0)),
        scratch_shapes=[pltpu.VMEM((tm, w.shape[0]), BF16)],
        compiler_params=_cparams(("parallel", "parallel")),
        name="outproj",
    )(oa, ob, oc, cg, gn, w, x, mod)


def _ffn_kernel(x_ref, mod_ref, g_ref, wg_ref, wu_ref, wd_ref, o_ref, h_ref):
    j = pl.program_id(2)

    @pl.when(j == 0)
    def _():
        y = _rms(x_ref[0], g_ref[...])
        h_ref[...] = (y * (1.0 + mod_ref[0, 4:5, :]) + mod_ref[0, 3:4, :]).astype(BF16)
        o_ref[0] = jnp.zeros(o_ref.shape[1:], F32)

    h = h_ref[...]
    a = jnp.dot(h, wg_ref[...], preferred_element_type=F32)
    u = jnp.dot(h, wu_ref[...], preferred_element_type=F32)
    t = (_silu(a) * u).astype(BF16)
    o_ref[0] += jnp.dot(t, wd_ref[...], preferred_element_type=F32)

    @pl.when(j == pl.num_programs(2) - 1)
    def _():
        o_ref[0] = x_ref[0] + mod_ref[0, 5:6, :] * o_ref[0]


def _ffn(x, mod, g, wg, wu, wd, layer, ctx):
    bsz, n, d = x.shape
    f = wg.shape[1]
    tm = min(512, n)
    tf = 512
    return pl.pallas_call(
        _ffn_kernel,
        out_shape=jax.ShapeDtypeStruct((bsz, n, d), F32),
        grid=(bsz, n // tm, f // tf),
        in_specs=[pl.BlockSpec((1, tm, d), lambda b, i, j: (b, i, 0)),
                  pl.BlockSpec((1, 6, d), _mod_index(ctx)),
                  pl.BlockSpec((None, 1, d), lambda b, i, j: (layer, 0, 0)),
                  pl.BlockSpec((d, tf), lambda b, i, j: (0, j)),
                  pl.BlockSpec((d, tf), lambda b, i, j: (0, j)),
                  pl.BlockSpec((tf, d), lambda b, i, j: (j, 0))],
        out_specs=pl.BlockSpec((1, tm, d), lambda b, i, j: (b, i, 0)),
        scratch_shapes=[pltpu.VMEM((tm, d), BF16)],
        compiler_params=_cparams(("parallel", "parallel", "arbitrary")),
        name="ffn",
    )(x, mod, g, wg, wu, wd)


def _rope_tables(n):
    pos = np.arange(n)
    row = (pos // GRID_W).astype(np.float64)[:, None]
    col = (pos % GRID_W).astype(np.float64)[:, None]

    def table(half):
        inv = ROPE_THETA ** (-np.arange(half, dtype=np.float64) / half)
        ar, ac = row * inv[None, :], col * inv[None, :]
        cos = np.concatenate([np.cos(ar), np.cos(ar), np.cos(ac), np.cos(ac)], axis=1)
        sin = np.concatenate([-np.sin(ar), np.sin(ar), -np.sin(ac), np.sin(ac)], axis=1)
        return cos, sin

    cos_a, sin_a = table(A_HEAD_DIM // 4)
    cos_b, sin_b = table(B_ROPE // 4)
    cos_b, sin_b = np.tile(cos_b, (1, 2)), np.tile(sin_b, (1, 2))
    return tuple(jnp.asarray(t, F32) for t in (cos_a, sin_a, cos_b, sin_b))


def _prepare_weights(w_in, w_uq, w_ukv, w_gk_f, b_gk_f, w_gk_b, b_gk_b, smalls):
    depth, d, _ = w_in.shape
    w_r = jnp.concatenate([
        w_in[:, :, 0:2304],
        w_in[:, :, 2304:2368], w_in[:, :, 2304:2368],
        w_in[:, :, 3904:3936], jnp.zeros((depth, d, 96), w_in.dtype),
        w_in[:, :, 2368:3904],
    ], axis=2).astype(BF16)
    uq = w_uq.reshape(depth, B_Q_RANK, B_HEADS, B_NOPE + B_ROPE)
    uq = jnp.concatenate([uq[..., :B_NOPE].reshape(depth, B_Q_RANK, -1),
                          uq[..., B_NOPE:].reshape(depth, B_Q_RANK, -1)], axis=2)
    ukv = w_ukv.reshape(depth, B_KV_RANK, B_HEADS, B_NOPE + B_V)
    ukv = jnp.concatenate([ukv[..., :B_NOPE].reshape(depth, B_KV_RANK, -1),
                           ukv[..., B_NOPE:].reshape(depth, B_KV_RANK, -1)], axis=2)
    nk = C_HEADS * C_DK
    w_gk = jnp.zeros((depth, 128, 2 * nk), F32)
    w_gk = w_gk.at[:, 0:C_GATE_RANK, 0:nk].set(w_gk_f).at[:, C_GATE_RANK:2 * C_GATE_RANK, nk:].set(w_gk_b)
    lw = {k: v.reshape(depth, 1, -1) for k, v in smalls.items()}
    lw["b_q_rope_norm2"] = jnp.tile(lw["b_q_rope_norm"], (1, 1, 2))
    lw["b_k_rope_norm2"] = jnp.tile(lw["b_k_rope_norm"], (1, 1, 2))
    lw.update(w_in=w_r, w_uq=uq.astype(BF16), w_ukv=ukv.astype(BF16), w_gk=w_gk,
              b_gk=jnp.concatenate([b_gk_f, b_gk_b], axis=1).reshape(depth, 1, -1))
    return lw


def kernel(x, c, ctx, c_ctx, w_mod, b_mod, norm1_g, norm2_g, w_in, a_q_norm, a_k_norm, b_q_lora_norm, b_kv_lora_norm, w_uq, w_ukv, b_q_nope_norm, b_k_nope_norm, b_q_rope_norm, b_k_rope_norm, w_gk_f, b_gk_f, w_gk_b, b_gk_b, c_out_norm, w_out, w_gate, w_up, w_down):
    bsz, n, d = x.shape
    depth = w_in.shape[0]
    tabs = _rope_tables(n)
    assert bsz + 1 <= 8, "the modulation kernel holds the context row and all batch rows in one 8-row tile"
    cpad = jnp.concatenate([c_ctx[None, :], c, jnp.zeros((8 - bsz - 1, d), F32)], axis=0)
    smalls = dict(a_q_norm=a_q_norm, a_k_norm=a_k_norm, b_q_lora_norm=b_q_lora_norm,
                  b_kv_lora_norm=b_kv_lora_norm, b_q_nope_norm=b_q_nope_norm, b_k_nope_norm=b_k_nope_norm,
                  b_q_rope_norm=b_q_rope_norm, b_k_rope_norm=b_k_rope_norm, c_out_norm=c_out_norm,
                  norm1_g=norm1_g, norm2_g=norm2_g)
    lw = _prepare_weights(w_in, w_uq, w_ukv, w_gk_f, b_gk_f, w_gk_b, b_gk_b, smalls)
    d_ff = w_gate.shape[2]
    ffn_out_w = (w_gate, w_up, w_down.reshape(depth, d, d_ff), w_out)
    b_mod3 = b_mod.reshape(depth, 1, -1)
    s_zero = jnp.zeros((2, bsz, 2, 2 * C_DV, 2 * C_DK), F32)
    a_args = dict(kv_heads=A_KV_HEADS, groups=A_HEADS // A_KV_HEADS, dq=A_HEAD_DIM, dv=A_HEAD_DIM)
    b_args = dict(kv_heads=B_HEADS, groups=1, dq=B_QK_PAD, dv=B_V)
    xc = ctx
    for l in range(depth):
        ctx_out = l < depth - 1
        mod = _modulation(cpad, w_mod, b_mod3, l)
        caq, cak, cav, cbq, cbk, cbv, cgd, ccq, cck, ccv, ccg = _proj(xc, mod, lw["norm1_g"], None, lw, l, True)
        laq, lak, lav, lbq, lbk, lbv, lgd, lcq, lck, lcv, lcg = _proj(x, mod, lw["norm1_g"], tabs, lw, l, False)

        o_a, wg, wu, wd, wo = _attention(laq, cak, cav, lak, lav, tq=512, cast=ffn_out_w, layer=l, **a_args)
        wd = wd.reshape(d_ff, d)
        o_b = _attention(lbq, cbk, cbv, lbk, lbv, tq=1024, **b_args)
        oc_c, s_ctx = _gla(ccq, cck, ccv, cgd, s_zero)
        o_c, _ = _gla(lcq, lck, lcv, lgd, s_ctx)

        x = _outproj(o_a, o_b, o_c, lcg, lw["c_out_norm"], wo, x, mod, l, False)
        x = _ffn(x, mod, lw["norm2_g"], wg, wu, wd, l, False)
        if ctx_out:
            oc_a = _attention(caq, cak, cav, None, None, tq=256, **a_args)
            oc_b = _attention(cbq, cbk, cbv, None, None, tq=256, **b_args)
            xc = _outproj(oc_a, oc_b, oc_c, ccg, lw["c_out_norm"], wo, xc, mod, l, True)
            xc = _ffn(xc, mod, lw["norm2_g"], wg, wu, wd, l, True)
    return x
```

```python
import functools

import numpy as np
import jax
import jax.numpy as jnp
from jax import lax
from jax.experimental import pallas as pl
from jax.experimental.pallas import tpu as pltpu

F32 = jnp.float32
BF16 = jnp.bfloat16

D_MODEL = 2048
GRID_W = 64
ROPE_THETA = 10000.0
NORM_EPS = 1e-6

A_HEADS = 8
A_KV_HEADS = 2
A_HEAD_DIM = 128
B_HEADS = 4
B_Q_RANK = 512
B_KV_RANK = 256
B_NOPE = 128
B_ROPE = 64
B_V = 128
B_QK_PAD = 256
C_HEADS = 4
C_DK = 64
C_DV = 128
C_GATE_RANK = 16
C_GATE_NORM = 16.0

D_FF = 5632
P_COLS = 4096

COL_AQ = 0
COL_AK = 1024
COL_BCQ = 1536
COL_BCKV = 2048
COL_CQ = 2560
COL_CV = 3072
COL_CG = 3584

GLA_CHUNK = 128
GLA_SAFE_DECAY = 60.0

VMEM_LIMIT = 56 * 2**20
LOG2E = 1.4426950408889634


def _cparams(sem, vmem=VMEM_LIMIT):
    return pltpu.CompilerParams(dimension_semantics=sem, vmem_limit_bytes=vmem)


def _silu(x):
    return x / (1.0 + jnp.exp(-x))


def _rms(x, g):
    ms = jnp.mean(x * x, axis=-1, keepdims=True)
    return x * lax.rsqrt(ms + NORM_EPS) * g


def _mod_kernel(c_ref, w_ref, b_ref, o_ref):
    s = _silu(c_ref[...]).astype(BF16)
    o_ref[...] = jnp.dot(s, w_ref[...].astype(BF16), preferred_element_type=F32) + b_ref[...]


def _modulation(cpad, w_mod, b_mod, layer):
    d = cpad.shape[1]
    n = w_mod.shape[2]
    tn = 1024
    out = pl.pallas_call(
        _mod_kernel,
        out_shape=jax.ShapeDtypeStruct((8, n), F32),
        grid=(n // tn,),
        in_specs=[pl.BlockSpec((8, d), lambda j: (0, 0)),
                  pl.BlockSpec((None, d, tn), lambda j: (layer, 0, j)),
                  pl.BlockSpec((None, 1, tn), lambda j: (layer, 0, j))],
        out_specs=pl.BlockSpec((8, tn), lambda j: (0, j)),
        compiler_params=_cparams(("arbitrary",)),
        name="modulation",
    )(cpad, w_mod, b_mod)
    return out.reshape(8, 6, d)


def _mod_index(ctx):
    if ctx:
        return lambda b, i, *_: (0, 0, 0)
    return lambda b, i, *_: (b + 1, 0, 0)


def _swap_halves(x, lane, width):
    return jnp.where((lane // width) % 2 == 0,
                     pltpu.roll(x, 128 - width, 1), pltpu.roll(x, width, 1))


def _proj_kernel(*refs, rope):
    x_ref, mod_ref, g1_ref, w_ref = refs[:4]
    refs = refs[4:]
    if rope:
        cosa_ref, sina_ref, cosb_ref, sinb_ref = refs[:4]
        refs = refs[4:]
    (gaq_ref, gak_ref, gql_ref, gkvl_ref, gqn_ref, gkn_ref, gqr_ref, gkr_ref,
     wuq_ref, wukv_ref, wgk_ref, bgk_ref,
     aq_ref, ak_ref, av_ref, bq_ref, bk_ref, bv_ref, gd_ref, cq_ref, ck_ref, cv_ref, cg_ref) = refs
    tm = x_ref.shape[1]
    lane = lax.broadcasted_iota(jnp.int32, (tm, 128), 1)
    lo = lane < 64

    y = _rms(x_ref[0], g1_ref[...])
    h = (y * (1.0 + mod_ref[0, 1:2, :]) + mod_ref[0, 0:1, :]).astype(BF16)

    def proj(c0, width):
        return jnp.dot(h, w_ref[:, c0:c0 + width], preferred_element_type=F32)

    def rope_a(x):
        if not rope:
            return x
        return x * cosa_ref[...] + _swap_halves(x, lane, 32) * sina_ref[...]

    def rope_b(x):
        if not rope:
            return x
        return x * cosb_ref[...] + _swap_halves(x, lane, 16) * sinb_ref[...]

    a_scale = LOG2E * A_HEAD_DIM ** -0.5
    for half in range(2):
        pq = proj(COL_AQ + half * 512, 512)
        for j in range(4):
            hh = half * 4 + j
            x = pq[:, j * 128:(j + 1) * 128]
            aq_ref[0, :, hh * 128:(hh + 1) * 128] = (rope_a(_rms(x, gaq_ref[...])) * a_scale).astype(BF16)
    pkv = proj(COL_AK, 512)
    for hh in range(A_KV_HEADS):
        x = pkv[:, hh * 128:(hh + 1) * 128]
        ak_ref[0, :, hh * 128:(hh + 1) * 128] = rope_a(_rms(x, gak_ref[...])).astype(BF16)
    av_ref[0] = pkv[:, 256:512].astype(BF16)

    b_scale = LOG2E * (B_NOPE + B_ROPE) ** -0.5
    cq = _rms(proj(COL_BCQ, 512), gql_ref[...]).astype(BF16)
    qb = jnp.dot(cq, wuq_ref[...], preferred_element_type=F32)
    for hh in range(B_HEADS):
        x = qb[:, hh * 128:(hh + 1) * 128]
        bq_ref[0, :, hh * B_QK_PAD:hh * B_QK_PAD + 128] = (_rms(x, gqn_ref[...]) * b_scale).astype(BF16)
    for p in range(B_HEADS // 2):
        r = qb[:, 512 + p * 128:512 + (p + 1) * 128]
        sq = r * r
        ms_lo = jnp.sum(jnp.where(lo, sq, 0.0), axis=-1, keepdims=True)
        ms_hi = jnp.sum(jnp.where(lo, 0.0, sq), axis=-1, keepdims=True)
        ms = jnp.where(lo, ms_lo, ms_hi) * (1.0 / B_ROPE)
        rn = rope_b(r * lax.rsqrt(ms + NORM_EPS) * gqr_ref[...]) * b_scale
        h0 = 2 * p
        bq_ref[0, :, h0 * B_QK_PAD + 128:(h0 + 1) * B_QK_PAD] = jnp.where(lo, rn, 0.0).astype(BF16)
        bq_ref[0, :, (h0 + 1) * B_QK_PAD + 128:(h0 + 2) * B_QK_PAD] = jnp.where(lo, 0.0, rn).astype(BF16)

    pmix = proj(COL_BCKV, 512)
    ckv = _rms(pmix[:, 0:256], gkvl_ref[...]).astype(BF16)
    kv = jnp.dot(ckv, wukv_ref[...], preferred_element_type=F32)
    krn = rope_b(_rms(pmix[:, 256:384], gkr_ref[...]))
    kr_even = jnp.where(lo, krn, 0.0).astype(BF16)
    kr_odd = jnp.where(lo, 0.0, krn).astype(BF16)
    for hh in range(B_HEADS):
        x = kv[:, hh * 128:(hh + 1) * 128]
        bk_ref[0, :, hh * B_QK_PAD:hh * B_QK_PAD + 128] = _rms(x, gkn_ref[...]).astype(BF16)
        bk_ref[0, :, hh * B_QK_PAD + 128:(hh + 1) * B_QK_PAD] = kr_even if hh % 2 == 0 else kr_odd
    bv_ref[0] = kv[:, 512:1024].astype(BF16)

    z = jnp.dot(pmix[:, 384:512], wgk_ref[...], preferred_element_type=F32,
                precision=lax.Precision.HIGHEST) + bgk_ref[...]
    gd_ref[0] = (jnp.minimum(z, 0.0) - jnp.log(1.0 + jnp.exp(-jnp.abs(z)))) * (1.0 / C_GATE_NORM)
    pqk = proj(COL_CQ, 512)
    cq_ref[0] = pqk[:, 0:256]
    ck_ref[0] = pqk[:, 256:512]
    cv_ref[0] = proj(COL_CV, 512).astype(BF16)
    cg_ref[0] = proj(COL_CG, 512)


def _proj(x, mod, g1, tabs, lw, layer, ctx):
    bsz, n, d = x.shape
    tm = min(512, n)
    rope = not ctx
    row = lambda b, i: (b, i, 0)
    once = pl.Buffered(1)
    in_specs = [pl.BlockSpec((1, tm, d), row),
                pl.BlockSpec((1, 6, d), _mod_index(ctx)),
                pl.BlockSpec((None, 1, d), lambda b, i: (layer, 0, 0)),
                pl.BlockSpec((None, d, P_COLS), lambda b, i: (layer, 0, 0), pipeline_mode=once)]
    args = [x, mod, g1, lw["w_in"]]
    if rope:
        in_specs += [pl.BlockSpec((tm, 128), lambda b, i: (i, 0))] * 4
        args += list(tabs)
    small = [lw["a_q_norm"], lw["a_k_norm"], lw["b_q_lora_norm"], lw["b_kv_lora_norm"],
             lw["b_q_nope_norm"], lw["b_k_nope_norm"], lw["b_q_rope_norm2"], lw["b_k_rope_norm2"],
             lw["w_uq"], lw["w_ukv"], lw["w_gk"], lw["b_gk"]]
    in_specs += [pl.BlockSpec((None,) + a.shape[1:], lambda b, i: (layer, 0, 0), pipeline_mode=once) for a in small]
    args += small
    out_shape = [jax.ShapeDtypeStruct((bsz, n, 1024), BF16),
                 jax.ShapeDtypeStruct((bsz, n, 256), BF16),
                 jax.ShapeDtypeStruct((bsz, n, 256), BF16),
                 jax.ShapeDtypeStruct((bsz, n, B_HEADS * B_QK_PAD), BF16),
                 jax.ShapeDtypeStruct((bsz, n, B_HEADS * B_QK_PAD), BF16),
                 jax.ShapeDtypeStruct((bsz, n, B_HEADS * B_V), BF16),
                 jax.ShapeDtypeStruct((bsz, n, 512), F32),
                 jax.ShapeDtypeStruct((bsz, n, 256), F32),
                 jax.ShapeDtypeStruct((bsz, n, 256), F32),
                 jax.ShapeDtypeStruct((bsz, n, 512), BF16),
                 jax.ShapeDtypeStruct((bsz, n, 512), F32)]
    out_specs = [pl.BlockSpec((1, tm, s.shape[2]), row) for s in out_shape]
    return pl.pallas_call(
        functools.partial(_proj_kernel, rope=rope),
        out_shape=out_shape,
        grid=(bsz, n // tm),
        in_specs=in_specs,
        out_specs=out_specs,
        compiler_params=_cparams(("parallel", "parallel")),
        name="proj_ctx" if ctx else "proj_lat",
    )(*args)


def _attn_kernel(*refs, groups, dq, dv, tk, rsub, has_lat, n_cast):
    if has_lat:
        q_ref, kc_ref, vc_ref, kl_ref, vl_ref = refs[:5]
        refs = refs[5:]
    else:
        q_ref, kc_ref, vc_ref = refs[:3]
        refs = refs[3:]
    cast_in, refs = refs[:n_cast], refs[n_cast:]
    o_ref, cast_out = refs[0], refs[1:1 + n_cast]
    q_scr, s_scr, p_scr, m_scr, l_scr, acc_scr = refs[1 + n_cast:]
    for src_ref, dst_ref in zip(cast_in, cast_out):
        dst_ref[...] = src_ref[...].astype(BF16)
    tq = q_ref.shape[1]
    m_rows = groups * tq
    nt = (((1,), (1,)), ((), ()))
    for g in range(groups):
        q_scr[g * tq:(g + 1) * tq, :] = q_ref[0, :, g * dq:(g + 1) * dq]
    m_scr[...] = jnp.full(m_scr.shape, -jnp.inf, F32)
    l_scr[...] = jnp.zeros(l_scr.shape, F32)
    acc_scr[...] = jnp.zeros(acc_scr.shape, F32)

    def scores(slot, k):
        s_scr[slot, :, 0:k.shape[0]] = lax.dot_general(q_scr[...], k, nt, preferred_element_type=F32)

    def softmax_pv(slot, v):
        w = v.shape[0]
        for r in range(m_rows // rsub):
            rs = slice(r * rsub, (r + 1) * rsub)
            cols = [s_scr[slot, rs, j * 128:(j + 1) * 128] for j in range(w // 128)]
            mx = functools.reduce(jnp.maximum, cols)
            m_old = m_scr[rs, :]
            m_new = jnp.maximum(m_old, jnp.max(mx, axis=-1, keepdims=True))
            ps = [jnp.exp2(c - m_new) for c in cols]
            a = jnp.exp2(m_old - m_new)
            l_scr[rs, :] = a * l_scr[rs, :] + functools.reduce(jnp.add, ps)
            m_scr[rs, :] = m_new
            acc_scr[rs, :] = a * acc_scr[rs, :]
            for j, p in enumerate(ps):
                p_scr[rs, j * 128:(j + 1) * 128] = p.astype(BF16)
        acc_scr[...] += jnp.dot(p_scr[:, 0:w], v, preferred_element_type=F32)

    chunks = []
    if has_lat:
        chunks += [(kl_ref, vl_ref, slice(c * tk, (c + 1) * tk)) for c in range(kl_ref.shape[1] // tk)]
    chunks.append((kc_ref, vc_ref, slice(0, kc_ref.shape[1])))
    scores(0, chunks[0][0][0, chunks[0][2], :])
    for n, (_, v_ref, rows) in enumerate(chunks):
        if n + 1 < len(chunks):
            k_next, _, rows_next = chunks[n + 1]
            scores((n + 1) % 2, k_next[0, rows_next, :])
        softmax_pv(n % 2, v_ref[0, rows, :])
    for g in range(groups):
        gs = slice(g * tq, (g + 1) * tq)
        l = jnp.sum(l_scr[gs, :], axis=-1, keepdims=True)
        o_ref[0, :, g * dv:(g + 1) * dv] = (acc_scr[gs, :] * (1.0 / l)).astype(o_ref.dtype)


def _attention(q, kc, vc, kl, vl, *, kv_heads, groups, dq, dv, tq, tk=1024, rsub=64, cast=(), layer=0):
    assert dv == 128, "the accumulator rescale reuses the 128-lane replicated running max"
    bsz, n, _ = q.shape
    nc = kc.shape[1]
    has_lat = kl is not None
    tq = min(tq, n)
    m_rows = groups * tq
    nq = n // tq
    steps = bsz * kv_heads * nq
    wmax = max(nc, tk) if has_lat else nc
    in_specs = [pl.BlockSpec((1, tq, groups * dq), lambda b, h, i: (b, i, h)),
                pl.BlockSpec((1, nc, dq), lambda b, h, i: (b, 0, h)),
                pl.BlockSpec((1, nc, dv), lambda b, h, i: (b, 0, h))]
    args = [q, kc, vc]
    if has_lat:
        nl = kl.shape[1]
        in_specs += [pl.BlockSpec((1, nl, dq), lambda b, h, i: (b, 0, h)),
                     pl.BlockSpec((1, nl, dv), lambda b, h, i: (b, 0, h))]
        args += [kl, vl]
    out_shape = [jax.ShapeDtypeStruct((bsz, n, kv_heads * groups * dv), BF16)]
    out_specs = [pl.BlockSpec((1, tq, groups * dv), lambda b, h, i: (b, i, h))]
    def slab_specs(w):
        share = next(k for k in (1, 2, 4, 8) if w.shape[1] * k % (steps * 16) == 0)
        rows, cols = w.shape[1] * share // steps, w.shape[2]
        slab = lambda b, h, i: ((b * kv_heads + h) * nq + i) // share
        return (pl.BlockSpec((None, rows, cols), lambda b, h, i: (layer, slab(b, h, i), 0)),
                pl.BlockSpec((rows, cols), lambda b, h, i: (slab(b, h, i), 0)))

    for w in cast:
        spec_in, spec_out = slab_specs(w)
        in_specs.append(spec_in)
        out_specs.append(spec_out)
        out_shape.append(jax.ShapeDtypeStruct(w.shape[1:], BF16))
        args.append(w)
    outs = pl.pallas_call(
        functools.partial(_attn_kernel, groups=groups, dq=dq, dv=dv, tk=tk, rsub=rsub, has_lat=has_lat,
                          n_cast=len(cast)),
        out_shape=out_shape,
        grid=(bsz, kv_heads, nq),
        in_specs=in_specs,
        out_specs=out_specs,
        scratch_shapes=[pltpu.VMEM((m_rows, dq), BF16), pltpu.VMEM((2, m_rows, wmax), F32),
                        pltpu.VMEM((m_rows, wmax), BF16), pltpu.VMEM((m_rows, 128), F32),
                        pltpu.VMEM((m_rows, 128), F32), pltpu.VMEM((m_rows, dv), F32)],
        compiler_params=_cparams(("parallel", "parallel", "arbitrary")),
        name="attention",
    )(*args)
    return outs if cast else outs[0]


def _gla_kernel(q_ref, k_ref, v_ref, g_ref, s0_ref, o_ref, sfin_ref, st_ref, att_ref, cb_ref, kk_ref, *, chunk):
    d = pl.program_id(0)
    i = pl.program_id(2)
    t_rows = q_ref.shape[1]
    nch = t_rows // chunk

    @pl.when(i == 0)
    def _():
        st_ref[...] = s0_ref[0, 0]

    row = lax.broadcasted_iota(jnp.int32, (chunk, chunk), 0)
    col = lax.broadcasted_iota(jnp.int32, (chunk, chunk), 1)
    tri = jnp.where(d == 0, col - row, row - col) <= 0
    tri_bf = jnp.where(tri, 1.0, 0.0).astype(BF16)
    row2 = lax.broadcasted_iota(jnp.int32, (chunk, 2 * chunk), 0)
    lane2 = lax.broadcasted_iota(jnp.int32, (chunk, 2 * chunk), 1)
    col2 = jnp.where(lane2 >= chunk, lane2 - chunk, lane2)
    tri2 = jnp.where(d == 0, col2 - row2, row2 - col2) <= 0
    lo = lax.broadcasted_iota(jnp.int32, (chunk, 128), 1) < C_DK
    vlo = lax.broadcasted_iota(jnp.int32, (chunk, 2 * C_DV), 1) < C_DV
    srow = lax.broadcasted_iota(jnp.int32, (2 * C_DV, 2 * C_DK), 0) // C_DV
    scol = lax.broadcasted_iota(jnp.int32, (2 * C_DV, 2 * C_DK), 1) // C_DK
    same_head = srow == scol
    nt = (((1,), (1,)), ((), ()))
    tn = (((0,), (0,)), ((), ()))

    def intra_scores_mxu(q_p, k_p, cb_p, qe_p):
        ke_p = (k_p * jnp.exp(-cb_p)).astype(BF16)
        zk = jnp.zeros_like(ke_p)
        ke_bd = jnp.concatenate([jnp.where(lo, ke_p, zk), jnp.where(lo, zk, ke_p)], axis=0)
        return lax.dot_general(qe_p, ke_bd, nt, preferred_element_type=F32)

    def intra_scores_guarded(q_p, k_p, cb_p, qe_p):
        cb_ref[...] = cb_p
        kk_ref[...] = k_p
        att_ref[...] = jnp.zeros(att_ref.shape, F32)

        def one_source(s, carry):
            cb_s = cb_ref[pl.ds(s, 1), :]
            w = q_p * jnp.exp(jnp.minimum(cb_p - cb_s, 0.0)) * kk_ref[pl.ds(s, 1), :]
            w0 = jnp.sum(jnp.where(lo, w, 0.0), axis=-1, keepdims=True)
            w1 = jnp.sum(jnp.where(lo, 0.0, w), axis=-1, keepdims=True)
            att_ref[...] += jnp.where(lane2 == s, w0, 0.0) + jnp.where(lane2 == s + chunk, w1, 0.0)
            return carry

        lax.fori_loop(0, chunk, one_source, 0)
        return att_ref[...]

    def scan_block(intra_scores):
        for c in range(nch):
            cc = c + d * (nch - 1 - 2 * c)
            r0 = pl.multiple_of(cc * chunk, chunk)
            g = g_ref[0, pl.ds(r0, chunk), :]
            g_hi = g.astype(BF16)
            g_lo = (g - g_hi.astype(F32)).astype(BF16)
            cb = (jnp.dot(tri_bf, g_hi, preferred_element_type=F32)
                  + jnp.dot(tri_bf, g_lo, preferred_element_type=F32))
            tot = jnp.sum(g, axis=0, keepdims=True)
            q = q_ref[0, pl.ds(r0, chunk), :] * (C_DK ** -0.5)
            k = k_ref[0, pl.ds(r0, chunk), :]
            v = v_ref[0, pl.ds(r0, chunk), :]
            qe = (q * jnp.exp(cb)).astype(BF16)
            kd = (k * jnp.exp(tot - cb)).astype(BF16)
            dec = jnp.exp(tot)
            for p in range(C_HEADS // 2):
                ls = slice(p * 128, (p + 1) * 128)
                vs = slice(p * 2 * C_DV, (p + 1) * 2 * C_DV)
                qe_p, kd_p, v_p = qe[:, ls], kd[:, ls], v[:, vs]
                att = intra_scores(q[:, ls], k[:, ls], cb[:, ls], qe_p)
                att = jnp.where(tri2, att, 0.0).astype(BF16)
                zv = jnp.zeros_like(v_p)
                v_bd = jnp.concatenate([jnp.where(vlo, v_p, zv), jnp.where(vlo, zv, v_p)], axis=0)
                st = st_ref[p]
                o = (jnp.dot(att, v_bd, preferred_element_type=F32)
                     + lax.dot_general(qe_p, st.astype(BF16), nt, preferred_element_type=F32))
                o_ref[0, 0, pl.ds(r0, chunk), vs] = o
                u = lax.dot_general(v_p, kd_p, tn, preferred_element_type=F32)
                st_ref[p] = dec[:, ls] * st + jnp.where(same_head, u, 0.0)

    worst = functools.reduce(jnp.maximum, [
        jnp.max(-jnp.sum(g_ref[0, c * chunk:(c + 1) * chunk, :], axis=0, keepdims=True)) for c in range(nch)])
    safe = worst <= GLA_SAFE_DECAY

    @pl.when(safe)
    def _():
        scan_block(intra_scores_mxu)

    @pl.when(jnp.logical_not(safe))
    def _():
        scan_block(intra_scores_guarded)

    @pl.when(i == pl.num_programs(2) - 1)
    def _():
        sfin_ref[0, 0] = st_ref[...]


def _gla(cq, ck, cv, gdec, s0):
    bsz, n, _ = cq.shape
    t_rows = min(512, n)
    nb = n // t_rows
    rb = lambda d, i: i + d * (nb - 1 - 2 * i)
    o, sfin = pl.pallas_call(
        functools.partial(_gla_kernel, chunk=GLA_CHUNK),
        out_shape=[jax.ShapeDtypeStruct((2, bsz, n, C_HEADS * C_DV), F32),
                   jax.ShapeDtypeStruct(s0.shape, F32)],
        grid=(2, bsz, nb),
        in_specs=[pl.BlockSpec((1, t_rows, 256), lambda d, b, i: (b, rb(d, i), 0)),
                  pl.BlockSpec((1, t_rows, 256), lambda d, b, i: (b, rb(d, i), 0)),
                  pl.BlockSpec((1, t_rows, 512), lambda d, b, i: (b, rb(d, i), 0)),
                  pl.BlockSpec((1, t_rows, 256), lambda d, b, i: (b, rb(d, i), d)),
                  pl.BlockSpec((1, 1, 2, 256, 128), lambda d, b, i: (d, b, 0, 0, 0))],
        out_specs=[pl.BlockSpec((1, 1, t_rows, 512), lambda d, b, i: (d, b, rb(d, i), 0)),
                   pl.BlockSpec((1, 1, 2, 256, 128), lambda d, b, i: (d, b, 0, 0, 0))],
        scratch_shapes=[pltpu.VMEM((2, 256, 128), F32), pltpu.VMEM((GLA_CHUNK, 2 * GLA_CHUNK), F32),
                        pltpu.VMEM((GLA_CHUNK, 128), F32), pltpu.VMEM((GLA_CHUNK, 128), F32)],
        compiler_params=_cparams(("parallel", "parallel", "arbitrary")),
        name="gla",
    )(cq, ck, cv, gdec, s0)
    return o, sfin


def _outproj_kernel(oa_ref, ob_ref, oc_ref, cg_ref, gn_ref, w_ref, x_ref, mod_ref, o_ref, z_ref):
    z_ref[:, 0:1024] = oa_ref[0]
    z_ref[:, 1024:1536] = ob_ref[0]
    for h in range(C_HEADS):
        hs = slice(h * C_DV, (h + 1) * C_DV)
        oc = oc_ref[0, 0, :, hs] + oc_ref[1, 0, :, hs]
        z_ref[:, 1536 + h * C_DV:1536 + (h + 1) * C_DV] = (
            _rms(oc, gn_ref[...]) * _silu(cg_ref[0, :, hs])).astype(BF16)
    y = jnp.dot(z_ref[...], w_ref[...], preferred_element_type=F32)
    o_ref[0] = x_ref[0] + mod_ref[0, 2:3, :] * y


def _outproj(oa, ob, oc, cg, gn, w, x, mod, layer, ctx):
    bsz, n, d = x.shape
    tm = min(512, n)
    return pl.pallas_call(
        _outproj_kernel,
        out_shape=jax.ShapeDtypeStruct((bsz, n, d), F32),
        grid=(bsz, n // tm),
        in_specs=[pl.BlockSpec((1, tm, 1024), lambda b, i: (b, i, 0)),
                  pl.BlockSpec((1, tm, 512), lambda b, i: (b, i, 0)),
                  pl.BlockSpec((2, 1, tm, 512), lambda b, i: (0, b, i, 0)),
                  pl.BlockSpec((1, tm, 512), lambda b, i: (b, i, 0)),
                  pl.BlockSpec((None, 1, C_DV), lambda b, i: (layer, 0, 0)),
                  pl.BlockSpec(w.shape, lambda b, i: (0, 0), pipeline_mode=pl.Buffered(1)),
                  pl.BlockSpec((1, tm, d), lambda b, i: (b, i, 0)),
                  pl.BlockSpec((1, 6, d), _mod_index(ctx))],
        out_specs=pl.BlockSpec((1, tm, d), lambda b, i: (b, i, 0)),
        scratch_shapes=[pltpu.VMEM((tm, w.shape[0]), BF16)],
        compiler_params=_cparams(("parallel", "parallel")),
        name="outproj",
    )(oa, ob, oc, cg, gn, w, x, mod)


def _ffn_kernel(x_ref, mod_ref, g_ref, wg_ref, wu_ref, wd_ref, o_ref, h_ref):
    j = pl.program_id(2)

    @pl.when(j == 0)
    def _():
        y = _rms(x_ref[0], g_ref[...])
        h_ref[...] = (y * (1.0 + mod_ref[0, 4:5, :]) + mod_ref[0, 3:4, :]).astype(BF16)
        o_ref[0] = jnp.zeros(o_ref.shape[1:], F32)

    h = h_ref[...]
    a = jnp.dot(h, wg_ref[...], preferred_element_type=F32)
    u = jnp.dot(h, wu_ref[...], preferred_element_type=F32)
    t = (_silu(a) * u).astype(BF16)
    o_ref[0] += jnp.dot(t, wd_ref[...], preferred_element_type=F32)

    @pl.when(j == pl.num_programs(2) - 1)
    def _():
        o_ref[0] = x_ref[0] + mod_ref[0, 5:6, :] * o_ref[0]


def _ffn(x, mod, g, wg, wu, wd, layer, ctx):
    bsz, n, d = x.shape
    f = wg.shape[1]
    tm = min(512, n)
    tf = 512
    return pl.pallas_call(
        _ffn_kernel,
        out_shape=jax.ShapeDtypeStruct((bsz, n, d), F32),
        grid=(bsz, n // tm, f // tf),
        in_specs=[pl.BlockSpec((1, tm, d), lambda b, i, j: (b, i, 0)),
                  pl.BlockSpec((1, 6, d), _mod_index(ctx)),
                  pl.BlockSpec((None, 1, d), lambda b, i, j: (layer, 0, 0)),
                  pl.BlockSpec((d, tf), lambda b, i, j: (0, j)),
                  pl.BlockSpec((d, tf), lambda b, i, j: (0, j)),
                  pl.BlockSpec((tf, d), lambda b, i, j: (j, 0))],
        out_specs=pl.BlockSpec((1, tm, d), lambda b, i, j: (b, i, 0)),
        scratch_shapes=[pltpu.VMEM((tm, d), BF16)],
        compiler_params=_cparams(("parallel", "parallel", "arbitrary")),
        name="ffn",
    )(x, mod, g, wg, wu, wd)


def _rope_tables(n):
    pos = np.arange(n)
    row = (pos // GRID_W).astype(np.float64)[:, None]
    col = (pos % GRID_W).astype(np.float64)[:, None]

    def table(half):
        inv = ROPE_THETA ** (-np.arange(half, dtype=np.float64) / half)
        ar, ac = row * inv[None, :], col * inv[None, :]
        cos = np.concatenate([np.cos(ar), np.cos(ar), np.cos(ac), np.cos(ac)], axis=1)
        sin = np.concatenate([-np.sin(ar), np.sin(ar), -np.sin(ac), np.sin(ac)], axis=1)
        return cos, sin

    cos_a, sin_a = table(A_HEAD_DIM // 4)
    cos_b, sin_b = table(B_ROPE // 4)
    cos_b, sin_b = np.tile(cos_b, (1, 2)), np.tile(sin_b, (1, 2))
    return tuple(jnp.asarray(t, F32) for t in (cos_a, sin_a, cos_b, sin_b))


def _prepare_weights(w_in, w_uq, w_ukv, w_gk_f, b_gk_f, w_gk_b, b_gk_b, smalls):
    depth, d, _ = w_in.shape
    w_r = jnp.concatenate([
        w_in[:, :, 0:2304],
        w_in[:, :, 2304:2368], w_in[:, :, 2304:2368],
        w_in[:, :, 3904:3936], jnp.zeros((depth, d, 96), w_in.dtype),
        w_in[:, :, 2368:3904],
    ], axis=2).astype(BF16)
    uq = w_uq.reshape(depth, B_Q_RANK, B_HEADS, B_NOPE + B_ROPE)
    uq = jnp.concatenate([uq[..., :B_NOPE].reshape(depth, B_Q_RANK, -1),
                          uq[..., B_NOPE:].reshape(depth, B_Q_RANK, -1)], axis=2)
    ukv = w_ukv.reshape(depth, B_KV_RANK, B_HEADS, B_NOPE + B_V)
    ukv = jnp.concatenate([ukv[..., :B_NOPE].reshape(depth, B_KV_RANK, -1),
                           ukv[..., B_NOPE:].reshape(depth, B_KV_RANK, -1)], axis=2)
    nk = C_HEADS * C_DK
    w_gk = jnp.zeros((depth, 128, 2 * nk), F32)
    w_gk = w_gk.at[:, 0:C_GATE_RANK, 0:nk].set(w_gk_f).at[:, C_GATE_RANK:2 * C_GATE_RANK, nk:].set(w_gk_b)
    lw = {k: v.reshape(depth, 1, -1) for k, v in smalls.items()}
    lw["b_q_rope_norm2"] = jnp.tile(lw["b_q_rope_norm"], (1, 1, 2))
    lw["b_k_rope_norm2"] = jnp.tile(lw["b_k_rope_norm"], (1, 1, 2))
    lw.update(w_in=w_r, w_uq=uq.astype(BF16), w_ukv=ukv.astype(BF16), w_gk=w_gk,
              b_gk=jnp.concatenate([b_gk_f, b_gk_b], axis=1).reshape(depth, 1, -1))
    return lw


def kernel(x, c, ctx, c_ctx, w_mod, b_mod, norm1_g, norm2_g, w_in, a_q_norm, a_k_norm, b_q_lora_norm, b_kv_lora_norm, w_uq, w_ukv, b_q_nope_norm, b_k_nope_norm, b_q_rope_norm, b_k_rope_norm, w_gk_f, b_gk_f, w_gk_b, b_gk_b, c_out_norm, w_out, w_gate, w_up, w_down):
    bsz, n, d = x.shape
    depth = w_in.shape[0]
    tabs = _rope_tables(n)
    assert bsz + 1 <= 8, "the modulation kernel holds the context row and all batch rows in one 8-row tile"
    cpad = jnp.concatenate([c_ctx[None, :], c, jnp.zeros((8 - bsz - 1, d), F32)], axis=0)
    smalls = dict(a_q_norm=a_q_norm, a_k_norm=a_k_norm, b_q_lora_norm=b_q_lora_norm,
                  b_kv_lora_norm=b_kv_lora_norm, b_q_nope_norm=b_q_nope_norm, b_k_nope_norm=b_k_nope_norm,
                  b_q_rope_norm=b_q_rope_norm, b_k_rope_norm=b_k_rope_norm, c_out_norm=c_out_norm,
                  norm1_g=norm1_g, norm2_g=norm2_g)
    lw = _prepare_weights(w_in, w_uq, w_ukv, w_gk_f, b_gk_f, w_gk_b, b_gk_b, smalls)
    ffn_out_w = (w_gate, w_up, w_down, w_out)
    b_mod3 = b_mod.reshape(depth, 1, -1)
    s_zero = jnp.zeros((2, bsz, 2, 2 * C_DV, 2 * C_DK), F32)
    a_args = dict(kv_heads=A_KV_HEADS, groups=A_HEADS // A_KV_HEADS, dq=A_HEAD_DIM, dv=A_HEAD_DIM)
    b_args = dict(kv_heads=B_HEADS, groups=1, dq=B_QK_PAD, dv=B_V)
    xc = ctx
    for l in range(depth):
        ctx_out = l < depth - 1
        mod = _modulation(cpad, w_mod, b_mod3, l)
        caq, cak, cav, cbq, cbk, cbv, cgd, ccq, cck, ccv, ccg = _proj(xc, mod, lw["norm1_g"], None, lw, l, True)
        laq, lak, lav, lbq, lbk, lbv, lgd, lcq, lck, lcv, lcg = _proj(x, mod, lw["norm1_g"], tabs, lw, l, False)

        o_a, wg, wu, wd, wo = _attention(laq, cak, cav, lak, lav, tq=512, cast=ffn_out_w, layer=l, **a_args)
        o_b = _attention(lbq, cbk, cbv, lbk, lbv, tq=1024, **b_args)
        oc_c, s_ctx = _gla(ccq, cck, ccv, cgd, s_zero)
        o_c, _ = _gla(lcq, lck, lcv, lgd, s_ctx)

        x = _outproj(o_a, o_b, o_c, lcg, lw["c_out_norm"], wo, x, mod, l, False)
        x = _ffn(x, mod, lw["norm2_g"], wg, wu, wd, l, False)
        if ctx_out:
            oc_a = _attention(caq, cak, cav, None, None, tq=256, **a_args)
            oc_b = _attention(cbq, cbk, cbv, None, None, tq=256, **b_args)
            xc = _outproj(oc_a, oc_b, oc_c, ccg, lw["c_out_norm"], wo, xc, mod, l, True)
            xc = _ffn(xc, mod, lw["norm2_g"], wg, wu, wd, l, True)
    return x
```

```python
import functools

import numpy as np
import jax
import jax.numpy as jnp
from jax import lax
from jax.experimental import pallas as pl
from jax.experimental.pallas import tpu as pltpu

F32 = jnp.float32
BF16 = jnp.bfloat16

D_MODEL = 2048
GRID_W = 64
ROPE_THETA = 10000.0
NORM_EPS = 1e-6

A_HEADS = 8
A_KV_HEADS = 2
A_HEAD_DIM = 128
B_HEADS = 4
B_Q_RANK = 512
B_KV_RANK = 256
B_NOPE = 128
B_ROPE = 64
B_V = 128
B_QK_PAD = 256
C_HEADS = 4
C_DK = 64
C_DV = 128
C_GATE_RANK = 16
C_GATE_NORM = 16.0

D_FF = 5632
P_COLS = 4096

COL_AQ = 0
COL_AK = 1024
COL_BCQ = 1536
COL_BCKV = 2048
COL_CQ = 2560
COL_CV = 3072
COL_CG = 3584

GLA_CHUNK = 128
GLA_SAFE_DECAY = 60.0

VMEM_LIMIT = 56 * 2**20
LOG2E = 1.4426950408889634


def _cparams(sem, vmem=VMEM_LIMIT):
    return pltpu.CompilerParams(dimension_semantics=sem, vmem_limit_bytes=vmem)


def _silu(x):
    return x / (1.0 + jnp.exp(-x))


def _rms(x, g):
    ms = jnp.mean(x * x, axis=-1, keepdims=True)
    return x * lax.rsqrt(ms + NORM_EPS) * g


def _mod_kernel(c_ref, w_ref, b_ref, o_ref):
    s = _silu(c_ref[...]).astype(BF16)
    o_ref[...] = jnp.dot(s, w_ref[...].astype(BF16), preferred_element_type=F32) + b_ref[...]


def _modulation(cpad, w_mod, b_mod, layer):
    d = cpad.shape[1]
    n = w_mod.shape[2]
    tn = 1024
    out = pl.pallas_call(
        _mod_kernel,
        out_shape=jax.ShapeDtypeStruct((8, n), F32),
        grid=(n // tn,),
        in_specs=[pl.BlockSpec((8, d), lambda j: (0, 0)),
                  pl.BlockSpec((None, d, tn), lambda j: (layer, 0, j)),
                  pl.BlockSpec((None, 1, tn), lambda j: (layer, 0, j))],
        out_specs=pl.BlockSpec((8, tn), lambda j: (0, j)),
        compiler_params=_cparams(("arbitrary",)),
        name="modulation",
    )(cpad, w_mod, b_mod)
    return out.reshape(8, 6, d)


def _mod_index(ctx):
    if ctx:
        return lambda b, i, *_: (0, 0, 0)
    return lambda b, i, *_: (b + 1, 0, 0)


def _swap_halves(x, lane, width):
    return jnp.where((lane // width) % 2 == 0,
                     pltpu.roll(x, 128 - width, 1), pltpu.roll(x, width, 1))


def _proj_kernel(*refs, rope):
    x_ref, mod_ref, g1_ref, w_ref = refs[:4]
    refs = refs[4:]
    if rope:
        cosa_ref, sina_ref, cosb_ref, sinb_ref = refs[:4]
        refs = refs[4:]
    (gaq_ref, gak_ref, gql_ref, gkvl_ref, gqn_ref, gkn_ref, gqr_ref, gkr_ref,
     wuq_ref, wukv_ref, wgk_ref, bgk_ref,
     aq_ref, ak_ref, av_ref, bq_ref, bk_ref, bv_ref, gd_ref, cq_ref, ck_ref, cv_ref, cg_ref) = refs
    tm = x_ref.shape[1]
    lane = lax.broadcasted_iota(jnp.int32, (tm, 128), 1)
    lo = lane < 64

    y = _rms(x_ref[0], g1_ref[...])
    h = (y * (1.0 + mod_ref[0, 1:2, :]) + mod_ref[0, 0:1, :]).astype(BF16)

    def proj(c0, width):
        return jnp.dot(h, w_ref[:, c0:c0 + width], preferred_element_type=F32)

    def rope_a(x):
        if not rope:
            return x
        return x * cosa_ref[...] + _swap_halves(x, lane, 32) * sina_ref[...]

    def rope_b(x):
        if not rope:
            return x
        return x * cosb_ref[...] + _swap_halves(x, lane, 16) * sinb_ref[...]

    a_scale = LOG2E * A_HEAD_DIM ** -0.5
    for half in range(2):
        pq = proj(COL_AQ + half * 512, 512)
        for j in range(4):
            hh = half * 4 + j
            x = pq[:, j * 128:(j + 1) * 128]
            aq_ref[0, :, hh * 128:(hh + 1) * 128] = (rope_a(_rms(x, gaq_ref[...])) * a_scale).astype(BF16)
    pkv = proj(COL_AK, 512)
    for hh in range(A_KV_HEADS):
        x = pkv[:, hh * 128:(hh + 1) * 128]
        ak_ref[0, :, hh * 128:(hh + 1) * 128] = rope_a(_rms(x, gak_ref[...])).astype(BF16)
    av_ref[0] = pkv[:, 256:512].astype(BF16)

    b_scale = LOG2E * (B_NOPE + B_ROPE) ** -0.5
    cq = _rms(proj(COL_BCQ, 512), gql_ref[...]).astype(BF16)
    qb = jnp.dot(cq, wuq_ref[...], preferred_element_type=F32)
    for hh in range(B_HEADS):
        x = qb[:, hh * 128:(hh + 1) * 128]
        bq_ref[0, :, hh * B_QK_PAD:hh * B_QK_PAD + 128] = (_rms(x, gqn_ref[...]) * b_scale).astype(BF16)
    for p in range(B_HEADS // 2):
        r = qb[:, 512 + p * 128:512 + (p + 1) * 128]
        sq = r * r
        ms_lo = jnp.sum(jnp.where(lo, sq, 0.0), axis=-1, keepdims=True)
        ms_hi = jnp.sum(jnp.where(lo, 0.0, sq), axis=-1, keepdims=True)
        ms = jnp.where(lo, ms_lo, ms_hi) * (1.0 / B_ROPE)
        rn = rope_b(r * lax.rsqrt(ms + NORM_EPS) * gqr_ref[...]) * b_scale
        h0 = 2 * p
        bq_ref[0, :, h0 * B_QK_PAD + 128:(h0 + 1) * B_QK_PAD] = jnp.where(lo, rn, 0.0).astype(BF16)
        bq_ref[0, :, (h0 + 1) * B_QK_PAD + 128:(h0 + 2) * B_QK_PAD] = jnp.where(lo, 0.0, rn).astype(BF16)

    pmix = proj(COL_BCKV, 512)
    ckv = _rms(pmix[:, 0:256], gkvl_ref[...]).astype(BF16)
    kv = jnp.dot(ckv, wukv_ref[...], preferred_element_type=F32)
    krn = rope_b(_rms(pmix[:, 256:384], gkr_ref[...]))
    kr_even = jnp.where(lo, krn, 0.0).astype(BF16)
    kr_odd = jnp.where(lo, 0.0, krn).astype(BF16)
    for hh in range(B_HEADS):
        x = kv[:, hh * 128:(hh + 1) * 128]
        bk_ref[0, :, hh * B_QK_PAD:hh * B_QK_PAD + 128] = _rms(x, gkn_ref[...]).astype(BF16)
        bk_ref[0, :, hh * B_QK_PAD + 128:(hh + 1) * B_QK_PAD] = kr_even if hh % 2 == 0 else kr_odd
    bv_ref[0] = kv[:, 512:1024].astype(BF16)

    z = jnp.dot(pmix[:, 384:512].astype(BF16), wgk_ref[...].astype(BF16),
                preferred_element_type=F32) + bgk_ref[...]
    gd_ref[0] = (jnp.minimum(z, 0.0) - jnp.log(1.0 + jnp.exp(-jnp.abs(z)))) * (1.0 / C_GATE_NORM)
    pqk = proj(COL_CQ, 512)
    cq_ref[0] = pqk[:, 0:256]
    ck_ref[0] = pqk[:, 256:512]
    cv_ref[0] = proj(COL_CV, 512).astype(BF16)
    cg_ref[0] = proj(COL_CG, 512)


def _proj(x, mod, g1, tabs, lw, layer, ctx):
    bsz, n, d = x.shape
    tm = min(512, n)
    rope = not ctx
    row = lambda b, i: (b, i, 0)
    once = pl.Buffered(1)
    in_specs = [pl.BlockSpec((1, tm, d), row),
                pl.BlockSpec((1, 6, d), _mod_index(ctx)),
                pl.BlockSpec((None, 1, d), lambda b, i: (layer, 0, 0)),
                pl.BlockSpec((None, d, P_COLS), lambda b, i: (layer, 0, 0), pipeline_mode=once)]
    args = [x, mod, g1, lw["w_in"]]
    if rope:
        in_specs += [pl.BlockSpec((tm, 128), lambda b, i: (i, 0))] * 4
        args += list(tabs)
    small = [lw["a_q_norm"], lw["a_k_norm"], lw["b_q_lora_norm"], lw["b_kv_lora_norm"],
             lw["b_q_nope_norm"], lw["b_k_nope_norm"], lw["b_q_rope_norm2"], lw["b_k_rope_norm2"],
             lw["w_uq"], lw["w_ukv"], lw["w_gk"], lw["b_gk"]]
    in_specs += [pl.BlockSpec((None,) + a.shape[1:], lambda b, i: (layer, 0, 0), pipeline_mode=once) for a in small]
    args += small
    out_shape = [jax.ShapeDtypeStruct((bsz, n, 1024), BF16),
                 jax.ShapeDtypeStruct((bsz, n, 256), BF16),
                 jax.ShapeDtypeStruct((bsz, n, 256), BF16),
                 jax.ShapeDtypeStruct((bsz, n, B_HEADS * B_QK_PAD), BF16),
                 jax.ShapeDtypeStruct((bsz, n, B_HEADS * B_QK_PAD), BF16),
                 jax.ShapeDtypeStruct((bsz, n, B_HEADS * B_V), BF16),
                 jax.ShapeDtypeStruct((bsz, n, 512), F32),
                 jax.ShapeDtypeStruct((bsz, n, 256), F32),
                 jax.ShapeDtypeStruct((bsz, n, 256), F32),
                 jax.ShapeDtypeStruct((bsz, n, 512), BF16),
                 jax.ShapeDtypeStruct((bsz, n, 512), F32)]
    out_specs = [pl.BlockSpec((1, tm, s.shape[2]), row) for s in out_shape]
    return pl.pallas_call(
        functools.partial(_proj_kernel, rope=rope),
        out_shape=out_shape,
        grid=(bsz, n // tm),
        in_specs=in_specs,
        out_specs=out_specs,
        compiler_params=_cparams(("parallel", "parallel")),
        name="proj_ctx" if ctx else "proj_lat",
    )(*args)


def _attn_kernel(*refs, groups, dq, dv, tk, rsub, has_lat, n_cast):
    if has_lat:
        q_ref, kc_ref, vc_ref, kl_ref, vl_ref = refs[:5]
        refs = refs[5:]
    else:
        q_ref, kc_ref, vc_ref = refs[:3]
        refs = refs[3:]
    cast_in, refs = refs[:n_cast], refs[n_cast:]
    o_ref, cast_out = refs[0], refs[1:1 + n_cast]
    q_scr, s_scr, p_scr, m_scr, l_scr, acc_scr = refs[1 + n_cast:]
    for src_ref, dst_ref in zip(cast_in, cast_out):
        dst_ref[...] = src_ref[...].astype(BF16)
    tq = q_ref.shape[1]
    m_rows = groups * tq
    nt = (((1,), (1,)), ((), ()))
    for g in range(groups):
        q_scr[g * tq:(g + 1) * tq, :] = q_ref[0, :, g * dq:(g + 1) * dq]
    m_scr[...] = jnp.full(m_scr.shape, -jnp.inf, F32)
    l_scr[...] = jnp.zeros(l_scr.shape, F32)
    acc_scr[...] = jnp.zeros(acc_scr.shape, F32)

    def scores(slot, k):
        s_scr[slot, :, 0:k.shape[0]] = lax.dot_general(q_scr[...], k, nt, preferred_element_type=F32)

    def softmax_pv(slot, v):
        w = v.shape[0]
        for r in range(m_rows // rsub):
            rs = slice(r * rsub, (r + 1) * rsub)
            cols = [s_scr[slot, rs, j * 128:(j + 1) * 128] for j in range(w // 128)]
            mx = functools.reduce(jnp.maximum, cols)
            m_old = m_scr[rs, :]
            m_new = jnp.maximum(m_old, jnp.max(mx, axis=-1, keepdims=True))
            ps = [jnp.exp2(c - m_new) for c in cols]
            a = jnp.exp2(m_old - m_new)
            l_scr[rs, :] = a * l_scr[rs, :] + functools.reduce(jnp.add, ps)
            m_scr[rs, :] = m_new
            acc_scr[rs, :] = a * acc_scr[rs, :]
            for j, p in enumerate(ps):
                p_scr[rs, j * 128:(j + 1) * 128] = p.astype(BF16)
        acc_scr[...] += jnp.dot(p_scr[:, 0:w], v, preferred_element_type=F32)

    chunks = []
    if has_lat:
        chunks += [(kl_ref, vl_ref, slice(c * tk, (c + 1) * tk)) for c in range(kl_ref.shape[1] // tk)]
    chunks.append((kc_ref, vc_ref, slice(0, kc_ref.shape[1])))
    scores(0, chunks[0][0][0, chunks[0][2], :])
    for n, (_, v_ref, rows) in enumerate(chunks):
        if n + 1 < len(chunks):
            k_next, _, rows_next = chunks[n + 1]
            scores((n + 1) % 2, k_next[0, rows_next, :])
        softmax_pv(n % 2, v_ref[0, rows, :])
    for g in range(groups):
        gs = slice(g * tq, (g + 1) * tq)
        l = jnp.sum(l_scr[gs, :], axis=-1, keepdims=True)
        o_ref[0, :, g * dv:(g + 1) * dv] = (acc_scr[gs, :] * (1.0 / l)).astype(o_ref.dtype)


def _attention(q, kc, vc, kl, vl, *, kv_heads, groups, dq, dv, tq, tk=1024, rsub=64, cast=(), layer=0):
    assert dv == 128, "the accumulator rescale reuses the 128-lane replicated running max"
    bsz, n, _ = q.shape
    nc = kc.shape[1]
    has_lat = kl is not None
    tq = min(tq, n)
    m_rows = groups * tq
    nq = n // tq
    steps = bsz * kv_heads * nq
    wmax = max(nc, tk) if has_lat else nc
    in_specs = [pl.BlockSpec((1, tq, groups * dq), lambda b, h, i: (b, i, h)),
                pl.BlockSpec((1, nc, dq), lambda b, h, i: (b, 0, h)),
                pl.BlockSpec((1, nc, dv), lambda b, h, i: (b, 0, h))]
    args = [q, kc, vc]
    if has_lat:
        nl = kl.shape[1]
        in_specs += [pl.BlockSpec((1, nl, dq), lambda b, h, i: (b, 0, h)),
                     pl.BlockSpec((1, nl, dv), lambda b, h, i: (b, 0, h))]
        args += [kl, vl]
    out_shape = [jax.ShapeDtypeStruct((bsz, n, kv_heads * groups * dv), BF16)]
    out_specs = [pl.BlockSpec((1, tq, groups * dv), lambda b, h, i: (b, i, h))]
    def slab_specs(w):
        share = next(k for k in (1, 2, 4, 8) if w.shape[1] * k % (steps * 16) == 0)
        rows, cols = w.shape[1] * share // steps, w.shape[2]
        slab = lambda b, h, i: ((b * kv_heads + h) * nq + i) // share
        return (pl.BlockSpec((None, rows, cols), lambda b, h, i: (layer, slab(b, h, i), 0)),
                pl.BlockSpec((rows, cols), lambda b, h, i: (slab(b, h, i), 0)))

    for w in cast:
        spec_in, spec_out = slab_specs(w)
        in_specs.append(spec_in)
        out_specs.append(spec_out)
        out_shape.append(jax.ShapeDtypeStruct(w.shape[1:], BF16))
        args.append(w)
    outs = pl.pallas_call(
        functools.partial(_attn_kernel, groups=groups, dq=dq, dv=dv, tk=tk, rsub=rsub, has_lat=has_lat,
                          n_cast=len(cast)),
        out_shape=out_shape,
        grid=(bsz, kv_heads, nq),
        in_specs=in_specs,
        out_specs=out_specs,
        scratch_shapes=[pltpu.VMEM((m_rows, dq), BF16), pltpu.VMEM((2, m_rows, wmax), F32),
                        pltpu.VMEM((m_rows, wmax), BF16), pltpu.VMEM((m_rows, 128), F32),
                        pltpu.VMEM((m_rows, 128), F32), pltpu.VMEM((m_rows, dv), F32)],
        compiler_params=_cparams(("parallel", "parallel", "arbitrary")),
        name="attention",
    )(*args)
    return outs if cast else outs[0]


def _gla_kernel(q_ref, k_ref, v_ref, g_ref, s0_ref, o_ref, sfin_ref, st_ref, att_ref, cb_ref, kk_ref, *, chunk):
    d = pl.program_id(0)
    i = pl.program_id(2)
    t_rows = q_ref.shape[1]
    nch = t_rows // chunk

    @pl.when(i == 0)
    def _():
        st_ref[...] = s0_ref[0, 0]

    row = lax.broadcasted_iota(jnp.int32, (chunk, chunk), 0)
    col = lax.broadcasted_iota(jnp.int32, (chunk, chunk), 1)
    tri = jnp.where(d == 0, col - row, row - col) <= 0
    tri_bf = jnp.where(tri, 1.0, 0.0).astype(BF16)
    row2 = lax.broadcasted_iota(jnp.int32, (chunk, 2 * chunk), 0)
    lane2 = lax.broadcasted_iota(jnp.int32, (chunk, 2 * chunk), 1)
    col2 = jnp.where(lane2 >= chunk, lane2 - chunk, lane2)
    tri2 = jnp.where(d == 0, col2 - row2, row2 - col2) <= 0
    lo = lax.broadcasted_iota(jnp.int32, (chunk, 128), 1) < C_DK
    vlo = lax.broadcasted_iota(jnp.int32, (chunk, 2 * C_DV), 1) < C_DV
    srow = lax.broadcasted_iota(jnp.int32, (2 * C_DV, 2 * C_DK), 0) // C_DV
    scol = lax.broadcasted_iota(jnp.int32, (2 * C_DV, 2 * C_DK), 1) // C_DK
    same_head = srow == scol
    nt = (((1,), (1,)), ((), ()))
    tn = (((0,), (0,)), ((), ()))

    def intra_scores_mxu(q_p, k_p, cb_p, qe_p):
        ke_p = (k_p * jnp.exp(-cb_p)).astype(BF16)
        zk = jnp.zeros_like(ke_p)
        ke_bd = jnp.concatenate([jnp.where(lo, ke_p, zk), jnp.where(lo, zk, ke_p)], axis=0)
        return lax.dot_general(qe_p, ke_bd, nt, preferred_element_type=F32)

    def intra_scores_guarded(q_p, k_p, cb_p, qe_p):
        cb_ref[...] = cb_p
        kk_ref[...] = k_p
        att_ref[...] = jnp.zeros(att_ref.shape, F32)

        def one_source(s, carry):
            cb_s = cb_ref[pl.ds(s, 1), :]
            w = q_p * jnp.exp(jnp.minimum(cb_p - cb_s, 0.0)) * kk_ref[pl.ds(s, 1), :]
            w0 = jnp.sum(jnp.where(lo, w, 0.0), axis=-1, keepdims=True)
            w1 = jnp.sum(jnp.where(lo, 0.0, w), axis=-1, keepdims=True)
            att_ref[...] += jnp.where(lane2 == s, w0, 0.0) + jnp.where(lane2 == s + chunk, w1, 0.0)
            return carry

        lax.fori_loop(0, chunk, one_source, 0)
        return att_ref[...]

    def scan_block(intra_scores):
        for c in range(nch):
            cc = c + d * (nch - 1 - 2 * c)
            r0 = pl.multiple_of(cc * chunk, chunk)
            g = g_ref[0, pl.ds(r0, chunk), :]
            g_hi = g.astype(BF16)
            g_lo = (g - g_hi.astype(F32)).astype(BF16)
            cb = (jnp.dot(tri_bf, g_hi, preferred_element_type=F32)
                  + jnp.dot(tri_bf, g_lo, preferred_element_type=F32))
            tot = jnp.sum(g, axis=0, keepdims=True)
            q = q_ref[0, pl.ds(r0, chunk), :] * (C_DK ** -0.5)
            k = k_ref[0, pl.ds(r0, chunk), :]
            v = v_ref[0, pl.ds(r0, chunk), :]
            qe = (q * jnp.exp(cb)).astype(BF16)
            kd = (k * jnp.exp(tot - cb)).astype(BF16)
            dec = jnp.exp(tot)
            for p in range(C_HEADS // 2):
                ls = slice(p * 128, (p + 1) * 128)
                vs = slice(p * 2 * C_DV, (p + 1) * 2 * C_DV)
                qe_p, kd_p, v_p = qe[:, ls], kd[:, ls], v[:, vs]
                att = intra_scores(q[:, ls], k[:, ls], cb[:, ls], qe_p)
                att = jnp.where(tri2, att, 0.0).astype(BF16)
                zv = jnp.zeros_like(v_p)
                v_bd = jnp.concatenate([jnp.where(vlo, v_p, zv), jnp.where(vlo, zv, v_p)], axis=0)
                st = st_ref[p]
                o = (jnp.dot(att, v_bd, preferred_element_type=F32)
                     + lax.dot_general(qe_p, st.astype(BF16), nt, preferred_element_type=F32))
                o_ref[0, 0, pl.ds(r0, chunk), vs] = o
                u = lax.dot_general(v_p, kd_p, tn, preferred_element_type=F32)
                st_ref[p] = dec[:, ls] * st + jnp.where(same_head, u, 0.0)

    worst = functools.reduce(jnp.maximum, [
        jnp.max(-jnp.sum(g_ref[0, c * chunk:(c + 1) * chunk, :], axis=0, keepdims=True)) for c in range(nch)])
    safe = worst <= GLA_SAFE_DECAY

    @pl.when(safe)
    def _():
        scan_block(intra_scores_mxu)

    @pl.when(jnp.logical_not(safe))
    def _():
        scan_block(intra_scores_guarded)

    @pl.when(i == pl.num_programs(2) - 1)
    def _():
        sfin_ref[0, 0] = st_ref[...]


def _gla(cq, ck, cv, gdec, s0):
    bsz, n, _ = cq.shape
    t_rows = min(1024, n)
    nb = n // t_rows
    rb = lambda d, i: i + d * (nb - 1 - 2 * i)
    o, sfin = pl.pallas_call(
        functools.partial(_gla_kernel, chunk=GLA_CHUNK),
        out_shape=[jax.ShapeDtypeStruct((2, bsz, n, C_HEADS * C_DV), F32),
                   jax.ShapeDtypeStruct(s0.shape, F32)],
        grid=(2, bsz, nb),
        in_specs=[pl.BlockSpec((1, t_rows, 256), lambda d, b, i: (b, rb(d, i), 0)),
                  pl.BlockSpec((1, t_rows, 256), lambda d, b, i: (b, rb(d, i), 0)),
                  pl.BlockSpec((1, t_rows, 512), lambda d, b, i: (b, rb(d, i), 0)),
                  pl.BlockSpec((1, t_rows, 256), lambda d, b, i: (b, rb(d, i), d)),
                  pl.BlockSpec((1, 1, 2, 256, 128), lambda d, b, i: (d, b, 0, 0, 0))],
        out_specs=[pl.BlockSpec((1, 1, t_rows, 512), lambda d, b, i: (d, b, rb(d, i), 0)),
                   pl.BlockSpec((1, 1, 2, 256, 128), lambda d, b, i: (d, b, 0, 0, 0))],
        scratch_shapes=[pltpu.VMEM((2, 256, 128), F32), pltpu.VMEM((GLA_CHUNK, 2 * GLA_CHUNK), F32),
                        pltpu.VMEM((GLA_CHUNK, 128), F32), pltpu.VMEM((GLA_CHUNK, 128), F32)],
        compiler_params=_cparams(("parallel", "parallel", "arbitrary")),
        name="gla",
    )(cq, ck, cv, gdec, s0)
    return o, sfin


def _outproj_kernel(oa_ref, ob_ref, oc_ref, cg_ref, gn_ref, w_ref, x_ref, mod_ref, o_ref, z_ref):
    z_ref[:, 0:1024] = oa_ref[0]
    z_ref[:, 1024:1536] = ob_ref[0]
    for h in range(C_HEADS):
        hs = slice(h * C_DV, (h + 1) * C_DV)
        oc = oc_ref[0, 0, :, hs] + oc_ref[1, 0, :, hs]
        z_ref[:, 1536 + h * C_DV:1536 + (h + 1) * C_DV] = (
            _rms(oc, gn_ref[...]) * _silu(cg_ref[0, :, hs])).astype(BF16)
    y = jnp.dot(z_ref[...], w_ref[...], preferred_element_type=F32)
    o_ref[0] = x_ref[0] + mod_ref[0, 2:3, :] * y


def _outproj(oa, ob, oc, cg, gn, w, x, mod, layer, ctx):
    bsz, n, d = x.shape
    tm = min(512, n)
    return pl.pallas_call(
        _outproj_kernel,
        out_shape=jax.ShapeDtypeStruct((bsz, n, d), F32),
        grid=(bsz, n // tm),
        in_specs=[pl.BlockSpec((1, tm, 1024), lambda b, i: (b, i, 0)),
                  pl.BlockSpec((1, tm, 512), lambda b, i: (b, i, 0)),
                  pl.BlockSpec((2, 1, tm, 512), lambda b, i: (0, b, i, 0)),
                  pl.BlockSpec((1, tm, 512), lambda b, i: (b, i, 0)),
                  pl.BlockSpec((None, 1, C_DV), lambda b, i: (layer, 0, 0)),
                  pl.BlockSpec(w.shape, lambda b, i: (0, 0), pipeline_mode=pl.Buffered(1)),
                  pl.BlockSpec((1, tm, d), lambda b, i: (b, i, 0)),
                  pl.BlockSpec((1, 6, d), _mod_index(ctx))],
        out_specs=pl.BlockSpec((1, tm, d), lambda b, i: (b, i, 0)),
        scratch_shapes=[pltpu.VMEM((tm, w.shape[0]), BF16)],
        compiler_params=_cparams(("parallel", "parallel")),
        name="outproj",
    )(oa, ob, oc, cg, gn, w, x, mod)


def _ffn_kernel(x_ref, mod_ref, g_ref, wg_ref, wu_ref, wd_ref, o_ref, h_ref):
    j = pl.program_id(2)

    @pl.when(j == 0)
    def _():
        y = _rms(x_ref[0], g_ref[...])
        h_ref[...] = (y * (1.0 + mod_ref[0, 4:5, :]) + mod_ref[0, 3:4, :]).astype(BF16)
        o_ref[0] = jnp.zeros(o_ref.shape[1:], F32)

    h = h_ref[...]
    a = jnp.dot(h, wg_ref[...], preferred_element_type=F32)
    u = jnp.dot(h, wu_ref[...], preferred_element_type=F32)
    t = (_silu(a) * u).astype(BF16)
    o_ref[0] += jnp.dot(t, wd_ref[...], preferred_element_type=F32)

    @pl.when(j == pl.num_programs(2) - 1)
    def _():
        o_ref[0] = x_ref[0] + mod_ref[0, 5:6, :] * o_ref[0]


def _ffn(x, mod, g, wg, wu, wd, layer, ctx):
    bsz, n, d = x.shape
    f = wg.shape[1]
    tm = min(512, n)
    tf = 512
    return pl.pallas_call(
        _ffn_kernel,
        out_shape=jax.ShapeDtypeStruct((bsz, n, d), F32),
        grid=(bsz, n // tm, f // tf),
        in_specs=[pl.BlockSpec((1, tm, d), lambda b, i, j: (b, i, 0)),
                  pl.BlockSpec((1, 6, d), _mod_index(ctx)),
                  pl.BlockSpec((None, 1, d), lambda b, i, j: (layer, 0, 0)),
                  pl.BlockSpec((d, tf), lambda b, i, j: (0, j)),
                  pl.BlockSpec((d, tf), lambda b, i, j: (0, j)),
                  pl.BlockSpec((tf, d), lambda b, i, j: (j, 0))],
        out_specs=pl.BlockSpec((1, tm, d), lambda b, i, j: (b, i, 0)),
        scratch_shapes=[pltpu.VMEM((tm, d), BF16)],
        compiler_params=_cparams(("parallel", "parallel", "arbitrary")),
        name="ffn",
    )(x, mod, g, wg, wu, wd)


def _rope_tables(n):
    pos = np.arange(n)
    row = (pos // GRID_W).astype(np.float64)[:, None]
    col = (pos % GRID_W).astype(np.float64)[:, None]

    def table(half):
        inv = ROPE_THETA ** (-np.arange(half, dtype=np.float64) / half)
        ar, ac = row * inv[None, :], col * inv[None, :]
        cos = np.concatenate([np.cos(ar), np.cos(ar), np.cos(ac), np.cos(ac)], axis=1)
        sin = np.concatenate([-np.sin(ar), np.sin(ar), -np.sin(ac), np.sin(ac)], axis=1)
        return cos, sin

    cos_a, sin_a = table(A_HEAD_DIM // 4)
    cos_b, sin_b = table(B_ROPE // 4)
    cos_b, sin_b = np.tile(cos_b, (1, 2)), np.tile(sin_b, (1, 2))
    return tuple(jnp.asarray(t, F32) for t in (cos_a, sin_a, cos_b, sin_b))


def _prepare_weights(w_in, w_uq, w_ukv, w_gk_f, b_gk_f, w_gk_b, b_gk_b, smalls):
    depth, d, _ = w_in.shape
    w_r = jnp.concatenate([
        w_in[:, :, 0:2304],
        w_in[:, :, 2304:2368], w_in[:, :, 2304:2368],
        w_in[:, :, 3904:3936], jnp.zeros((depth, d, 96), w_in.dtype),
        w_in[:, :, 2368:3904],
    ], axis=2).astype(BF16)
    uq = w_uq.reshape(depth, B_Q_RANK, B_HEADS, B_NOPE + B_ROPE)
    uq = jnp.concatenate([uq[..., :B_NOPE].reshape(depth, B_Q_RANK, -1),
                          uq[..., B_NOPE:].reshape(depth, B_Q_RANK, -1)], axis=2)
    ukv = w_ukv.reshape(depth, B_KV_RANK, B_HEADS, B_NOPE + B_V)
    ukv = jnp.concatenate([ukv[..., :B_NOPE].reshape(depth, B_KV_RANK, -1),
                           ukv[..., B_NOPE:].reshape(depth, B_KV_RANK, -1)], axis=2)
    nk = C_HEADS * C_DK
    w_gk = jnp.zeros((depth, 128, 2 * nk), F32)
    w_gk = w_gk.at[:, 0:C_GATE_RANK, 0:nk].set(w_gk_f).at[:, C_GATE_RANK:2 * C_GATE_RANK, nk:].set(w_gk_b)
    lw = {k: v.reshape(depth, 1, -1) for k, v in smalls.items()}
    lw["b_q_rope_norm2"] = jnp.tile(lw["b_q_rope_norm"], (1, 1, 2))
    lw["b_k_rope_norm2"] = jnp.tile(lw["b_k_rope_norm"], (1, 1, 2))
    lw.update(w_in=w_r, w_uq=uq.astype(BF16), w_ukv=ukv.astype(BF16), w_gk=w_gk,
              b_gk=jnp.concatenate([b_gk_f, b_gk_b], axis=1).reshape(depth, 1, -1))
    return lw


def kernel(x, c, ctx, c_ctx, w_mod, b_mod, norm1_g, norm2_g, w_in, a_q_norm, a_k_norm, b_q_lora_norm, b_kv_lora_norm, w_uq, w_ukv, b_q_nope_norm, b_k_nope_norm, b_q_rope_norm, b_k_rope_norm, w_gk_f, b_gk_f, w_gk_b, b_gk_b, c_out_norm, w_out, w_gate, w_up, w_down):
    bsz, n, d = x.shape
    depth = w_in.shape[0]
    tabs = _rope_tables(n)
    assert bsz + 1 <= 8, "the modulation kernel holds the context row and all batch rows in one 8-row tile"
    cpad = jnp.concatenate([c_ctx[None, :], c, jnp.zeros((8 - bsz - 1, d), F32)], axis=0)
    smalls = dict(a_q_norm=a_q_norm, a_k_norm=a_k_norm, b_q_lora_norm=b_q_lora_norm,
                  b_kv_lora_norm=b_kv_lora_norm, b_q_nope_norm=b_q_nope_norm, b_k_nope_norm=b_k_nope_norm,
                  b_q_rope_norm=b_q_rope_norm, b_k_rope_norm=b_k_rope_norm, c_out_norm=c_out_norm,
                  norm1_g=norm1_g, norm2_g=norm2_g)
    lw = _prepare_weights(w_in, w_uq, w_ukv, w_gk_f, b_gk_f, w_gk_b, b_gk_b, smalls)
    ffn_out_w = (w_gate, w_up, w_down, w_out)
    b_mod3 = b_mod.reshape(depth, 1, -1)
    s_zero = jnp.zeros((2, bsz, 2, 2 * C_DV, 2 * C_DK), F32)
    a_args = dict(kv_heads=A_KV_HEADS, groups=A_HEADS // A_KV_HEADS, dq=A_HEAD_DIM, dv=A_HEAD_DIM)
    b_args = dict(kv_heads=B_HEADS, groups=1, dq=B_QK_PAD, dv=B_V)
    xc = ctx
    for l in range(depth):
        ctx_out = l < depth - 1
        mod = _modulation(cpad, w_mod, b_mod3, l)
        caq, cak, cav, cbq, cbk, cbv, cgd, ccq, cck, ccv, ccg = _proj(xc, mod, lw["norm1_g"], None, lw, l, True)
        laq, lak, lav, lbq, lbk, lbv, lgd, lcq, lck, lcv, lcg = _proj(x, mod, lw["norm1_g"], tabs, lw, l, False)

        o_a, wg, wu, wd, wo = _attention(laq, cak, cav, lak, lav, tq=512, cast=ffn_out_w, layer=l, **a_args)
        o_b = _attention(lbq, cbk, cbv, lbk, lbv, tq=1024, **b_args)
        oc_c, s_ctx = _gla(ccq, cck, ccv, cgd, s_zero)
        o_c, _ = _gla(lcq, lck, lcv, lgd, s_ctx)

        x = _outproj(o_a, o_b, o_c, lcg, lw["c_out_norm"], wo, x, mod, l, False)
        x = _ffn(x, mod, lw["norm2_g"], wg, wu, wd, l, False)
        if ctx_out:
            oc_a = _attention(caq, cak, cav, None, None, tq=256, **a_args)
            oc_b = _attention(cbq, cbk, cbv, None, None, tq=256, **b_args)
            xc = _outproj(oc_a, oc_b, oc_c, ccg, lw["c_out_norm"], wo, xc, mod, l, True)
            xc = _ffn(xc, mod, lw["norm2_g"], wg, wu, wd, l, True)
    return x
```

```python
import functools

import numpy as np
import jax
import jax.numpy as jnp
from jax import lax
from jax.experimental import pallas as pl
from jax.experimental.pallas import tpu as pltpu

F32 = jnp.float32
BF16 = jnp.bfloat16

D_MODEL = 2048
GRID_W = 64
ROPE_THETA = 10000.0
NORM_EPS = 1e-6

A_HEADS = 8
A_KV_HEADS = 2
A_HEAD_DIM = 128
B_HEADS = 4
B_Q_RANK = 512
B_KV_RANK = 256
B_NOPE = 128
B_ROPE = 64
B_V = 128
B_QK_PAD = 256
C_HEADS = 4
C_DK = 64
C_DV = 128
C_GATE_RANK = 16
C_GATE_NORM = 16.0

D_FF = 5632
P_COLS = 4096

COL_AQ = 0
COL_AK = 1024
COL_BCQ = 1536
COL_BCKV = 2048
COL_CQ = 2560
COL_CV = 3072
COL_CG = 3584

GLA_CHUNK = 128
GLA_SAFE_DECAY = 60.0

VMEM_LIMIT = 56 * 2**20
LOG2E = 1.4426950408889634


def _cparams(sem, vmem=VMEM_LIMIT):
    return pltpu.CompilerParams(dimension_semantics=sem, vmem_limit_bytes=vmem)


def _silu(x):
    return x / (1.0 + jnp.exp(-x))


def _rms(x, g):
    ms = jnp.mean(x * x, axis=-1, keepdims=True)
    return x * lax.rsqrt(ms + NORM_EPS) * g


def _mod_kernel(c_ref, w_ref, b_ref, o_ref):
    s = _silu(c_ref[...]).astype(BF16)
    o_ref[...] = jnp.dot(s, w_ref[...].astype(BF16), preferred_element_type=F32) + b_ref[...]


def _modulation(cpad, w_mod, b_mod, layer):
    d = cpad.shape[1]
    n = w_mod.shape[2]
    tn = 1024
    out = pl.pallas_call(
        _mod_kernel,
        out_shape=jax.ShapeDtypeStruct((8, n), F32),
        grid=(n // tn,),
        in_specs=[pl.BlockSpec((8, d), lambda j: (0, 0)),
                  pl.BlockSpec((None, d, tn), lambda j: (layer, 0, j)),
                  pl.BlockSpec((None, 1, tn), lambda j: (layer, 0, j))],
        out_specs=pl.BlockSpec((8, tn), lambda j: (0, j)),
        compiler_params=_cparams(("arbitrary",)),
        name="modulation",
    )(cpad, w_mod, b_mod)
    return out.reshape(8, 6, d)


def _mod_index(ctx):
    if ctx:
        return lambda b, i, *_: (0, 0, 0)
    return lambda b, i, *_: (b + 1, 0, 0)


def _swap_halves(x, lane, width):
    return jnp.where((lane // width) % 2 == 0,
                     pltpu.roll(x, 128 - width, 1), pltpu.roll(x, width, 1))


def _proj_kernel(*refs, rope):
    x_ref, mod_ref, g1_ref, w_ref = refs[:4]
    refs = refs[4:]
    if rope:
        cosa_ref, sina_ref, cosb_ref, sinb_ref = refs[:4]
        refs = refs[4:]
    (gaq_ref, gak_ref, gql_ref, gkvl_ref, gqn_ref, gkn_ref, gqr_ref, gkr_ref,
     wuq_ref, wukv_ref, wgk_ref, bgk_ref,
     aq_ref, ak_ref, av_ref, bq_ref, bk_ref, bv_ref, gd_ref, cq_ref, ck_ref, cv_ref, cg_ref) = refs
    tm = x_ref.shape[1]
    lane = lax.broadcasted_iota(jnp.int32, (tm, 128), 1)
    lo = lane < 64

    y = _rms(x_ref[0], g1_ref[...])
    h = (y * (1.0 + mod_ref[0, 1:2, :]) + mod_ref[0, 0:1, :]).astype(BF16)

    def proj(c0, width):
        return jnp.dot(h, w_ref[:, c0:c0 + width], preferred_element_type=F32)

    def rope_a(x):
        if not rope:
            return x
        return x * cosa_ref[...] + _swap_halves(x, lane, 32) * sina_ref[...]

    def rope_b(x):
        if not rope:
            return x
        return x * cosb_ref[...] + _swap_halves(x, lane, 16) * sinb_ref[...]

    a_scale = LOG2E * A_HEAD_DIM ** -0.5
    for half in range(2):
        pq = proj(COL_AQ + half * 512, 512)
        for j in range(4):
            hh = half * 4 + j
            x = pq[:, j * 128:(j + 1) * 128]
            aq_ref[0, :, hh * 128:(hh + 1) * 128] = (rope_a(_rms(x, gaq_ref[...])) * a_scale).astype(BF16)
    pkv = proj(COL_AK, 512)
    for hh in range(A_KV_HEADS):
        x = pkv[:, hh * 128:(hh + 1) * 128]
        ak_ref[0, :, hh * 128:(hh + 1) * 128] = rope_a(_rms(x, gak_ref[...])).astype(BF16)
    av_ref[0] = pkv[:, 256:512].astype(BF16)

    b_scale = LOG2E * (B_NOPE + B_ROPE) ** -0.5
    cq = _rms(proj(COL_BCQ, 512), gql_ref[...]).astype(BF16)
    qb = jnp.dot(cq, wuq_ref[...], preferred_element_type=F32)
    for hh in range(B_HEADS):
        x = qb[:, hh * 128:(hh + 1) * 128]
        bq_ref[0, :, hh * B_QK_PAD:hh * B_QK_PAD + 128] = (_rms(x, gqn_ref[...]) * b_scale).astype(BF16)
    for p in range(B_HEADS // 2):
        r = qb[:, 512 + p * 128:512 + (p + 1) * 128]
        sq = r * r
        ms_lo = jnp.sum(jnp.where(lo, sq, 0.0), axis=-1, keepdims=True)
        ms_hi = jnp.sum(jnp.where(lo, 0.0, sq), axis=-1, keepdims=True)
        ms = jnp.where(lo, ms_lo, ms_hi) * (1.0 / B_ROPE)
        rn = rope_b(r * lax.rsqrt(ms + NORM_EPS) * gqr_ref[...]) * b_scale
        h0 = 2 * p
        bq_ref[0, :, h0 * B_QK_PAD + 128:(h0 + 1) * B_QK_PAD] = jnp.where(lo, rn, 0.0).astype(BF16)
        bq_ref[0, :, (h0 + 1) * B_QK_PAD + 128:(h0 + 2) * B_QK_PAD] = jnp.where(lo, 0.0, rn).astype(BF16)

    pmix = proj(COL_BCKV, 512)
    ckv = _rms(pmix[:, 0:256], gkvl_ref[...]).astype(BF16)
    kv = jnp.dot(ckv, wukv_ref[...], preferred_element_type=F32)
    krn = rope_b(_rms(pmix[:, 256:384], gkr_ref[...]))
    kr_even = jnp.where(lo, krn, 0.0).astype(BF16)
    kr_odd = jnp.where(lo, 0.0, krn).astype(BF16)
    for hh in range(B_HEADS):
        x = kv[:, hh * 128:(hh + 1) * 128]
        bk_ref[0, :, hh * B_QK_PAD:hh * B_QK_PAD + 128] = _rms(x, gkn_ref[...]).astype(BF16)
        bk_ref[0, :, hh * B_QK_PAD + 128:(hh + 1) * B_QK_PAD] = kr_even if hh % 2 == 0 else kr_odd
    bv_ref[0] = kv[:, 512:1024].astype(BF16)

    z = jnp.dot(pmix[:, 384:512].astype(BF16), wgk_ref[...].astype(BF16),
                preferred_element_type=F32) + bgk_ref[...]
    gd_ref[0] = (jnp.minimum(z, 0.0) - jnp.log(1.0 + jnp.exp(-jnp.abs(z)))) * (1.0 / C_GATE_NORM)
    pqk = proj(COL_CQ, 512)
    cq_ref[0] = pqk[:, 0:256]
    ck_ref[0] = pqk[:, 256:512]
    cv_ref[0] = proj(COL_CV, 512).astype(BF16)
    cg_ref[0] = proj(COL_CG, 512)


def _proj(x, mod, g1, tabs, lw, layer, ctx):
    bsz, n, d = x.shape
    tm = min(512, n)
    rope = not ctx
    row = lambda b, i: (b, i, 0)
    once = pl.Buffered(1)
    in_specs = [pl.BlockSpec((1, tm, d), row),
                pl.BlockSpec((1, 6, d), _mod_index(ctx)),
                pl.BlockSpec((None, 1, d), lambda b, i: (layer, 0, 0)),
                pl.BlockSpec((None, d, P_COLS), lambda b, i: (layer, 0, 0), pipeline_mode=once)]
    args = [x, mod, g1, lw["w_in"]]
    if rope:
        in_specs += [pl.BlockSpec((tm, 128), lambda b, i: (i, 0))] * 4
        args += list(tabs)
    small = [lw["a_q_norm"], lw["a_k_norm"], lw["b_q_lora_norm"], lw["b_kv_lora_norm"],
             lw["b_q_nope_norm"], lw["b_k_nope_norm"], lw["b_q_rope_norm2"], lw["b_k_rope_norm2"],
             lw["w_uq"], lw["w_ukv"], lw["w_gk"], lw["b_gk"]]
    in_specs += [pl.BlockSpec((None,) + a.shape[1:], lambda b, i: (layer, 0, 0), pipeline_mode=once) for a in small]
    args += small
    out_shape = [jax.ShapeDtypeStruct((bsz, n, 1024), BF16),
                 jax.ShapeDtypeStruct((bsz, n, 256), BF16),
                 jax.ShapeDtypeStruct((bsz, n, 256), BF16),
                 jax.ShapeDtypeStruct((bsz, n, B_HEADS * B_QK_PAD), BF16),
                 jax.ShapeDtypeStruct((bsz, n, B_HEADS * B_QK_PAD), BF16),
                 jax.ShapeDtypeStruct((bsz, n, B_HEADS * B_V), BF16),
                 jax.ShapeDtypeStruct((bsz, n, 512), F32),
                 jax.ShapeDtypeStruct((bsz, n, 256), F32),
                 jax.ShapeDtypeStruct((bsz, n, 256), F32),
                 jax.ShapeDtypeStruct((bsz, n, 512), BF16),
                 jax.ShapeDtypeStruct((bsz, n, 512), F32)]
    out_specs = [pl.BlockSpec((1, tm, s.shape[2]), row) for s in out_shape]
    return pl.pallas_call(
        functools.partial(_proj_kernel, rope=rope),
        out_shape=out_shape,
        grid=(bsz, n // tm),
        in_specs=in_specs,
        out_specs=out_specs,
        compiler_params=_cparams(("parallel", "parallel")),
        name="proj_ctx" if ctx else "proj_lat",
    )(*args)


def _attn_kernel(*refs, groups, dq, dv, tk, rsub, has_lat, n_cast):
    if has_lat:
        q_ref, kc_ref, vc_ref, kl_ref, vl_ref = refs[:5]
        refs = refs[5:]
    else:
        q_ref, kc_ref, vc_ref = refs[:3]
        refs = refs[3:]
    cast_in, refs = refs[:n_cast], refs[n_cast:]
    o_ref, cast_out = refs[0], refs[1:1 + n_cast]
    q_scr, s_scr, p_scr, m_scr, l_scr, acc_scr = refs[1 + n_cast:]
    for src_ref, dst_ref in zip(cast_in, cast_out):
        dst_ref[...] = src_ref[...].astype(BF16)
    tq = q_ref.shape[1]
    m_rows = groups * tq
    nt = (((1,), (1,)), ((), ()))
    for g in range(groups):
        q_scr[g * tq:(g + 1) * tq, :] = q_ref[0, :, g * dq:(g + 1) * dq]
    m_scr[...] = jnp.full(m_scr.shape, -jnp.inf, F32)
    l_scr[...] = jnp.zeros(l_scr.shape, F32)
    acc_scr[...] = jnp.zeros(acc_scr.shape, F32)

    def scores(slot, k):
        s_scr[slot, :, 0:k.shape[0]] = lax.dot_general(q_scr[...], k, nt, preferred_element_type=F32)

    def softmax_pv(slot, v):
        w = v.shape[0]
        for r in range(m_rows // rsub):
            rs = slice(r * rsub, (r + 1) * rsub)
            cols = [s_scr[slot, rs, j * 128:(j + 1) * 128] for j in range(w // 128)]
            mx = functools.reduce(jnp.maximum, cols)
            m_old = m_scr[rs, :]
            m_new = jnp.maximum(m_old, jnp.max(mx, axis=-1, keepdims=True))
            ps = [jnp.exp2(c - m_new) for c in cols]
            a = jnp.exp2(m_old - m_new)
            l_scr[rs, :] = a * l_scr[rs, :]
            m_scr[rs, :] = m_new
            acc_scr[rs, :] = a * acc_scr[rs, :]
            for j, p in enumerate(ps):
                p_scr[rs, j * 128:(j + 1) * 128] = p.astype(BF16)
        v_ones = jnp.concatenate([v, jnp.ones((w, 128), BF16)], axis=1)
        pv = jnp.dot(p_scr[:, 0:w], v_ones, preferred_element_type=F32)
        acc_scr[...] += pv[:, 0:dv]
        l_scr[...] += pv[:, dv:]

    chunks = []
    if has_lat:
        chunks += [(kl_ref, vl_ref, slice(c * tk, (c + 1) * tk)) for c in range(kl_ref.shape[1] // tk)]
    chunks.append((kc_ref, vc_ref, slice(0, kc_ref.shape[1])))
    scores(0, chunks[0][0][0, chunks[0][2], :])
    for n, (_, v_ref, rows) in enumerate(chunks):
        if n + 1 < len(chunks):
            k_next, _, rows_next = chunks[n + 1]
            scores((n + 1) % 2, k_next[0, rows_next, :])
        softmax_pv(n % 2, v_ref[0, rows, :])
    for g in range(groups):
        gs = slice(g * tq, (g + 1) * tq)
        o_ref[0, :, g * dv:(g + 1) * dv] = (acc_scr[gs, :] * (1.0 / l_scr[gs, :])).astype(o_ref.dtype)


def _attention(q, kc, vc, kl, vl, *, kv_heads, groups, dq, dv, tq, tk=1024, rsub=64, cast=(), layer=0):
    assert dv == 128, "the accumulator rescale reuses the 128-lane replicated running max"
    bsz, n, _ = q.shape
    nc = kc.shape[1]
    has_lat = kl is not None
    tq = min(tq, n)
    m_rows = groups * tq
    nq = n // tq
    steps = bsz * kv_heads * nq
    wmax = max(nc, tk) if has_lat else nc
    in_specs = [pl.BlockSpec((1, tq, groups * dq), lambda b, h, i: (b, i, h)),
                pl.BlockSpec((1, nc, dq), lambda b, h, i: (b, 0, h)),
                pl.BlockSpec((1, nc, dv), lambda b, h, i: (b, 0, h))]
    args = [q, kc, vc]
    if has_lat:
        nl = kl.shape[1]
        in_specs += [pl.BlockSpec((1, nl, dq), lambda b, h, i: (b, 0, h)),
                     pl.BlockSpec((1, nl, dv), lambda b, h, i: (b, 0, h))]
        args += [kl, vl]
    out_shape = [jax.ShapeDtypeStruct((bsz, n, kv_heads * groups * dv), BF16)]
    out_specs = [pl.BlockSpec((1, tq, groups * dv), lambda b, h, i: (b, i, h))]
    def slab_specs(w):
        share = next(k for k in (1, 2, 4, 8) if w.shape[1] * k % (steps * 16) == 0)
        rows, cols = w.shape[1] * share // steps, w.shape[2]
        slab = lambda b, h, i: ((b * kv_heads + h) * nq + i) // share
        return (pl.BlockSpec((None, rows, cols), lambda b, h, i: (layer, slab(b, h, i), 0)),
                pl.BlockSpec((rows, cols), lambda b, h, i: (slab(b, h, i), 0)))

    for w in cast:
        spec_in, spec_out = slab_specs(w)
        in_specs.append(spec_in)
        out_specs.append(spec_out)
        out_shape.append(jax.ShapeDtypeStruct(w.shape[1:], BF16))
        args.append(w)
    outs = pl.pallas_call(
        functools.partial(_attn_kernel, groups=groups, dq=dq, dv=dv, tk=tk, rsub=rsub, has_lat=has_lat,
                          n_cast=len(cast)),
        out_shape=out_shape,
        grid=(bsz, kv_heads, nq),
        in_specs=in_specs,
        out_specs=out_specs,
        scratch_shapes=[pltpu.VMEM((m_rows, dq), BF16), pltpu.VMEM((2, m_rows, wmax), F32),
                        pltpu.VMEM((m_rows, wmax), BF16), pltpu.VMEM((m_rows, 128), F32),
                        pltpu.VMEM((m_rows, 128), F32), pltpu.VMEM((m_rows, dv), F32)],
        compiler_params=_cparams(("parallel", "parallel", "arbitrary")),
        name="attention",
    )(*args)
    return outs if cast else outs[0]


def _gla_kernel(q_ref, k_ref, v_ref, g_ref, s0_ref, o_ref, sfin_ref, st_ref, att_ref, cb_ref, kk_ref, *, chunk):
    d = pl.program_id(0)
    i = pl.program_id(2)
    t_rows = q_ref.shape[1]
    nch = t_rows // chunk

    @pl.when(i == 0)
    def _():
        st_ref[...] = s0_ref[0, 0]

    row = lax.broadcasted_iota(jnp.int32, (chunk, chunk), 0)
    col = lax.broadcasted_iota(jnp.int32, (chunk, chunk), 1)
    tri = jnp.where(d == 0, col - row, row - col) <= 0
    tri_bf = jnp.where(tri, 1.0, 0.0).astype(BF16)
    row2 = lax.broadcasted_iota(jnp.int32, (chunk, 2 * chunk), 0)
    lane2 = lax.broadcasted_iota(jnp.int32, (chunk, 2 * chunk), 1)
    col2 = jnp.where(lane2 >= chunk, lane2 - chunk, lane2)
    tri2 = jnp.where(d == 0, col2 - row2, row2 - col2) <= 0
    lo = lax.broadcasted_iota(jnp.int32, (chunk, 128), 1) < C_DK
    vlo = lax.broadcasted_iota(jnp.int32, (chunk, 2 * C_DV), 1) < C_DV
    srow = lax.broadcasted_iota(jnp.int32, (2 * C_DV, 2 * C_DK), 0) // C_DV
    scol = lax.broadcasted_iota(jnp.int32, (2 * C_DV, 2 * C_DK), 1) // C_DK
    same_head = srow == scol
    nt = (((1,), (1,)), ((), ()))
    tn = (((0,), (0,)), ((), ()))

    def intra_scores_mxu(q_p, k_p, cb_p, qe_p):
        ke_p = (k_p * jnp.exp(-cb_p)).astype(BF16)
        zk = jnp.zeros_like(ke_p)
        ke_bd = jnp.concatenate([jnp.where(lo, ke_p, zk), jnp.where(lo, zk, ke_p)], axis=0)
        return lax.dot_general(qe_p, ke_bd, nt, preferred_element_type=F32)

    def intra_scores_guarded(q_p, k_p, cb_p, qe_p):
        cb_ref[...] = cb_p
        kk_ref[...] = k_p
        att_ref[...] = jnp.zeros(att_ref.shape, F32)

        def one_source(s, carry):
            cb_s = cb_ref[pl.ds(s, 1), :]
            w = q_p * jnp.exp(jnp.minimum(cb_p - cb_s, 0.0)) * kk_ref[pl.ds(s, 1), :]
            w0 = jnp.sum(jnp.where(lo, w, 0.0), axis=-1, keepdims=True)
            w1 = jnp.sum(jnp.where(lo, 0.0, w), axis=-1, keepdims=True)
            att_ref[...] += jnp.where(lane2 == s, w0, 0.0) + jnp.where(lane2 == s + chunk, w1, 0.0)
            return carry

        lax.fori_loop(0, chunk, one_source, 0)
        return att_ref[...]

    def scan_block(intra_scores):
        for c in range(nch):
            cc = c + d * (nch - 1 - 2 * c)
            r0 = pl.multiple_of(cc * chunk, chunk)
            g = g_ref[0, pl.ds(r0, chunk), :]
            g_hi = g.astype(BF16)
            g_lo = (g - g_hi.astype(F32)).astype(BF16)
            cb = (jnp.dot(tri_bf, g_hi, preferred_element_type=F32)
                  + jnp.dot(tri_bf, g_lo, preferred_element_type=F32))
            tot = jnp.sum(g, axis=0, keepdims=True)
            q = q_ref[0, pl.ds(r0, chunk), :] * (C_DK ** -0.5)
            k = k_ref[0, pl.ds(r0, chunk), :]
            v = v_ref[0, pl.ds(r0, chunk), :]
            qe = (q * jnp.exp(cb)).astype(BF16)
            kd = (k * jnp.exp(tot - cb)).astype(BF16)
            dec = jnp.exp(tot)
            for p in range(C_HEADS // 2):
                ls = slice(p * 128, (p + 1) * 128)
                vs = slice(p * 2 * C_DV, (p + 1) * 2 * C_DV)
                qe_p, kd_p, v_p = qe[:, ls], kd[:, ls], v[:, vs]
                att = intra_scores(q[:, ls], k[:, ls], cb[:, ls], qe_p)
                att = jnp.where(tri2, att, 0.0).astype(BF16)
                zv = jnp.zeros_like(v_p)
                v_bd = jnp.concatenate([jnp.where(vlo, v_p, zv), jnp.where(vlo, zv, v_p)], axis=0)
                st = st_ref[p]
                o = (jnp.dot(att, v_bd, preferred_element_type=F32)
                     + lax.dot_general(qe_p, st.astype(BF16), nt, preferred_element_type=F32))
                o_ref[0, 0, pl.ds(r0, chunk), vs] = o
                u = lax.dot_general(v_p, kd_p, tn, preferred_element_type=F32)
                st_ref[p] = dec[:, ls] * st + jnp.where(same_head, u, 0.0)

    worst = functools.reduce(jnp.maximum, [
        jnp.max(-jnp.sum(g_ref[0, c * chunk:(c + 1) * chunk, :], axis=0, keepdims=True)) for c in range(nch)])
    safe = worst <= GLA_SAFE_DECAY

    @pl.when(safe)
    def _():
        scan_block(intra_scores_mxu)

    @pl.when(jnp.logical_not(safe))
    def _():
        scan_block(intra_scores_guarded)

    @pl.when(i == pl.num_programs(2) - 1)
    def _():
        sfin_ref[0, 0] = st_ref[...]


def _gla(cq, ck, cv, gdec, s0):
    bsz, n, _ = cq.shape
    t_rows = min(1024, n)
    nb = n // t_rows
    rb = lambda d, i: i + d * (nb - 1 - 2 * i)
    o, sfin = pl.pallas_call(
        functools.partial(_gla_kernel, chunk=GLA_CHUNK),
        out_shape=[jax.ShapeDtypeStruct((2, bsz, n, C_HEADS * C_DV), F32),
                   jax.ShapeDtypeStruct(s0.shape, F32)],
        grid=(2, bsz, nb),
        in_specs=[pl.BlockSpec((1, t_rows, 256), lambda d, b, i: (b, rb(d, i), 0)),
                  pl.BlockSpec((1, t_rows, 256), lambda d, b, i: (b, rb(d, i), 0)),
                  pl.BlockSpec((1, t_rows, 512), lambda d, b, i: (b, rb(d, i), 0)),
                  pl.BlockSpec((1, t_rows, 256), lambda d, b, i: (b, rb(d, i), d)),
                  pl.BlockSpec((1, 1, 2, 256, 128), lambda d, b, i: (d, b, 0, 0, 0))],
        out_specs=[pl.BlockSpec((1, 1, t_rows, 512), lambda d, b, i: (d, b, rb(d, i), 0)),
                   pl.BlockSpec((1, 1, 2, 256, 128), lambda d, b, i: (d, b, 0, 0, 0))],
        scratch_shapes=[pltpu.VMEM((2, 256, 128), F32), pltpu.VMEM((GLA_CHUNK, 2 * GLA_CHUNK), F32),
                        pltpu.VMEM((GLA_CHUNK, 128), F32), pltpu.VMEM((GLA_CHUNK, 128), F32)],
        compiler_params=_cparams(("parallel", "parallel", "arbitrary")),
        name="gla",
    )(cq, ck, cv, gdec, s0)
    return o, sfin


def _outproj_kernel(oa_ref, ob_ref, oc_ref, cg_ref, gn_ref, w_ref, x_ref, mod_ref, o_ref, z_ref):
    z_ref[:, 0:1024] = oa_ref[0]
    z_ref[:, 1024:1536] = ob_ref[0]
    for h in range(C_HEADS):
        hs = slice(h * C_DV, (h + 1) * C_DV)
        oc = oc_ref[0, 0, :, hs] + oc_ref[1, 0, :, hs]
        z_ref[:, 1536 + h * C_DV:1536 + (h + 1) * C_DV] = (
            _rms(oc, gn_ref[...]) * _silu(cg_ref[0, :, hs])).astype(BF16)
    y = jnp.dot(z_ref[...], w_ref[...], preferred_element_type=F32)
    o_ref[0] = x_ref[0] + mod_ref[0, 2:3, :] * y


def _outproj(oa, ob, oc, cg, gn, w, x, mod, layer, ctx):
    bsz, n, d = x.shape
    tm = min(512, n)
    return pl.pallas_call(
        _outproj_kernel,
        out_shape=jax.ShapeDtypeStruct((bsz, n, d), F32),
        grid=(bsz, n // tm),
        in_specs=[pl.BlockSpec((1, tm, 1024), lambda b, i: (b, i, 0)),
                  pl.BlockSpec((1, tm, 512), lambda b, i: (b, i, 0)),
                  pl.BlockSpec((2, 1, tm, 512), lambda b, i: (0, b, i, 0)),
                  pl.BlockSpec((1, tm, 512), lambda b, i: (b, i, 0)),
                  pl.BlockSpec((None, 1, C_DV), lambda b, i: (layer, 0, 0)),
                  pl.BlockSpec(w.shape, lambda b, i: (0, 0), pipeline_mode=pl.Buffered(1)),
                  pl.BlockSpec((1, tm, d), lambda b, i: (b, i, 0)),
                  pl.BlockSpec((1, 6, d), _mod_index(ctx))],
        out_specs=pl.BlockSpec((1, tm, d), lambda b, i: (b, i, 0)),
        scratch_shapes=[pltpu.VMEM((tm, w.shape[0]), BF16)],
        compiler_params=_cparams(("parallel", "parallel")),
        name="outproj",
    )(oa, ob, oc, cg, gn, w, x, mod)


def _ffn_kernel(x_ref, mod_ref, g_ref, wg_ref, wu_ref, wd_ref, o_ref, h_ref):
    j = pl.program_id(2)

    @pl.when(j == 0)
    def _():
        y = _rms(x_ref[0], g_ref[...])
        h_ref[...] = (y * (1.0 + mod_ref[0, 4:5, :]) + mod_ref[0, 3:4, :]).astype(BF16)
        o_ref[0] = jnp.zeros(o_ref.shape[1:], F32)

    h = h_ref[...]
    a = jnp.dot(h, wg_ref[...], preferred_element_type=F32)
    u = jnp.dot(h, wu_ref[...], preferred_element_type=F32)
    t = (_silu(a) * u).astype(BF16)
    o_ref[0] += jnp.dot(t, wd_ref[...], preferred_element_type=F32)

    @pl.when(j == pl.num_programs(2) - 1)
    def _():
        o_ref[0] = x_ref[0] + mod_ref[0, 5:6, :] * o_ref[0]


def _ffn(x, mod, g, wg, wu, wd, layer, ctx):
    bsz, n, d = x.shape
    f = wg.shape[1]
    tm = min(512, n)
    tf = 512
    return pl.pallas_call(
        _ffn_kernel,
        out_shape=jax.ShapeDtypeStruct((bsz, n, d), F32),
        grid=(bsz, n // tm, f // tf),
        in_specs=[pl.BlockSpec((1, tm, d), lambda b, i, j: (b, i, 0)),
                  pl.BlockSpec((1, 6, d), _mod_index(ctx)),
                  pl.BlockSpec((None, 1, d), lambda b, i, j: (layer, 0, 0)),
                  pl.BlockSpec((d, tf), lambda b, i, j: (0, j)),
                  pl.BlockSpec((d, tf), lambda b, i, j: (0, j)),
                  pl.BlockSpec((tf, d), lambda b, i, j: (j, 0))],
        out_specs=pl.BlockSpec((1, tm, d), lambda b, i, j: (b, i, 0)),
        scratch_shapes=[pltpu.VMEM((tm, d), BF16)],
        compiler_params=_cparams(("parallel", "parallel", "arbitrary")),
        name="ffn",
    )(x, mod, g, wg, wu, wd)


def _rope_tables(n):
    pos = np.arange(n)
    row = (pos // GRID_W).astype(np.float64)[:, None]
    col = (pos % GRID_W).astype(np.float64)[:, None]

    def table(half):
        inv = ROPE_THETA ** (-np.arange(half, dtype=np.float64) / half)
        ar, ac = row * inv[None, :], col * inv[None, :]
        cos = np.concatenate([np.cos(ar), np.cos(ar), np.cos(ac), np.cos(ac)], axis=1)
        sin = np.concatenate([-np.sin(ar), np.sin(ar), -np.sin(ac), np.sin(ac)], axis=1)
        return cos, sin

    cos_a, sin_a = table(A_HEAD_DIM // 4)
    cos_b, sin_b = table(B_ROPE // 4)
    cos_b, sin_b = np.tile(cos_b, (1, 2)), np.tile(sin_b, (1, 2))
    return tuple(jnp.asarray(t, F32) for t in (cos_a, sin_a, cos_b, sin_b))


def _prepare_weights(w_in, w_uq, w_ukv, w_gk_f, b_gk_f, w_gk_b, b_gk_b, smalls):
    depth, d, _ = w_in.shape
    w_r = jnp.concatenate([
        w_in[:, :, 0:2304],
        w_in[:, :, 2304:2368], w_in[:, :, 2304:2368],
        w_in[:, :, 3904:3936], jnp.zeros((depth, d, 96), w_in.dtype),
        w_in[:, :, 2368:3904],
    ], axis=2).astype(BF16)
    uq = w_uq.reshape(depth, B_Q_RANK, B_HEADS, B_NOPE + B_ROPE)
    uq = jnp.concatenate([uq[..., :B_NOPE].reshape(depth, B_Q_RANK, -1),
                          uq[..., B_NOPE:].reshape(depth, B_Q_RANK, -1)], axis=2)
    ukv = w_ukv.reshape(depth, B_KV_RANK, B_HEADS, B_NOPE + B_V)
    ukv = jnp.concatenate([ukv[..., :B_NOPE].reshape(depth, B_KV_RANK, -1),
                           ukv[..., B_NOPE:].reshape(depth, B_KV_RANK, -1)], axis=2)
    nk = C_HEADS * C_DK
    w_gk = jnp.zeros((depth, 128, 2 * nk), F32)
    w_gk = w_gk.at[:, 0:C_GATE_RANK, 0:nk].set(w_gk_f).at[:, C_GATE_RANK:2 * C_GATE_RANK, nk:].set(w_gk_b)
    lw = {k: v.reshape(depth, 1, -1) for k, v in smalls.items()}
    lw["b_q_rope_norm2"] = jnp.tile(lw["b_q_rope_norm"], (1, 1, 2))
    lw["b_k_rope_norm2"] = jnp.tile(lw["b_k_rope_norm"], (1, 1, 2))
    lw.update(w_in=w_r, w_uq=uq.astype(BF16), w_ukv=ukv.astype(BF16), w_gk=w_gk,
              b_gk=jnp.concatenate([b_gk_f, b_gk_b], axis=1).reshape(depth, 1, -1))
    return lw


def kernel(x, c, ctx, c_ctx, w_mod, b_mod, norm1_g, norm2_g, w_in, a_q_norm, a_k_norm, b_q_lora_norm, b_kv_lora_norm, w_uq, w_ukv, b_q_nope_norm, b_k_nope_norm, b_q_rope_norm, b_k_rope_norm, w_gk_f, b_gk_f, w_gk_b, b_gk_b, c_out_norm, w_out, w_gate, w_up, w_down):
    bsz, n, d = x.shape
    depth = w_in.shape[0]
    tabs = _rope_tables(n)
    assert bsz + 1 <= 8, "the modulation kernel holds the context row and all batch rows in one 8-row tile"
    cpad = jnp.concatenate([c_ctx[None, :], c, jnp.zeros((8 - bsz - 1, d), F32)], axis=0)
    smalls = dict(a_q_norm=a_q_norm, a_k_norm=a_k_norm, b_q_lora_norm=b_q_lora_norm,
                  b_kv_lora_norm=b_kv_lora_norm, b_q_nope_norm=b_q_nope_norm, b_k_nope_norm=b_k_nope_norm,
                  b_q_rope_norm=b_q_rope_norm, b_k_rope_norm=b_k_rope_norm, c_out_norm=c_out_norm,
                  norm1_g=norm1_g, norm2_g=norm2_g)
    lw = _prepare_weights(w_in, w_uq, w_ukv, w_gk_f, b_gk_f, w_gk_b, b_gk_b, smalls)
    ffn_out_w = (w_gate, w_up, w_down, w_out)
    b_mod3 = b_mod.reshape(depth, 1, -1)
    s_zero = jnp.zeros((2, bsz, 2, 2 * C_DV, 2 * C_DK), F32)
    a_args = dict(kv_heads=A_KV_HEADS, groups=A_HEADS // A_KV_HEADS, dq=A_HEAD_DIM, dv=A_HEAD_DIM)
    b_args = dict(kv_heads=B_HEADS, groups=1, dq=B_QK_PAD, dv=B_V)
    xc = ctx
    for l in range(depth):
        ctx_out = l < depth - 1
        mod = _modulation(cpad, w_mod, b_mod3, l)
        caq, cak, cav, cbq, cbk, cbv, cgd, ccq, cck, ccv, ccg = _proj(xc, mod, lw["norm1_g"], None, lw, l, True)
        laq, lak, lav, lbq, lbk, lbv, lgd, lcq, lck, lcv, lcg = _proj(x, mod, lw["norm1_g"], tabs, lw, l, False)

        o_a, wg, wu, wd, wo = _attention(laq, cak, cav, lak, lav, tq=512, cast=ffn_out_w, layer=l, **a_args)
        o_b = _attention(lbq, cbk, cbv, lbk, lbv, tq=1024, **b_args)
        oc_c, s_ctx = _gla(ccq, cck, ccv, cgd, s_zero)
        o_c, _ = _gla(lcq, lck, lcv, lgd, s_ctx)

        x = _outproj(o_a, o_b, o_c, lcg, lw["c_out_norm"], wo, x, mod, l, False)
        x = _ffn(x, mod, lw["norm2_g"], wg, wu, wd, l, False)
        if ctx_out:
            oc_a = _attention(caq, cak, cav, None, None, tq=256, **a_args)
            oc_b = _attention(cbq, cbk, cbv, None, None, tq=256, **b_args)
            xc = _outproj(oc_a, oc_b, oc_c, ccg, lw["c_out_norm"], wo, xc, mod, l, True)
            xc = _ffn(xc, mod, lw["norm2_g"], wg, wu, wd, l, True)
    return x
```
